```python
import math
import jax, jax.numpy as jnp
from jax import lax
import numpy as np

D_MODEL = 1024
BATCH = 32
SEQ = 2048
DEPTH = 4

GRID_W = 64
CTX_LEN = 256
HEAD_DIM = 64
Q_BLOCK = 128
EPS = 1e-6
ROPE_THETA = 10000.0
NEG_INF = -1e30

NA_HEADS = 4
NA_WIN_R = 8
NA_WIN_C = 16
NA_W = NA_HEADS * HEAD_DIM
DIFF_HEADS = 4
DIFF_DH = HEAD_DIM // 2
DIFF_W = DIFF_HEADS * 2 * DIFF_DH
DIFF_LAMBDA_STD = 0.1
GQA_HEADS = 4
GQA_KV_HEADS = 2
GQA_GROUP = GQA_HEADS // GQA_KV_HEADS
GQA_Q_W = GQA_HEADS * HEAD_DIM
GQA_KV_W = GQA_KV_HEADS * HEAD_DIM
MLA_HEADS = 4
MLA_Q_RANK = 192
MLA_KV_RANK = 128
MLA_NOPE = 64
MLA_ROPE = 32
MLA_V = 64
MLA_QK = MLA_NOPE + MLA_ROPE

N_BRANCH = 4
BRANCH_W = NA_W
IN_SIZES = (NA_W, NA_W, NA_W, DIFF_W, DIFF_W, DIFF_W, GQA_Q_W, GQA_KV_W, GQA_KV_W, MLA_Q_RANK, MLA_KV_RANK, MLA_ROPE)
IN_W = sum(IN_SIZES)
N_MOD = 6
D_FF = 2816
CONV_W = 3

kernel_name = "hybrid_parallel_dit_block"


def rms_norm(x, g):
    xf = x.astype(jnp.float32)
    y = xf * lax.rsqrt(jnp.mean(xf * xf, axis=-1, keepdims=True) + EPS)
    return (y * g.astype(jnp.float32)).astype(x.dtype)


def axial_rope(n_tokens, rot_dim):
    t = jnp.arange(n_tokens)
    row = (t // GRID_W).astype(jnp.float32)
    col = (t % GRID_W).astype(jnp.float32)
    n_axis = rot_dim // 4
    inv_freq = ROPE_THETA ** (-jnp.arange(n_axis, dtype=jnp.float32) / n_axis)
    ang = jnp.concatenate([row[:, None] * inv_freq, col[:, None] * inv_freq], axis=-1)
    return jnp.cos(ang), jnp.sin(ang)


def apply_rope(x, rope):
    cos, sin = rope
    half = x.shape[-1] // 2
    shp = (1, x.shape[1]) + (1,) * (x.ndim - 3) + (half,)
    c = cos.reshape(shp).astype(x.dtype)
    s = sin.reshape(shp).astype(x.dtype)
    x1, x2 = x[..., :half], x[..., half:]
    return jnp.concatenate([x1 * c - x2 * s, x1 * s + x2 * c], axis=-1)


def split_cols(p):
    outs, o = [], 0
    for n in IN_SIZES:
        outs.append(p[..., o:o + n])
        o += n
    return outs


def sweep_query_blocks(fn, q):
    b, s = q.shape[:2]
    nb = s // Q_BLOCK
    qb = jnp.moveaxis(q.reshape((b, nb, Q_BLOCK) + q.shape[2:]), 1, 0)
    out = lax.map(fn, qb)
    return jnp.moveaxis(out, 0, 1).reshape((b, s) + out.shape[3:])


def softmax_attention(q, k, v, scale):
    sc = jnp.einsum("bqhgd,bthd->bhgqt", q, k).astype(jnp.float32) * scale
    p = jax.nn.softmax(sc, axis=-1).astype(v.dtype)
    return jnp.einsum("bhgqt,bthd->bqhgd", p, v)


def neighborhood_attention(q, k, v, k_ctx, v_ctx, rel_bias, rows):
    b, s, h, d = q.shape
    win_r = min(NA_WIN_R, rows)
    scale = d ** -0.5
    qg = jnp.moveaxis(q.reshape(b, rows, GRID_W, h, d), 1, 0)
    kg = k.reshape(b, rows, GRID_W, h, d)
    vg = v.reshape(b, rows, GRID_W, h, d)
    q_rows = jnp.arange(rows)
    row_start = jnp.clip(q_rows - win_r // 2, 0, rows - win_r)
    cols = jnp.arange(GRID_W)
    col_start = jnp.clip(cols - NA_WIN_C // 2, 0, GRID_W - NA_WIN_C)
    col_in = (cols[None, :] >= col_start[:, None]) & (cols[None, :] < col_start[:, None] + NA_WIN_C)
    col_idx = jnp.clip(cols[None, :] - cols[:, None], -(NA_WIN_C - 1), NA_WIN_C - 1) + NA_WIN_C - 1
    n_band = win_r * GRID_W

    def one_row(args):
        r, rs, q_row = args
        k_band = lax.dynamic_slice_in_dim(kg, rs, win_r, axis=1)
        v_band = lax.dynamic_slice_in_dim(vg, rs, win_r, axis=1)
        row_idx = rs + jnp.arange(win_r) - r + NA_WIN_R - 1
        bias = rel_bias[:, row_idx[None, :, None], col_idx[:, None, :]]
        s_nb = jnp.einsum("bqhd,bikhd->bhqik", q_row, k_band).astype(jnp.float32) * scale + bias.astype(jnp.float32)
        s_nb = jnp.where(col_in[:, None, :], s_nb, NEG_INF)
        s_cx = jnp.einsum("bqhd,blhd->bhql", q_row, k_ctx).astype(jnp.float32) * scale
        p = jax.nn.softmax(jnp.concatenate([s_nb.reshape(b, h, GRID_W, n_band), s_cx], axis=-1), axis=-1).astype(v.dtype)
        p_nb = p[..., :n_band].reshape(b, h, GRID_W, win_r, GRID_W)
        return (jnp.einsum("bhqik,bikhd->bqhd", p_nb, v_band)
                + jnp.einsum("bhql,blhd->bqhd", p[..., n_band:], v_ctx))

    out = lax.map(one_row, (q_rows, row_start, qg))
    return jnp.moveaxis(out, 0, 1).reshape(b, s, h * d)


def branch_neighborhood(p, pc, rows, need_ctx, qk_g, rel_bias):
    q, k, v = [t.reshape(t.shape[:2] + (NA_HEADS, HEAD_DIM)) for t in p]
    qc, kc, vc = [t.reshape(t.shape[:2] + (NA_HEADS, HEAD_DIM)) for t in pc]
    q, k = rms_norm(q, qk_g[0]), rms_norm(k, qk_g[1])
    qc, kc = rms_norm(qc, qk_g[0]), rms_norm(kc, qk_g[1])
    y = neighborhood_attention(q, k, v, kc, vc, rel_bias, rows)
    yc = None
    if need_ctx:
        yc = softmax_attention(qc[:, :, :, None], kc, vc, HEAD_DIM ** -0.5).reshape(qc.shape[:2] + (NA_W,))
    return y, yc


def branch_differential(p, pc, rope, need_ctx, qk_g, lam_p, subln_g, lambda_init):
    def qkv(parts, rope_):
        q, k, v = parts
        q = rms_norm(q.reshape(q.shape[:2] + (DIFF_HEADS, 2, DIFF_DH)), qk_g[0])
        k = rms_norm(k.reshape(k.shape[:2] + (DIFF_HEADS, 2, DIFF_DH)), qk_g[1])
        v = v.reshape(v.shape[:2] + (DIFF_HEADS, 2 * DIFF_DH))
        if rope_ is not None:
            q, k = apply_rope(q, rope_), apply_rope(k, rope_)
        return q, k, v

    q, k, v = qkv(p, rope)
    qc, kc, vc = qkv(pc, None)
    lp = lam_p.astype(jnp.float32)
    lam = jnp.exp(jnp.sum(lp[0] * lp[1])) - jnp.exp(jnp.sum(lp[2] * lp[3])) + lambda_init
    scale = DIFF_DH ** -0.5

    def core(qb, k_, v_):
        sc = jnp.einsum("bqhmd,bthmd->bhmqt", qb, k_).astype(jnp.float32) * scale
        pr = jax.nn.softmax(sc, axis=-1)
        a = (pr[:, :, 0] - lam * pr[:, :, 1]).astype(v_.dtype)
        return jnp.einsum("bhqt,bthd->bqhd", a, v_)

    def post(o):
        return (rms_norm(o, subln_g) * (1.0 - lambda_init)).reshape(o.shape[:2] + (DIFF_W,))

    k_all = jnp.concatenate([k, kc], axis=1)
    v_all = jnp.concatenate([v, vc], axis=1)
    y = post(sweep_query_blocks(lambda qb: core(qb, k_all, v_all), q))
    yc = post(core(qc, kc, vc)) if need_ctx else None
    return y, yc


def branch_gqa(p, pc, rope, need_ctx, qk_g):
    def qkv(parts, rope_):
        q, k, v = parts
        q = rms_norm(q.reshape(q.shape[:2] + (GQA_HEADS, HEAD_DIM)), qk_g[0])
        k = rms_norm(k.reshape(k.shape[:2] + (GQA_KV_HEADS, HEAD_DIM)), qk_g[1])
        v = v.reshape(v.shape[:2] + (GQA_KV_HEADS, HEAD_DIM))
        if rope_ is not None:
            q, k = apply_rope(q, rope_), apply_rope(k, rope_)
        return q.reshape(q.shape[:2] + (GQA_KV_HEADS, GQA_GROUP, HEAD_DIM)), k, v

    q, k, v = qkv(p, rope)
    qc, kc, vc = qkv(pc, None)
    scale = HEAD_DIM ** -0.5
    k_all = jnp.concatenate([k, kc], axis=1)
    v_all = jnp.concatenate([v, vc], axis=1)
    y = sweep_query_blocks(lambda qb: softmax_attention(qb, k_all, v_all, scale), q)
    y = y.reshape(y.shape[:2] + (GQA_Q_W,))
    yc = None
    if need_ctx:
        yc = softmax_attention(qc, kc, vc, scale)
        yc = yc.reshape(yc.shape[:2] + (GQA_Q_W,))
    return y, yc


def branch_mla(p, pc, rope, need_ctx, qa_g, kva_g, w_qb, w_kvb, qk_g):
    def qkv(parts, rope_):
        qa, kva, kr = parts
        bb, t = qa.shape[:2]
        q = (rms_norm(qa, qa_g) @ w_qb).reshape(bb, t, MLA_HEADS, MLA_QK)
        kv = (rms_norm(kva, kva_g) @ w_kvb).reshape(bb, t, MLA_HEADS, MLA_NOPE + MLA_V)
        q_nope = rms_norm(q[..., :MLA_NOPE], qk_g[0, :MLA_NOPE])
        q_rope = rms_norm(q[..., MLA_NOPE:], qk_g[0, MLA_NOPE:])
        k_nope = rms_norm(kv[..., :MLA_NOPE], qk_g[1, :MLA_NOPE])
        k_rope = rms_norm(kr, qk_g[1, MLA_NOPE:])
        if rope_ is not None:
            q_rope, k_rope = apply_rope(q_rope, rope_), apply_rope(k_rope, rope_)
        q = jnp.concatenate([q_nope, q_rope], axis=-1)[:, :, :, None]
        k = jnp.concatenate([k_nope, jnp.broadcast_to(k_rope[:, :, None], (bb, t, MLA_HEADS, MLA_ROPE))], axis=-1)
        return q, k, kv[..., MLA_NOPE:]

    q, k, v = qkv(p, rope)
    qc, kc, vc = qkv(pc, None)
    scale = MLA_QK ** -0.5
    k_all = jnp.concatenate([k, kc], axis=1)
    v_all = jnp.concatenate([v, vc], axis=1)
    y = sweep_query_blocks(lambda qb: softmax_attention(qb, k_all, v_all, scale), q)
    y = y.reshape(y.shape[:2] + (MLA_HEADS * MLA_V,))
    yc = None
    if need_ctx:
        yc = softmax_attention(qc, kc, vc, scale)
        yc = yc.reshape(yc.shape[:2] + (MLA_HEADS * MLA_V,))
    return y, yc


def merge_branches(h, branches, w_gate, w_branch, w_out):
    y = None
    for i in range(N_BRANCH):
        term = jax.nn.sigmoid(h @ w_gate[i]) * (branches[i] @ w_branch[i])
        y = term if y is None else y + term
    return y @ w_out


def depthwise_conv_centered(u, w, bias):
    t = u.shape[1]
    pad = CONV_W // 2
    up = jnp.pad(u, ((0, 0), (pad, pad), (0, 0)))
    out = bias
    for j in range(CONV_W):
        out = out + up[:, j:j + t] * w[j]
    return out


def conv_ffn(h, w_up, conv_w, conv_b, w_down):
    u = depthwise_conv_centered(h @ w_up, conv_w, conv_b)
    gate, val = u[..., :D_FF], u[..., D_FF:]
    return (jax.nn.silu(gate) * val) @ w_down


def setup_inputs(seed: int = 0) -> dict:
    key = jax.random.key(seed)
    ks = iter(jax.random.split(key, 40))

    def nrm(shape, scale):
        return jax.random.normal(next(ks), shape, jnp.float32) * scale

    def gain(shape):
        return 1.0 + nrm(shape, 0.02)

    L, D = DEPTH, D_MODEL
    return {
        "x": nrm((BATCH, SEQ, D), 1.0),
        "c": nrm((BATCH, D), 1.0),
        "ctx": nrm((BATCH, CTX_LEN, D), 1.0),
        "c_ctx": nrm((D,), 1.0),
        "w_mod": nrm((L, D, N_MOD * D), 0.3 * D ** -0.5),
        "b_mod": nrm((L, N_MOD * D), 0.02),
        "norm1_g": gain((L, D)),
        "norm2_g": gain((L, D)),
        "w_in": nrm((L, D, IN_W), D ** -0.5),
        "na_qk_g": gain((L, 2, HEAD_DIM)),
        "na_rel_bias": nrm((L, NA_HEADS, 2 * NA_WIN_R - 1, 2 * NA_WIN_C - 1), 0.02),
        "diff_qk_g": gain((L, 2, DIFF_DH)),
        "diff_lambda": nrm((L, 4, DIFF_DH), DIFF_LAMBDA_STD),
        "diff_subln_g": gain((L, 2 * DIFF_DH)),
        "gqa_qk_g": gain((L, 2, HEAD_DIM)),
        "mla_qa_g": gain((L, MLA_Q_RANK)),
        "mla_kva_g": gain((L, MLA_KV_RANK)),
        "w_mla_qb": nrm((L, MLA_Q_RANK, MLA_HEADS * MLA_QK), MLA_Q_RANK ** -0.5),
        "w_mla_kvb": nrm((L, MLA_KV_RANK, MLA_HEADS * (MLA_NOPE + MLA_V)), MLA_KV_RANK ** -0.5),
        "mla_qk_g": gain((L, 2, MLA_QK)),
        "w_gate": nrm((L, N_BRANCH, D, D), D ** -0.5),
        "w_branch": nrm((L, N_BRANCH, BRANCH_W, D), BRANCH_W ** -0.5),
        "w_out": nrm((L, D, D), D ** -0.5),
        "w_up": nrm((L, D, 2 * D_FF), D ** -0.5),
        "conv_w": nrm((L, CONV_W, 2 * D_FF), CONV_W ** -0.5),
        "conv_b": nrm((L, 2 * D_FF), 0.02),
        "w_down": nrm((L, D_FF, D), D_FF ** -0.5),
    }


def reference(x, c, ctx, c_ctx, w_mod, b_mod, norm1_g, norm2_g, w_in, na_qk_g, na_rel_bias,
              diff_qk_g, diff_lambda, diff_subln_g, gqa_qk_g, mla_qa_g, mla_kva_g, w_mla_qb, w_mla_kvb,
              mla_qk_g, w_gate, w_branch, w_out, w_up, conv_w, conv_b, w_down):
    b, s, d = x.shape
    rows = s // GRID_W
    rope_diff = axial_rope(s, DIFF_DH)
    rope_gqa = axial_rope(s, HEAD_DIM)
    rope_mla = axial_rope(s, MLA_ROPE)
    c_act = jax.nn.silu(c)
    cc_act = jax.nn.silu(c_ctx)
    xc = ctx
    for i in range(DEPTH):
        need_ctx = i < DEPTH - 1
        lambda_init = 0.8 - 0.6 * math.exp(-0.3 * i)
        mod = (c_act @ w_mod[i] + b_mod[i]).reshape(b, N_MOD, 1, d)
        modc = (cc_act @ w_mod[i] + b_mod[i]).reshape(N_MOD, 1, d)
        h = rms_norm(x, norm1_g[i]) * (1 + mod[:, 1]) + mod[:, 0]
        hc = rms_norm(xc, norm1_g[i]) * (1 + modc[1]) + modc[0]
        p = split_cols(h @ w_in[i])
        pc = split_cols(hc @ w_in[i])
        ya, yac = branch_neighborhood(p[0:3], pc[0:3], rows, need_ctx, na_qk_g[i], na_rel_bias[i])
        yb, ybc = branch_differential(p[3:6], pc[3:6], rope_diff, need_ctx, diff_qk_g[i], diff_lambda[i],
                                      diff_subln_g[i], lambda_init)
        yg, ygc = branch_gqa(p[6:9], pc[6:9], rope_gqa, need_ctx, gqa_qk_g[i])
        ym, ymc = branch_mla(p[9:12], pc[9:12], rope_mla, need_ctx, mla_qa_g[i], mla_kva_g[i],
                             w_mla_qb[i], w_mla_kvb[i], mla_qk_g[i])
        x = x + mod[:, 2] * merge_branches(h, (ya, yb, yg, ym), w_gate[i], w_branch[i], w_out[i])
        h2 = rms_norm(x, norm2_g[i]) * (1 + mod[:, 4]) + mod[:, 3]
        x = x + mod[:, 5] * conv_ffn(h2, w_up[i], conv_w[i], conv_b[i], w_down[i])
        if need_ctx:
            xc = xc + modc[2] * merge_branches(hc, (yac, ybc, ygc, ymc), w_gate[i], w_branch[i], w_out[i])
            hc2 = rms_norm(xc, norm2_g[i]) * (1 + modc[4]) + modc[3]
            xc = xc + modc[5] * conv_ffn(hc2, w_up[i], conv_w[i], conv_b[i], w_down[i])
    return x
```

```python
import functools
import math

import jax
import jax.numpy as jnp
from jax import lax
from jax.experimental import pallas as pl
from jax.experimental.pallas import tpu as pltpu

D_MODEL = 1024
GRID_W = 64
HEAD_DIM = 64
EPS = 1e-6
ROPE_THETA = 10000.0
NEG_INF = -1e30
NA_WIN_R = 8
NA_WIN_C = 16
DIFF_DH = 32
MLA_Q_RANK = 192
MLA_KV_RANK = 128
MLA_NOPE = 64
MLA_ROPE = 32
MLA_V = 64
MLA_QK = MLA_NOPE + MLA_ROPE
N_HEADS = 4
GQA_KV_HEADS = 2
N_MOD = 6
D_FF = 2816
IN_SIZES = (256, 256, 256, 256, 256, 256, 256, 128, 128, MLA_Q_RANK, MLA_KV_RANK, MLA_ROPE)

LANES = 128
MXU_DIM = 256
TM = 256
FF_CHUNK = 256
HALO = 8
NA_ROWS_PER_TILE = TM // GRID_W
NA_WIN_ROWS = NA_WIN_R + NA_ROWS_PER_TILE
VMEM_LIMIT = 56 * 1024 * 1024

BF16 = jnp.bfloat16
F32 = jnp.float32


def _cparams(sem):
    return pltpu.CompilerParams(dimension_semantics=sem, vmem_limit_bytes=VMEM_LIMIT)


def _dot(a, b):
    return jnp.dot(a, b, preferred_element_type=F32)


def _dot_t(a, b):
    return lax.dot_general(a, b, (((1,), (1,)), ((), ())), preferred_element_type=F32)


def _sigmoid(z):
    return 1.0 / (1.0 + jnp.exp(-z))


def _row_rmsnorm(x, g):
    ms = jnp.mean(x * x, axis=-1, keepdims=True)
    return x * lax.rsqrt(ms + EPS) * g


def _mod_kernel(c_ref, w_ref, b_ref, o_ref):
    c = c_ref[...]
    act = (c * _sigmoid(c)).astype(BF16)
    o_ref[0] = _dot(act, w_ref[0]) + b_ref[0]


def _mod_all(cc, w_mod, b_mod):
    n_layers, _, n_out = w_mod.shape
    rows = cc.shape[0]
    tn = 1024
    return pl.pallas_call(
        _mod_kernel,
        grid=(n_layers, n_out // tn),
        in_specs=[
            pl.BlockSpec((rows, D_MODEL), lambda l, j: (0, 0)),
            pl.BlockSpec((1, D_MODEL, tn), lambda l, j: (l, 0, j)),
            pl.BlockSpec((1, 1, tn), lambda l, j: (l, 0, j)),
        ],
        out_specs=pl.BlockSpec((1, rows, tn), lambda l, j: (l, 0, j)),
        out_shape=jax.ShapeDtypeStruct((n_layers, rows, n_out), F32),
        compiler_params=_cparams(("arbitrary", "arbitrary")),
        name="mod_all",
    )(cc, w_mod, b_mod)


def _group_ms(y, g_mat):
    outs = []
    for c in range(y.shape[1] // MXU_DIM):
        yc = y[:, c * MXU_DIM:(c + 1) * MXU_DIM]
        sq = yc * yc
        hi = sq.astype(BF16)
        lo = (sq - hi.astype(F32)).astype(BF16)
        outs.append(_dot(hi, g_mat) + _dot(lo, g_mat))
    return outs[0] if len(outs) == 1 else jnp.concatenate(outs, axis=1)


def _group_ms128(y, g_mat):
    sq = y * y
    hi = sq.astype(BF16)
    lo = (sq - hi.astype(F32)).astype(BF16)
    g = g_mat[:LANES, :LANES]
    return _dot(hi, g) + _dot(lo, g)


def _rope(y, cos, sin_signed, half):
    outs = []
    lane = lax.broadcasted_iota(jnp.int32, (y.shape[0], LANES), 1)
    first = jnp.bitwise_and(lane, 2 * half - 1) < half
    for c in range(y.shape[1] // LANES):
        yc = y[:, c * LANES:(c + 1) * LANES]
        fwd = pltpu.roll(yc, LANES - half, 1)
        bwd = pltpu.roll(yc, half, 1)
        partner = jnp.where(first, fwd, bwd)
        outs.append(yc * cos + partner * sin_signed)
    return outs[0] if len(outs) == 1 else jnp.concatenate(outs, axis=1)


def _store_heads(ref, y):
    for h in range(y.shape[1] // LANES):
        ref[0, h] = y[:, h * LANES:(h + 1) * LANES].astype(ref.dtype)


def _inproj_kernel(x_ref, mod_ref, g1_ref, w_ref, wqb_ref, wkvb_ref, gv_ref, gm_ref, tab_ref,
                   qna_ref, kna_ref, vna_ref, qdf_ref, kdf_ref, vdf_ref,
                   qgq_ref, kgq_ref, vgq_ref, qml_ref, kml_ref, vml_ref):
    x = x_ref[0]
    mod = mod_ref[0]
    h = (_row_rmsnorm(x, g1_ref[...]) * (1.0 + mod[1:2]) + mod[0:1]).astype(BF16)

    g_a, g_d, g_m = gm_ref[0], gm_ref[1], gm_ref[2]
    tab = tab_ref[...]
    cos_d, sin_d = tab[:, 0:128], tab[:, 128:256]
    cos_g, sin_g = tab[:, 256:384], tab[:, 384:512]
    cos_m, sin_m = tab[:, 512:640], tab[:, 640:768]

    def proj(lo, width):
        return _dot(h, w_ref[:, lo:lo + width])

    def normed(y, g_mat, row):
        return y * lax.rsqrt(_group_ms(y, g_mat) + EPS) * gv_ref[row:row + 1, :y.shape[1]]

    _store_heads(qna_ref, normed(proj(0, 512), g_a, 0))
    _store_heads(kna_ref, normed(proj(512, 512), g_a, 1))
    _store_heads(vna_ref, proj(1024, 512))
    _store_heads(qdf_ref, _rope(normed(proj(1536, 512), g_d, 2), cos_d, sin_d, DIFF_DH // 2))
    _store_heads(kdf_ref, _rope(normed(proj(2048, 512), g_d, 3), cos_d, sin_d, DIFF_DH // 2))
    _store_heads(vdf_ref, proj(2560, 512))
    _store_heads(qgq_ref, _rope(normed(proj(3072, 512), g_a, 4), cos_g, sin_g, HEAD_DIM // 2))
    _store_heads(kgq_ref, _rope(normed(proj(3584, 256), g_a, 5), cos_g, sin_g, HEAD_DIM // 2))
    _store_heads(vgq_ref, proj(3840, 256))
    qa = proj(4096, 256)
    ms_qa = jnp.sum(qa * qa, axis=-1, keepdims=True) * (1.0 / MLA_Q_RANK)
    qa_n = (qa * lax.rsqrt(ms_qa + EPS) * gv_ref[8:9, :256]).astype(BF16)
    q_ml = _dot(qa_n, wqb_ref[...])
    _store_heads(qml_ref, _rope(normed(q_ml, g_m, 6), cos_m, sin_m, MLA_ROPE // 2))
    kva = proj(4352, 128)
    kva_n = _row_rmsnorm(kva, gv_ref[9:10, :128]).astype(BF16)
    kv = _dot(kva_n, wkvb_ref[...])
    k_nope = normed(kv[:, :512], g_m, 7)
    kr = proj(4480, 128)
    kr_n = kr * lax.rsqrt(_group_ms128(kr, g_m) + EPS) * gv_ref[10:11, :128]
    kr_r = _rope(kr_n, cos_m, sin_m, MLA_ROPE // 2)
    _store_heads(kml_ref, k_nope + jnp.concatenate([kr_r] * N_HEADS, axis=1))
    _store_heads(vml_ref, kv[:, 512:])


def _inproj(x_all, mods, g1, lw, tables):
    bsz, t_all, _ = x_all.shape
    nt = t_all // TM
    ctx_tile = nt - 1
    ctx_row = mods.shape[0] - HALO

    def head_spec(nh):
        return pl.BlockSpec((1, nh, TM, LANES), lambda t, b: (b, 0, t, 0))

    def head_shape(nh):
        return jax.ShapeDtypeStruct((bsz, nh, t_all, LANES), BF16)

    def full(a):
        nd = a.ndim
        return pl.BlockSpec(a.shape, lambda t, b: (0,) * nd)

    heads = (4, 4, 4, 4, 4, 4, 4, GQA_KV_HEADS, GQA_KV_HEADS, 4, 4, 4)
    return pl.pallas_call(
        _inproj_kernel,
        grid=(nt, bsz),
        in_specs=[
            pl.BlockSpec((1, TM, D_MODEL), lambda t, b: (b, t, 0)),
            pl.BlockSpec((1, N_MOD, D_MODEL), lambda t, b: (jnp.where(t == ctx_tile, ctx_row, b), 0, 0)),
            full(g1), full(lw["w_all"]), full(lw["w_qb"]), full(lw["w_kvb"]), full(lw["gvec"]), full(lw["gmat"]),
            pl.BlockSpec((TM, tables.shape[1]), lambda t, b: (t, 0)),
        ],
        out_specs=[head_spec(n) for n in heads],
        out_shape=[head_shape(n) for n in heads],
        compiler_params=_cparams(("arbitrary", "arbitrary")),
        name="inproj",
    )(x_all, mods, g1, lw["w_all"], lw["w_qb"], lw["w_kvb"], lw["gvec"], lw["gmat"], tables)


def _softmax_pv(s, v):
    m = jnp.max(s, axis=-1, keepdims=True)
    p = jnp.exp(s - m)
    l = jnp.sum(p, axis=-1, keepdims=True)
    return _dot(p.astype(BF16), v) * (1.0 / l)


def _pack_heads(o_scr, o_ref):
    o_ref[0] = jnp.concatenate([o_scr[h][:, :HEAD_DIM] for h in range(N_HEADS)], axis=1).astype(o_ref.dtype)


def _attn_kernel(mode, seq, kv_shift, lambda_init, q_ref, k_ref, v_ref, *rest):
    if mode == "na":
        bias_ref, o_ref, o_scr = rest
    elif mode == "diff":
        lam_ref, sub_ref, o_ref, o_scr = rest
    else:
        o_ref, o_scr = rest
    qi = pl.program_id(1)
    t_all = k_ref.shape[2]
    n_lat_tiles = seq // TM

    if mode == "diff":
        lp = lam_ref[...]
        lam = (jnp.exp(jnp.sum(lp[0:1] * lp[1:2], axis=-1, keepdims=True))
               - jnp.exp(jnp.sum(lp[2:3] * lp[3:4], axis=-1, keepdims=True)) + lambda_init)
        lane = lax.broadcasted_iota(jnp.int32, (TM, LANES), 1)

    def one_head(h, k_lo, latent):
        q = q_ref[0, h]
        hk = lax.shift_right_logical(h, kv_shift) if kv_shift else h
        if mode == "na" and latent:
            ws = pl.multiple_of(jnp.clip(NA_ROWS_PER_TILE * qi - NA_WIN_R // 2, 0,
                                         seq // GRID_W - NA_WIN_ROWS) * GRID_W, GRID_W)
            k = jnp.concatenate([k_ref[0, hk, pl.ds(ws, NA_WIN_ROWS * GRID_W), :], k_ref[0, hk, seq:t_all, :]], axis=0)
            v = jnp.concatenate([v_ref[0, hk, pl.ds(ws, NA_WIN_ROWS * GRID_W), :], v_ref[0, hk, seq:t_all, :]], axis=0)
            o = _softmax_pv(_dot_t(q, k) + bias_ref[0, h], v)
        else:
            k = k_ref[0, hk, k_lo:t_all, :]
            v = v_ref[0, hk, k_lo:t_all, :]
            if mode == "diff":
                qf = q.astype(F32)
                q0 = jnp.where(lane < DIFF_DH, qf, 0.0).astype(BF16)
                q1 = jnp.where(lane >= DIFF_DH, qf, 0.0).astype(BF16)
                o = _softmax_pv(_dot_t(q0, k), v) - lam * _softmax_pv(_dot_t(q1, k), v)
                ms = jnp.sum(o * o, axis=-1, keepdims=True) * (1.0 / HEAD_DIM)
                o = o * lax.rsqrt(ms + EPS) * sub_ref[...] * (1.0 - lambda_init)
            else:
                o = _softmax_pv(_dot_t(q, k), v)
        o_scr[h] = o

    def run(k_lo, latent):
        def body(h, carry):
            one_head(h, k_lo, latent)
            return carry
        lax.fori_loop(0, N_HEADS, body, 0)
        _pack_heads(o_scr, o_ref)

    @pl.when(qi < n_lat_tiles)
    def _():
        run(0, True)

    @pl.when(qi >= n_lat_tiles)
    def _():
        run(seq, False)


def _attention(mode, q, k, v, seq, n_q_tiles, lambda_init=0.0, extras=()):
    bsz, _, t_all, _ = q.shape
    hk = k.shape[1]
    kv_shift = 0 if hk == N_HEADS else 1
    n_lat_tiles = seq // TM
    in_specs = [
        pl.BlockSpec((1, N_HEADS, TM, LANES), lambda b, i: (b, 0, i, 0)),
        pl.BlockSpec((1, hk, t_all, LANES), lambda b, i: (b, 0, 0, 0)),
        pl.BlockSpec((1, hk, t_all, LANES), lambda b, i: (b, 0, 0, 0)),
    ]
    if mode == "na":
        (bias,) = extras
        n_cls = bias.shape[0]
        in_specs.append(pl.BlockSpec(
            (1,) + bias.shape[1:],
            lambda b, i: (jnp.where(i == 0, 0, jnp.where(i >= n_lat_tiles - 1, n_cls - 1, 1)), 0, 0, 0)))
    elif mode == "diff":
        lam_p, sub_g = extras
        in_specs += [pl.BlockSpec(lam_p.shape, lambda b, i: (0, 0)), pl.BlockSpec(sub_g.shape, lambda b, i: (0, 0))]
    return pl.pallas_call(
        functools.partial(_attn_kernel, mode, seq, kv_shift, lambda_init),
        grid=(bsz, n_q_tiles),
        in_specs=in_specs,
        out_specs=pl.BlockSpec((1, TM, N_HEADS * HEAD_DIM), lambda b, i: (b, i, 0)),
        out_shape=jax.ShapeDtypeStruct((bsz, t_all, N_HEADS * HEAD_DIM), BF16),
        scratch_shapes=[pltpu.VMEM((N_HEADS, TM, LANES), F32)],
        compiler_params=_cparams(("arbitrary", "arbitrary")),
        name="attn_" + mode,
    )(q, k, v, *extras)


def _merge_kernel(x_ref, mod_ref, g1_ref, ya_ref, yb_ref, yg_ref, ym_ref, wg_ref, wb_ref, wo_ref, o_ref):
    x = x_ref[0]
    mod = mod_ref[0]
    h = (_row_rmsnorm(x, g1_ref[...]) * (1.0 + mod[1:2]) + mod[0:1]).astype(BF16)
    acc = None
    for i, y_ref in enumerate((ya_ref, yb_ref, yg_ref, ym_ref)):
        term = _sigmoid(_dot(h, wg_ref[i])) * _dot(y_ref[0], wb_ref[i])
        acc = term if acc is None else acc + term
    o_ref[0] = x + mod[2:3] * _dot(acc.astype(BF16), wo_ref[...])


def _merge(x_all, mods, g1, ys, lw, n_tiles):
    bsz, t_all, _ = x_all.shape
    ctx_tile = t_all // TM - 1
    ctx_row = mods.shape[0] - HALO

    def full(a):
        nd = a.ndim
        return pl.BlockSpec(a.shape, lambda b, t: (0,) * nd)

    tok = lambda w: pl.BlockSpec((1, TM, w), lambda b, t: (b, t, 0))
    return pl.pallas_call(
        _merge_kernel,
        grid=(bsz, n_tiles),
        in_specs=[
            tok(D_MODEL),
            pl.BlockSpec((1, N_MOD, D_MODEL), lambda b, t: (jnp.where(t == ctx_tile, ctx_row, b), 0, 0)),
            full(g1), tok(256), tok(256), tok(256), tok(256),
            full(lw["w_gate"]), full(lw["w_branch"]), full(lw["w_out"]),
        ],
        out_specs=tok(D_MODEL),
        out_shape=jax.ShapeDtypeStruct(x_all.shape, F32),
        input_output_aliases={0: 0},
        compiler_params=_cparams(("arbitrary", "arbitrary")),
        name="merge",
    )(x_all, mods, g1, *ys, lw["w_gate"], lw["w_branch"], lw["w_out"])


def _ffn_kernel(seq, t_all, x_ref, xp_ref, xn_ref, mod_ref, g2_ref, wug_ref, wuv_ref, cg_ref, cv_ref, wd_ref,
                o_ref, h_scr, acc_scr):
    t = pl.program_id(1)
    x = x_ref[0]
    mod = mod_ref[0]
    g2 = g2_ref[...]

    def prenorm(rows):
        return (_row_rmsnorm(rows, g2) * (1.0 + mod[4:5]) + mod[3:4]).astype(BF16)

    h_scr[0:TM] = prenorm(x)
    h_scr[TM:TM + HALO] = prenorm(xp_ref[0])
    h_scr[TM + HALO:TM + 2 * HALO] = prenorm(xn_ref[0])
    acc_scr[...] = jnp.zeros_like(acc_scr)

    row = lax.broadcasted_iota(jnp.int32, (TM, FF_CHUNK), 0)
    g_first = t * TM
    has_prev = jnp.logical_and(g_first != 0, g_first != seq)
    has_next = jnp.logical_and(g_first + TM != seq, g_first + TM != t_all)
    first_row = row == 0
    last_row = row == TM - 1

    def conv(u_ext, c_ref, j):
        c = c_ref[j]
        u = u_ext[0:TM]
        prev_row = jnp.where(has_prev, u_ext[TM + HALO - 1:TM + HALO], 0.0)
        next_row = jnp.where(has_next, u_ext[TM + HALO:TM + HALO + 1], 0.0)
        up = jnp.where(first_row, prev_row, pltpu.roll(u, 1, 0))
        dn = jnp.where(last_row, next_row, pltpu.roll(u, TM - 1, 0))
        return c[3:4] + up * c[0:1] + u * c[1:2] + dn * c[2:3]

    def body(j, carry):
        hx = h_scr[...]
        gate = conv(_dot(hx, wug_ref[j]), cg_ref, j)
        val = conv(_dot(hx, wuv_ref[j]), cv_ref, j)
        act = (gate * _sigmoid(gate) * val).astype(BF16)
        acc_scr[...] += _dot(act, wd_ref[j])
        return carry

    lax.fori_loop(0, D_FF // FF_CHUNK, body, 0)
    o_ref[0] = x + mod[5:6] * acc_scr[...]


def _ffn(x_all, mods, g2, lw, seq, n_tiles, out_rows):
    bsz, t_all, _ = x_all.shape
    nt_all = t_all // TM
    ctx_tile = nt_all - 1
    ctx_row = mods.shape[0] - HALO
    blocks_per_tile = TM // HALO
    last_block = t_all // HALO - 1

    def full(a):
        nd = a.ndim
        return pl.BlockSpec(a.shape, lambda b, t: (0,) * nd)

    return pl.pallas_call(
        functools.partial(_ffn_kernel, seq, t_all),
        grid=(bsz, n_tiles),
        in_specs=[
            pl.BlockSpec((1, TM, D_MODEL), lambda b, t: (b, t, 0)),
            pl.BlockSpec((1, HALO, D_MODEL), lambda b, t: (b, jnp.maximum(t * blocks_per_tile - 1, 0), 0)),
            pl.BlockSpec((1, HALO, D_MODEL), lambda b, t: (b, jnp.minimum((t + 1) * blocks_per_tile, last_block), 0)),
            pl.BlockSpec((1, N_MOD, D_MODEL), lambda b, t: (jnp.where(t == ctx_tile, ctx_row, b), 0, 0)),
            full(g2), full(lw["w_up_g"]), full(lw["w_up_v"]), full(lw["conv_g"]), full(lw["conv_v"]), full(lw["w_down"]),
        ],
        out_specs=pl.BlockSpec((1, TM, D_MODEL), lambda b, t: (b, t, 0)),
        out_shape=jax.ShapeDtypeStruct((bsz, out_rows, D_MODEL), F32),
        scratch_shapes=[pltpu.VMEM((TM + 2 * HALO, D_MODEL), BF16), pltpu.VMEM((TM, D_MODEL), F32)],
        compiler_params=_cparams(("arbitrary", "arbitrary")),
        name="ffn",
    )(x_all, x_all, x_all, mods, g2, lw["w_up_g"], lw["w_up_v"], lw["conv_g"], lw["conv_v"], lw["w_down"])


def _pad_heads(w, n_heads, d):
    lead = w.shape[:-1]
    w = w.reshape(lead + (n_heads, d))
    w = jnp.pad(w, [(0, 0)] * len(lead) + [(0, 0), (0, LANES - d)])
    return w.reshape(lead + (n_heads * LANES,))


def _pad_lanes(v, width):
    return jnp.pad(v, (0, width - v.shape[0]))


def _group_matrix(groups):
    m = jnp.zeros((LANES, LANES), F32)
    for start, size in groups:
        m = m.at[start:start + size, start:start + size].set(1.0 / size)
    z = jnp.zeros_like(m)
    return jnp.block([[m, z], [z, m]])


def _rope_table(n_lat, n_ctx, rot_dim, lane0, lanes_used):
    t = jnp.arange(n_lat)
    row = (t // GRID_W).astype(F32)
    col = (t % GRID_W).astype(F32)
    n_axis = rot_dim // 4
    inv_freq = ROPE_THETA ** (-jnp.arange(n_axis, dtype=F32) / n_axis)
    ang = jnp.concatenate([row[:, None] * inv_freq, col[:, None] * inv_freq], axis=-1)
    cos_g = jnp.concatenate([jnp.cos(ang), jnp.cos(ang)], axis=-1)
    sin_g = jnp.concatenate([-jnp.sin(ang), jnp.sin(ang)], axis=-1)
    reps = lanes_used // rot_dim
    cos = jnp.ones((n_lat, LANES), F32).at[:, lane0:lane0 + lanes_used].set(jnp.tile(cos_g, (1, reps)))
    sin = jnp.zeros((n_lat, LANES), F32).at[:, lane0:lane0 + lanes_used].set(jnp.tile(sin_g, (1, reps)))
    cos = jnp.concatenate([cos, jnp.ones((n_ctx, LANES), F32)], axis=0)
    sin = jnp.concatenate([sin, jnp.zeros((n_ctx, LANES), F32)], axis=0)
    return cos, sin


def _rope_tables(n_lat, n_ctx):
    cd, sd = _rope_table(n_lat, n_ctx, DIFF_DH, 0, 2 * DIFF_DH)
    cg, sg = _rope_table(n_lat, n_ctx, HEAD_DIM, 0, HEAD_DIM)
    cm, sm = _rope_table(n_lat, n_ctx, MLA_ROPE, MLA_NOPE, MLA_ROPE)
    return jnp.concatenate([cd, sd, cg, sg, cm, sm], axis=1)


def _na_bias_tables(rel_bias, rows, n_ctx):
    cq = jnp.arange(GRID_W)
    col_start = jnp.clip(cq - NA_WIN_C // 2, 0, GRID_W - NA_WIN_C)
    ck = jnp.arange(GRID_W)
    col_in = (ck[None, :] >= col_start[:, None]) & (ck[None, :] < col_start[:, None] + NA_WIN_C)
    col_idx = jnp.clip(ck[None, :] - cq[:, None], -(NA_WIN_C - 1), NA_WIN_C - 1) + NA_WIN_C - 1
    tables = []
    for r0 in (0, NA_ROWS_PER_TILE, rows - NA_ROWS_PER_TILE):
        ws = min(max(r0 - NA_WIN_R // 2, 0), rows - NA_WIN_ROWS)
        r = r0 + jnp.arange(NA_ROWS_PER_TILE)
        kr = ws + jnp.arange(NA_WIN_ROWS)
        rs = jnp.clip(r - NA_WIN_R // 2, 0, rows - NA_WIN_R)
        in_band = (kr[None, :] >= rs[:, None]) & (kr[None, :] < rs[:, None] + NA_WIN_R)
        row_idx = jnp.clip(kr[None, :] - r[:, None] + NA_WIN_R - 1, 0, 2 * NA_WIN_R - 2)
        b = rel_bias[:, row_idx[:, None, :, None], col_idx[None, :, None, :]]
        ok = in_band[:, None, :, None] & col_in[None, :, None, :]
        b = jnp.where(ok[None], b, NEG_INF)
        b = b.reshape(N_HEADS, TM, NA_WIN_ROWS * GRID_W)
        tables.append(jnp.concatenate([b, jnp.zeros((N_HEADS, TM, n_ctx), F32)], axis=-1))
    return jnp.stack(tables)


def _layer_weights(i, p):
    w_in = p["w_in"][i]
    offs = [0]
    for n in IN_SIZES:
        offs.append(offs[-1] + n)
    seg = [w_in[:, offs[j]:offs[j + 1]] for j in range(len(IN_SIZES))]
    kr_cols = jnp.pad(seg[11], ((0, 0), (MLA_NOPE, LANES - MLA_NOPE - MLA_ROPE)))
    w_all = jnp.concatenate(
        [_pad_heads(seg[0], 4, 64), _pad_heads(seg[1], 4, 64), _pad_heads(seg[2], 4, 64),
         _pad_heads(seg[3], 4, 64), _pad_heads(seg[4], 4, 64), _pad_heads(seg[5], 4, 64),
         _pad_heads(seg[6], 4, 64), _pad_heads(seg[7], 2, 64), _pad_heads(seg[8], 2, 64),
         jnp.pad(seg[9], ((0, 0), (0, 256 - MLA_Q_RANK))), seg[10], kr_cols], axis=1).astype(BF16)

    w_qb = jnp.pad(_pad_heads(p["w_mla_qb"][i], N_HEADS, MLA_QK), ((0, 256 - MLA_Q_RANK), (0, 0))).astype(BF16)
    kvb = p["w_mla_kvb"][i].reshape(MLA_KV_RANK, N_HEADS, MLA_NOPE + MLA_V)
    w_kvb = jnp.concatenate([_pad_heads(kvb[:, :, :MLA_NOPE].reshape(MLA_KV_RANK, -1), N_HEADS, MLA_NOPE),
                             _pad_heads(kvb[:, :, MLA_NOPE:].reshape(MLA_KV_RANK, -1), N_HEADS, MLA_V)],
                            axis=1).astype(BF16)

    def head_gain(g, n_heads, scale=1.0):
        return _pad_lanes(jnp.tile(_pad_lanes(g * scale, LANES), n_heads), 512)

    dq = jnp.tile(p["diff_qk_g"][i], (1, 2))
    mq = p["mla_qk_g"][i]
    gvec = jnp.stack([
        head_gain(p["na_qk_g"][i, 0], 4, HEAD_DIM ** -0.5), head_gain(p["na_qk_g"][i, 1], 4),
        head_gain(dq[0], 4, DIFF_DH ** -0.5), head_gain(dq[1], 4),
        head_gain(p["gqa_qk_g"][i, 0], 4, HEAD_DIM ** -0.5), head_gain(p["gqa_qk_g"][i, 1], 2),
        head_gain(mq[0], 4, MLA_QK ** -0.5), head_gain(_pad_lanes(mq[1, :MLA_NOPE], MLA_QK), 4),
        _pad_lanes(p["mla_qa_g"][i], 512), _pad_lanes(p["mla_kva_g"][i], 512),
        _pad_lanes(jnp.pad(mq[1, MLA_NOPE:], (MLA_NOPE, 0)), 512),
    ] + [jnp.zeros((512,), F32)] * 5)
    gmat = jnp.stack([_group_matrix([(0, 64)]), _group_matrix([(0, 32), (32, 32)]),
                      _group_matrix([(0, 64), (64, 32)])]).astype(BF16)

    n_chunks = D_FF // FF_CHUNK
    w_up = p["w_up"][i]

    def chunk_cols(w):
        return w.reshape(w.shape[0], n_chunks, FF_CHUNK).transpose(1, 0, 2)

    conv = jnp.concatenate([p["conv_w"][i], p["conv_b"][i][None], jnp.zeros((4, 2 * D_FF), F32)], axis=0)
    return dict(
        w_all=w_all, w_qb=w_qb, w_kvb=w_kvb, gvec=gvec, gmat=gmat,
        w_gate=p["w_gate"][i].astype(BF16), w_branch=p["w_branch"][i].astype(BF16), w_out=p["w_out"][i].astype(BF16),
        w_up_g=chunk_cols(w_up[:, :D_FF]).astype(BF16), w_up_v=chunk_cols(w_up[:, D_FF:]).astype(BF16),
        conv_g=chunk_cols(conv[:, :D_FF]), conv_v=chunk_cols(conv[:, D_FF:]),
        w_down=p["w_down"][i].reshape(n_chunks, FF_CHUNK, D_MODEL).astype(BF16),
        sub_g=_pad_lanes(p["diff_subln_g"][i], LANES)[None],
    )


def kernel(x, c, ctx, c_ctx, w_mod, b_mod, norm1_g, norm2_g, w_in, na_qk_g, na_rel_bias, diff_qk_g, diff_lambda, diff_subln_g, gqa_qk_g, mla_qa_g, mla_kva_g, w_mla_qb, w_mla_kvb, mla_qk_g, w_gate, w_branch, w_out, w_up, conv_w, conv_b, w_down):
    p = dict(w_in=w_in, na_qk_g=na_qk_g, diff_qk_g=diff_qk_g, gqa_qk_g=gqa_qk_g, mla_qa_g=mla_qa_g,
             mla_kva_g=mla_kva_g, w_mla_qb=w_mla_qb, w_mla_kvb=w_mla_kvb, mla_qk_g=mla_qk_g, w_gate=w_gate,
             w_branch=w_branch, w_out=w_out, w_up=w_up, conv_w=conv_w, conv_b=conv_b, w_down=w_down,
             diff_subln_g=diff_subln_g)
    bsz, seq, d = x.shape
    n_ctx = ctx.shape[1]
    depth = w_mod.shape[0]
    assert d == D_MODEL and seq % TM == 0 and n_ctx == TM and seq // GRID_W >= NA_WIN_ROWS
    t_all = seq + n_ctx
    n_lat_tiles = seq // TM

    x_all = jnp.concatenate([x, ctx], axis=1)
    cc = jnp.concatenate([c, c_ctx[None], jnp.zeros((HALO - 1, d), F32)], axis=0)
    mods_all = _mod_all(cc, w_mod.astype(BF16), b_mod[:, None, :])
    mods_all = mods_all.reshape(depth, bsz + HALO, N_MOD, d)
    tables = _rope_tables(seq, n_ctx)

    for i in range(depth):
        last = i == depth - 1
        n_tiles = n_lat_tiles if last else n_lat_tiles + 1
        lambda_init = 0.8 - 0.6 * math.exp(-0.3 * i)
        lw = _layer_weights(i, p)
        mods = mods_all[i]
        g1 = norm1_g[i][None]
        g2 = norm2_g[i][None]
        (qna, kna, vna, qdf, kdf, vdf, qgq, kgq, vgq, qml, kml, vml) = _inproj(x_all, mods, g1, lw, tables)
        bias = _na_bias_tables(na_rel_bias[i], seq // GRID_W, n_ctx)
        ya = _attention("na", qna, kna, vna, seq, n_tiles, extras=(bias,))
        yb = _attention("diff", qdf, kdf, vdf, seq, n_tiles, lambda_init, extras=(diff_lambda[i], lw["sub_g"]))
        yg = _attention("sm", qgq, kgq, vgq, seq, n_tiles)
        ym = _attention("sm", qml, kml, vml, seq, n_tiles)
        x_all = _merge(x_all, mods, g1, (ya, yb, yg, ym), lw, n_tiles)
        x_all = _ffn(x_all, mods, g2, lw, seq, n_tiles, seq if last else t_all)
    return x_all
```

```python
import functools
import math

import jax
import jax.numpy as jnp
from jax import lax
from jax.experimental import pallas as pl
from jax.experimental.pallas import tpu as pltpu

D_MODEL = 1024
GRID_W = 64
HEAD_DIM = 64
EPS = 1e-6
ROPE_THETA = 10000.0
NEG_INF = -1e30
NA_WIN_R = 8
NA_WIN_C = 16
DIFF_DH = 32
MLA_Q_RANK = 192
MLA_KV_RANK = 128
MLA_NOPE = 64
MLA_ROPE = 32
MLA_V = 64
MLA_QK = MLA_NOPE + MLA_ROPE
N_HEADS = 4
GQA_KV_HEADS = 2
N_MOD = 6
D_FF = 2816
IN_SIZES = (256, 256, 256, 256, 256, 256, 256, 128, 128, MLA_Q_RANK, MLA_KV_RANK, MLA_ROPE)

LANES = 128
MXU_DIM = 256
TM = 256
FF_CHUNK = 256
HALO = 8
NA_ROWS_PER_TILE = TM // GRID_W
NA_WIN_ROWS = NA_WIN_R + NA_ROWS_PER_TILE
VMEM_LIMIT = 56 * 1024 * 1024

BF16 = jnp.bfloat16
F32 = jnp.float32


def _cparams(sem):
    return pltpu.CompilerParams(dimension_semantics=sem, vmem_limit_bytes=VMEM_LIMIT)


def _dot(a, b):
    return jnp.dot(a, b, preferred_element_type=F32)


def _dot_t(a, b):
    return lax.dot_general(a, b, (((1,), (1,)), ((), ())), preferred_element_type=F32)


def _sigmoid(z):
    return 1.0 / (1.0 + jnp.exp(-z))


def _row_rmsnorm(x, g):
    ms = jnp.mean(x * x, axis=-1, keepdims=True)
    return x * lax.rsqrt(ms + EPS) * g


def _mod_kernel(c_ref, w_ref, b_ref, o_ref):
    c = c_ref[...]
    act = (c * _sigmoid(c)).astype(BF16)
    o_ref[0] = _dot(act, w_ref[0]) + b_ref[0]


def _mod_all(cc, w_mod, b_mod):
    n_layers, _, n_out = w_mod.shape
    rows = cc.shape[0]
    tn = 1024
    return pl.pallas_call(
        _mod_kernel,
        grid=(n_layers, n_out // tn),
        in_specs=[
            pl.BlockSpec((rows, D_MODEL), lambda l, j: (0, 0)),
            pl.BlockSpec((1, D_MODEL, tn), lambda l, j: (l, 0, j)),
            pl.BlockSpec((1, 1, tn), lambda l, j: (l, 0, j)),
        ],
        out_specs=pl.BlockSpec((1, rows, tn), lambda l, j: (l, 0, j)),
        out_shape=jax.ShapeDtypeStruct((n_layers, rows, n_out), F32),
        compiler_params=_cparams(("arbitrary", "arbitrary")),
        name="mod_all",
    )(cc, w_mod, b_mod)


def _group_ms(y, g_mat):
    outs = []
    for c in range(y.shape[1] // MXU_DIM):
        yc = y[:, c * MXU_DIM:(c + 1) * MXU_DIM]
        sq = yc * yc
        hi = sq.astype(BF16)
        lo = (sq - hi.astype(F32)).astype(BF16)
        outs.append(_dot(hi, g_mat) + _dot(lo, g_mat))
    return outs[0] if len(outs) == 1 else jnp.concatenate(outs, axis=1)


def _group_ms128(y, g_mat):
    sq = y * y
    hi = sq.astype(BF16)
    lo = (sq - hi.astype(F32)).astype(BF16)
    g = g_mat[:LANES, :LANES]
    return _dot(hi, g) + _dot(lo, g)


def _rope(y, cos, sin_signed, half):
    outs = []
    lane = lax.broadcasted_iota(jnp.int32, (y.shape[0], LANES), 1)
    first = jnp.bitwise_and(lane, 2 * half - 1) < half
    for c in range(y.shape[1] // LANES):
        yc = y[:, c * LANES:(c + 1) * LANES]
        fwd = pltpu.roll(yc, LANES - half, 1)
        bwd = pltpu.roll(yc, half, 1)
        partner = jnp.where(first, fwd, bwd)
        outs.append(yc * cos + partner * sin_signed)
    return outs[0] if len(outs) == 1 else jnp.concatenate(outs, axis=1)


def _store_heads(ref, y):
    for h in range(y.shape[1] // LANES):
        ref[0, h] = y[:, h * LANES:(h + 1) * LANES].astype(ref.dtype)


def _inproj_kernel(x_ref, mod_ref, g1_ref, w_ref, wqb_ref, wkvb_ref, gv_ref, gm_ref, tab_ref,
                   qna_ref, kna_ref, vna_ref, qdf_ref, kdf_ref, vdf_ref,
                   qgq_ref, kgq_ref, vgq_ref, qml_ref, kml_ref, vml_ref):
    x = x_ref[0]
    mod = mod_ref[0]
    h = (_row_rmsnorm(x, g1_ref[...]) * (1.0 + mod[1:2]) + mod[0:1]).astype(BF16)

    g_a, g_d, g_m = gm_ref[0], gm_ref[1], gm_ref[2]
    tab = tab_ref[...]
    cos_d, sin_d = tab[:, 0:128], tab[:, 128:256]
    cos_g, sin_g = tab[:, 256:384], tab[:, 384:512]
    cos_m, sin_m = tab[:, 512:640], tab[:, 640:768]

    def proj(lo, width):
        return _dot(h, w_ref[:, lo:lo + width])

    def normed(y, g_mat, row):
        return y * lax.rsqrt(_group_ms(y, g_mat) + EPS) * gv_ref[row:row + 1, :y.shape[1]]

    _store_heads(qna_ref, normed(proj(0, 512), g_a, 0))
    _store_heads(kna_ref, normed(proj(512, 512), g_a, 1))
    _store_heads(vna_ref, proj(1024, 512))
    _store_heads(qdf_ref, _rope(normed(proj(1536, 512), g_d, 2), cos_d, sin_d, DIFF_DH // 2))
    _store_heads(kdf_ref, _rope(normed(proj(2048, 512), g_d, 3), cos_d, sin_d, DIFF_DH // 2))
    _store_heads(vdf_ref, proj(2560, 512))
    _store_heads(qgq_ref, _rope(normed(proj(3072, 512), g_a, 4), cos_g, sin_g, HEAD_DIM // 2))
    _store_heads(kgq_ref, _rope(normed(proj(3584, 256), g_a, 5), cos_g, sin_g, HEAD_DIM // 2))
    _store_heads(vgq_ref, proj(3840, 256))
    qa = proj(4096, 256)
    ms_qa = jnp.sum(qa * qa, axis=-1, keepdims=True) * (1.0 / MLA_Q_RANK)
    qa_n = (qa * lax.rsqrt(ms_qa + EPS) * gv_ref[8:9, :256]).astype(BF16)
    q_ml = _dot(qa_n, wqb_ref[...])
    _store_heads(qml_ref, _rope(normed(q_ml, g_m, 6), cos_m, sin_m, MLA_ROPE // 2))
    kva = proj(4352, 128)
    kva_n = _row_rmsnorm(kva, gv_ref[9:10, :128]).astype(BF16)
    kv = _dot(kva_n, wkvb_ref[...])
    k_nope = normed(kv[:, :512], g_m, 7)
    kr = proj(4480, 128)
    kr_n = kr * lax.rsqrt(_group_ms128(kr, g_m) + EPS) * gv_ref[10:11, :128]
    kr_r = _rope(kr_n, cos_m, sin_m, MLA_ROPE // 2)
    _store_heads(kml_ref, k_nope + jnp.concatenate([kr_r] * N_HEADS, axis=1))
    _store_heads(vml_ref, kv[:, 512:])


def _inproj(x_all, mods, g1, lw, tables):
    bsz, t_all, _ = x_all.shape
    nt = t_all // TM
    ctx_tile = nt - 1
    ctx_row = mods.shape[0] - HALO

    def head_spec(nh):
        return pl.BlockSpec((1, nh, TM, LANES), lambda t, b: (b, 0, t, 0))

    def head_shape(nh):
        return jax.ShapeDtypeStruct((bsz, nh, t_all, LANES), BF16)

    def full(a):
        nd = a.ndim
        return pl.BlockSpec(a.shape, lambda t, b: (0,) * nd)

    heads = (4, 4, 4, 4, 4, 4, 4, GQA_KV_HEADS, GQA_KV_HEADS, 4, 4, 4)
    return pl.pallas_call(
        _inproj_kernel,
        grid=(nt, bsz),
        in_specs=[
            pl.BlockSpec((1, TM, D_MODEL), lambda t, b: (b, t, 0)),
            pl.BlockSpec((1, N_MOD, D_MODEL), lambda t, b: (jnp.where(t == ctx_tile, ctx_row, b), 0, 0)),
            full(g1), full(lw["w_all"]), full(lw["w_qb"]), full(lw["w_kvb"]), full(lw["gvec"]), full(lw["gmat"]),
            pl.BlockSpec((TM, tables.shape[1]), lambda t, b: (t, 0)),
        ],
        out_specs=[head_spec(n) for n in heads],
        out_shape=[head_shape(n) for n in heads],
        compiler_params=_cparams(("arbitrary", "arbitrary")),
        name="inproj",
    )(x_all, mods, g1, lw["w_all"], lw["w_qb"], lw["w_kvb"], lw["gvec"], lw["gmat"], tables)


def _softmax_pv(s, v):
    m = jnp.max(s, axis=-1, keepdims=True)
    p = jnp.exp(s - m)
    l = jnp.sum(p, axis=-1, keepdims=True)
    return _dot(p.astype(BF16), v) * (1.0 / l)


def _pack_heads(o_scr, o_ref):
    o_ref[0] = jnp.concatenate([o_scr[h][:, :HEAD_DIM] for h in range(N_HEADS)], axis=1).astype(o_ref.dtype)


def _attn_kernel(mode, seq, kv_shift, lambda_init, q_ref, k_ref, v_ref, *rest):
    if mode == "na":
        bias_ref, o_ref, o_scr = rest
    elif mode == "diff":
        lam_ref, sub_ref, o_ref, o_scr = rest
    else:
        o_ref, o_scr = rest
    qi = pl.program_id(1)
    t_all = k_ref.shape[2]
    n_lat_tiles = seq // TM

    if mode == "diff":
        lp = lam_ref[...]
        lam = (jnp.exp(jnp.sum(lp[0:1] * lp[1:2], axis=-1, keepdims=True))
               - jnp.exp(jnp.sum(lp[2:3] * lp[3:4], axis=-1, keepdims=True)) + lambda_init)
        lane = lax.broadcasted_iota(jnp.int32, (TM, LANES), 1)

    def one_head(h, k_lo, latent):
        q = q_ref[0, h]
        hk = h >> kv_shift
        if mode == "na" and latent:
            ws = pl.multiple_of(jnp.clip(NA_ROWS_PER_TILE * qi - NA_WIN_R // 2, 0,
                                         seq // GRID_W - NA_WIN_ROWS) * GRID_W, GRID_W)
            k = jnp.concatenate([k_ref[0, hk, pl.ds(ws, NA_WIN_ROWS * GRID_W), :], k_ref[0, hk, seq:t_all, :]], axis=0)
            v = jnp.concatenate([v_ref[0, hk, pl.ds(ws, NA_WIN_ROWS * GRID_W), :], v_ref[0, hk, seq:t_all, :]], axis=0)
            o = _softmax_pv(_dot_t(q, k) + bias_ref[0, h], v)
        else:
            k = k_ref[0, hk, k_lo:t_all, :]
            v = v_ref[0, hk, k_lo:t_all, :]
            if mode == "diff":
                qf = q.astype(F32)
                q0 = jnp.where(lane < DIFF_DH, qf, 0.0).astype(BF16)
                q1 = jnp.where(lane >= DIFF_DH, qf, 0.0).astype(BF16)
                o = _softmax_pv(_dot_t(q0, k), v) - lam * _softmax_pv(_dot_t(q1, k), v)
                ms = jnp.sum(o * o, axis=-1, keepdims=True) * (1.0 / HEAD_DIM)
                o = o * lax.rsqrt(ms + EPS) * sub_ref[...] * (1.0 - lambda_init)
            else:
                o = _softmax_pv(_dot_t(q, k), v)
        o_scr[h] = o

    def run(k_lo, latent):
        for h in range(N_HEADS):
            one_head(h, k_lo, latent)
        _pack_heads(o_scr, o_ref)

    @pl.when(qi < n_lat_tiles)
    def _():
        run(0, True)

    @pl.when(qi >= n_lat_tiles)
    def _():
        run(seq, False)


def _attention(mode, q, k, v, seq, n_q_tiles, lambda_init=0.0, extras=()):
    bsz, _, t_all, _ = q.shape
    hk = k.shape[1]
    kv_shift = 0 if hk == N_HEADS else 1
    n_lat_tiles = seq // TM
    in_specs = [
        pl.BlockSpec((1, N_HEADS, TM, LANES), lambda b, i: (b, 0, i, 0)),
        pl.BlockSpec((1, hk, t_all, LANES), lambda b, i: (b, 0, 0, 0)),
        pl.BlockSpec((1, hk, t_all, LANES), lambda b, i: (b, 0, 0, 0)),
    ]
    if mode == "na":
        (bias,) = extras
        n_cls = bias.shape[0]
        in_specs.append(pl.BlockSpec(
            (1,) + bias.shape[1:],
            lambda b, i: (jnp.where(i == 0, 0, jnp.where(i >= n_lat_tiles - 1, n_cls - 1, 1)), 0, 0, 0)))
    elif mode == "diff":
        lam_p, sub_g = extras
        in_specs += [pl.BlockSpec(lam_p.shape, lambda b, i: (0, 0)), pl.BlockSpec(sub_g.shape, lambda b, i: (0, 0))]
    return pl.pallas_call(
        functools.partial(_attn_kernel, mode, seq, kv_shift, lambda_init),
        grid=(bsz, n_q_tiles),
        in_specs=in_specs,
        out_specs=pl.BlockSpec((1, TM, N_HEADS * HEAD_DIM), lambda b, i: (b, i, 0)),
        out_shape=jax.ShapeDtypeStruct((bsz, t_all, N_HEADS * HEAD_DIM), BF16),
        scratch_shapes=[pltpu.VMEM((N_HEADS, TM, LANES), F32)],
        compiler_params=_cparams(("arbitrary", "arbitrary")),
        name="attn_" + mode,
    )(q, k, v, *extras)


def _merge_kernel(x_ref, mod_ref, g1_ref, ya_ref, yb_ref, yg_ref, ym_ref, wg_ref, wb_ref, wo_ref, o_ref):
    x = x_ref[0]
    mod = mod_ref[0]
    h = (_row_rmsnorm(x, g1_ref[...]) * (1.0 + mod[1:2]) + mod[0:1]).astype(BF16)
    acc = None
    for i, y_ref in enumerate((ya_ref, yb_ref, yg_ref, ym_ref)):
        term = _sigmoid(_dot(h, wg_ref[i])) * _dot(y_ref[0], wb_ref[i])
        acc = term if acc is None else acc + term
    o_ref[0] = x + mod[2:3] * _dot(acc.astype(BF16), wo_ref[...])


def _merge(x_all, mods, g1, ys, lw, n_tiles):
    bsz, t_all, _ = x_all.shape
    ctx_tile = t_all // TM - 1
    ctx_row = mods.shape[0] - HALO

    def full(a):
        nd = a.ndim
        return pl.BlockSpec(a.shape, lambda b, t: (0,) * nd)

    tok = lambda w: pl.BlockSpec((1, TM, w), lambda b, t: (b, t, 0))
    return pl.pallas_call(
        _merge_kernel,
        grid=(bsz, n_tiles),
        in_specs=[
            tok(D_MODEL),
            pl.BlockSpec((1, N_MOD, D_MODEL), lambda b, t: (jnp.where(t == ctx_tile, ctx_row, b), 0, 0)),
            full(g1), tok(256), tok(256), tok(256), tok(256),
            full(lw["w_gate"]), full(lw["w_branch"]), full(lw["w_out"]),
        ],
        out_specs=tok(D_MODEL),
        out_shape=jax.ShapeDtypeStruct(x_all.shape, F32),
        input_output_aliases={0: 0},
        compiler_params=_cparams(("arbitrary", "arbitrary")),
        name="merge",
    )(x_all, mods, g1, *ys, lw["w_gate"], lw["w_branch"], lw["w_out"])


def _ffn_kernel(seq, t_all, x_ref, xp_ref, xn_ref, mod_ref, g2_ref, wu_ref, cw_ref, wd_ref, o_ref, act_scr):
    t = pl.program_id(1)
    x = x_ref[0]
    mod = mod_ref[0]
    x_ext = jnp.concatenate([x, xp_ref[0], xn_ref[0]], axis=0)
    hx = (_row_rmsnorm(x_ext, g2_ref[...]) * (1.0 + mod[4:5]) + mod[3:4]).astype(BF16)

    row = lax.broadcasted_iota(jnp.int32, (TM, FF_CHUNK), 0)
    g_first = t * TM
    has_prev = jnp.logical_and(g_first != 0, g_first != seq)
    has_next = jnp.logical_and(g_first + TM != seq, g_first + TM != t_all)
    first_row = row == 0
    last_row = row == TM - 1

    def conv(lo):
        u_ext = _dot(hx, wu_ref[:, lo:lo + FF_CHUNK])
        c = cw_ref[:, lo:lo + FF_CHUNK]
        u = u_ext[0:TM]
        prev_row = jnp.where(has_prev, u_ext[TM + HALO - 1:TM + HALO], 0.0)
        next_row = jnp.where(has_next, u_ext[TM + HALO:TM + HALO + 1], 0.0)
        up = jnp.where(first_row, prev_row, pltpu.roll(u, 1, 0))
        dn = jnp.where(last_row, next_row, pltpu.roll(u, TM - 1, 0))
        return c[3:4] + up * c[0:1] + u * c[1:2] + dn * c[2:3]

    for j in range(D_FF // FF_CHUNK):
        lo = j * FF_CHUNK
        gate = conv(lo)
        val = conv(D_FF + lo)
        act_scr[:, lo:lo + FF_CHUNK] = (gate * _sigmoid(gate) * val).astype(BF16)
    o_ref[0] = x + mod[5:6] * _dot(act_scr[...], wd_ref[...])


def _ffn(x_all, mods, g2, lw, seq, n_tiles, out_rows):
    bsz, t_all, _ = x_all.shape
    nt_all = t_all // TM
    ctx_tile = nt_all - 1
    ctx_row = mods.shape[0] - HALO
    blocks_per_tile = TM // HALO
    last_block = t_all // HALO - 1

    def full(a):
        nd = a.ndim
        return pl.BlockSpec(a.shape, lambda b, t: (0,) * nd)

    return pl.pallas_call(
        functools.partial(_ffn_kernel, seq, t_all),
        grid=(bsz, n_tiles),
        in_specs=[
            pl.BlockSpec((1, TM, D_MODEL), lambda b, t: (b, t, 0)),
            pl.BlockSpec((1, HALO, D_MODEL), lambda b, t: (b, jnp.maximum(t * blocks_per_tile - 1, 0), 0)),
            pl.BlockSpec((1, HALO, D_MODEL), lambda b, t: (b, jnp.minimum((t + 1) * blocks_per_tile, last_block), 0)),
            pl.BlockSpec((1, N_MOD, D_MODEL), lambda b, t: (jnp.where(t == ctx_tile, ctx_row, b), 0, 0)),
            full(g2), full(lw["w_up"]), full(lw["conv"]), full(lw["w_down"]),
        ],
        out_specs=pl.BlockSpec((1, TM, D_MODEL), lambda b, t: (b, t, 0)),
        out_shape=jax.ShapeDtypeStruct((bsz, out_rows, D_MODEL), F32),
        scratch_shapes=[pltpu.VMEM((TM, D_FF), BF16)],
        compiler_params=_cparams(("arbitrary", "arbitrary")),
        name="ffn",
    )(x_all, x_all, x_all, mods, g2, lw["w_up"], lw["conv"], lw["w_down"])


def _pad_heads(w, n_heads, d):
    lead = w.shape[:-1]
    w = w.reshape(lead + (n_heads, d))
    w = jnp.pad(w, [(0, 0)] * len(lead) + [(0, 0), (0, LANES - d)])
    return w.reshape(lead + (n_heads * LANES,))


def _pad_lanes(v, width):
    return jnp.pad(v, (0, width - v.shape[0]))


def _group_matrix(groups):
    m = jnp.zeros((LANES, LANES), F32)
    for start, size in groups:
        m = m.at[start:start + size, start:start + size].set(1.0 / size)
    z = jnp.zeros_like(m)
    return jnp.block([[m, z], [z, m]])


def _rope_table(n_lat, n_ctx, rot_dim, lane0, lanes_used):
    t = jnp.arange(n_lat)
    row = (t // GRID_W).astype(F32)
    col = (t % GRID_W).astype(F32)
    n_axis = rot_dim // 4
    inv_freq = ROPE_THETA ** (-jnp.arange(n_axis, dtype=F32) / n_axis)
    ang = jnp.concatenate([row[:, None] * inv_freq, col[:, None] * inv_freq], axis=-1)
    cos_g = jnp.concatenate([jnp.cos(ang), jnp.cos(ang)], axis=-1)
    sin_g = jnp.concatenate([-jnp.sin(ang), jnp.sin(ang)], axis=-1)
    reps = lanes_used // rot_dim
    cos = jnp.ones((n_lat, LANES), F32).at[:, lane0:lane0 + lanes_used].set(jnp.tile(cos_g, (1, reps)))
    sin = jnp.zeros((n_lat, LANES), F32).at[:, lane0:lane0 + lanes_used].set(jnp.tile(sin_g, (1, reps)))
    cos = jnp.concatenate([cos, jnp.ones((n_ctx, LANES), F32)], axis=0)
    sin = jnp.concatenate([sin, jnp.zeros((n_ctx, LANES), F32)], axis=0)
    return cos, sin


def _rope_tables(n_lat, n_ctx):
    cd, sd = _rope_table(n_lat, n_ctx, DIFF_DH, 0, 2 * DIFF_DH)
    cg, sg = _rope_table(n_lat, n_ctx, HEAD_DIM, 0, HEAD_DIM)
    cm, sm = _rope_table(n_lat, n_ctx, MLA_ROPE, MLA_NOPE, MLA_ROPE)
    return jnp.concatenate([cd, sd, cg, sg, cm, sm], axis=1)


def _na_bias_tables(rel_bias, rows, n_ctx):
    cq = jnp.arange(GRID_W)
    col_start = jnp.clip(cq - NA_WIN_C // 2, 0, GRID_W - NA_WIN_C)
    ck = jnp.arange(GRID_W)
    col_in = (ck[None, :] >= col_start[:, None]) & (ck[None, :] < col_start[:, None] + NA_WIN_C)
    col_idx = jnp.clip(ck[None, :] - cq[:, None], -(NA_WIN_C - 1), NA_WIN_C - 1) + NA_WIN_C - 1
    col_sel = jax.nn.one_hot(col_idx.reshape(-1), 2 * NA_WIN_C - 1, dtype=F32)
    tables = []
    for r0 in (0, NA_ROWS_PER_TILE, rows - NA_ROWS_PER_TILE):
        ws = min(max(r0 - NA_WIN_R // 2, 0), rows - NA_WIN_ROWS)
        r = r0 + jnp.arange(NA_ROWS_PER_TILE)
        kr = ws + jnp.arange(NA_WIN_ROWS)
        rs = jnp.clip(r - NA_WIN_R // 2, 0, rows - NA_WIN_R)
        in_band = (kr[None, :] >= rs[:, None]) & (kr[None, :] < rs[:, None] + NA_WIN_R)
        row_idx = jnp.clip(kr[None, :] - r[:, None] + NA_WIN_R - 1, 0, 2 * NA_WIN_R - 2)
        row_sel = jax.nn.one_hot(row_idx.reshape(-1), 2 * NA_WIN_R - 1, dtype=F32)
        b = jnp.einsum("pa,lhac,qc->lhpq", row_sel, rel_bias, col_sel, precision=lax.Precision.HIGHEST)
        b = b.reshape(rel_bias.shape[:2] + (NA_ROWS_PER_TILE, NA_WIN_ROWS, GRID_W, GRID_W)).transpose(0, 1, 2, 4, 3, 5)
        ok = in_band[:, None, :, None] & col_in[None, :, None, :]
        b = jnp.where(ok[None, None], b, NEG_INF)
        b = b.reshape(rel_bias.shape[:2] + (TM, NA_WIN_ROWS * GRID_W))
        tables.append(jnp.concatenate([b, jnp.zeros(rel_bias.shape[:2] + (TM, n_ctx), F32)], axis=-1))
    return jnp.stack(tables, axis=1)


def _layer_weights(i, p):
    w_in = p["w_in"][i]
    offs = [0]
    for n in IN_SIZES:
        offs.append(offs[-1] + n)
    seg = [w_in[:, offs[j]:offs[j + 1]] for j in range(len(IN_SIZES))]
    kr_cols = jnp.pad(seg[11], ((0, 0), (MLA_NOPE, LANES - MLA_NOPE - MLA_ROPE)))
    w_all = jnp.concatenate(
        [_pad_heads(seg[0], 4, 64), _pad_heads(seg[1], 4, 64), _pad_heads(seg[2], 4, 64),
         _pad_heads(seg[3], 4, 64), _pad_heads(seg[4], 4, 64), _pad_heads(seg[5], 4, 64),
         _pad_heads(seg[6], 4, 64), _pad_heads(seg[7], 2, 64), _pad_heads(seg[8], 2, 64),
         jnp.pad(seg[9], ((0, 0), (0, 256 - MLA_Q_RANK))), seg[10], kr_cols], axis=1).astype(BF16)

    w_qb = jnp.pad(_pad_heads(p["w_mla_qb"][i], N_HEADS, MLA_QK), ((0, 256 - MLA_Q_RANK), (0, 0))).astype(BF16)
    kvb = p["w_mla_kvb"][i].reshape(MLA_KV_RANK, N_HEADS, MLA_NOPE + MLA_V)
    w_kvb = jnp.concatenate([_pad_heads(kvb[:, :, :MLA_NOPE].reshape(MLA_KV_RANK, -1), N_HEADS, MLA_NOPE),
                             _pad_heads(kvb[:, :, MLA_NOPE:].reshape(MLA_KV_RANK, -1), N_HEADS, MLA_V)],
                            axis=1).astype(BF16)

    def head_gain(g, n_heads, scale=1.0):
        return _pad_lanes(jnp.tile(_pad_lanes(g * scale, LANES), n_heads), 512)

    dq = jnp.tile(p["diff_qk_g"][i], (1, 2))
    mq = p["mla_qk_g"][i]
    gvec = jnp.stack([
        head_gain(p["na_qk_g"][i, 0], 4, HEAD_DIM ** -0.5), head_gain(p["na_qk_g"][i, 1], 4),
        head_gain(dq[0], 4, DIFF_DH ** -0.5), head_gain(dq[1], 4),
        head_gain(p["gqa_qk_g"][i, 0], 4, HEAD_DIM ** -0.5), head_gain(p["gqa_qk_g"][i, 1], 2),
        head_gain(mq[0], 4, MLA_QK ** -0.5), head_gain(_pad_lanes(mq[1, :MLA_NOPE], MLA_QK), 4),
        _pad_lanes(p["mla_qa_g"][i], 512), _pad_lanes(p["mla_kva_g"][i], 512),
        _pad_lanes(jnp.pad(mq[1, MLA_NOPE:], (MLA_NOPE, 0)), 512),
    ] + [jnp.zeros((512,), F32)] * 5)
    gmat = jnp.stack([_group_matrix([(0, 64)]), _group_matrix([(0, 32), (32, 32)]),
                      _group_matrix([(0, 64), (64, 32)])]).astype(BF16)

    conv = jnp.concatenate([p["conv_w"][i], p["conv_b"][i][None], jnp.zeros((4, 2 * D_FF), F32)], axis=0)
    return dict(
        w_all=w_all, w_qb=w_qb, w_kvb=w_kvb, gvec=gvec, gmat=gmat,
        w_gate=p["w_gate"][i].astype(BF16), w_branch=p["w_branch"][i].astype(BF16), w_out=p["w_out"][i].astype(BF16),
        w_up=p["w_up"][i].astype(BF16), conv=conv, w_down=p["w_down"][i].astype(BF16),
        sub_g=_pad_lanes(p["diff_subln_g"][i], LANES)[None],
    )


def kernel(x, c, ctx, c_ctx, w_mod, b_mod, norm1_g, norm2_g, w_in, na_qk_g, na_rel_bias, diff_qk_g, diff_lambda, diff_subln_g, gqa_qk_g, mla_qa_g, mla_kva_g, w_mla_qb, w_mla_kvb, mla_qk_g, w_gate, w_branch, w_out, w_up, conv_w, conv_b, w_down):
    p = dict(w_in=w_in, na_qk_g=na_qk_g, diff_qk_g=diff_qk_g, gqa_qk_g=gqa_qk_g, mla_qa_g=mla_qa_g,
             mla_kva_g=mla_kva_g, w_mla_qb=w_mla_qb, w_mla_kvb=w_mla_kvb, mla_qk_g=mla_qk_g, w_gate=w_gate,
             w_branch=w_branch, w_out=w_out, w_up=w_up, conv_w=conv_w, conv_b=conv_b, w_down=w_down,
             diff_subln_g=diff_subln_g)
    bsz, seq, d = x.shape
    n_ctx = ctx.shape[1]
    depth = w_mod.shape[0]
    assert d == D_MODEL and seq % TM == 0 and n_ctx == TM and seq // GRID_W >= NA_WIN_ROWS
    t_all = seq + n_ctx
    n_lat_tiles = seq // TM

    x_all = jnp.concatenate([x, ctx], axis=1)
    cc = jnp.concatenate([c, c_ctx[None], jnp.zeros((HALO - 1, d), F32)], axis=0)
    mods_all = _mod_all(cc, w_mod.astype(BF16), b_mod[:, None, :])
    mods_all = mods_all.reshape(depth, bsz + HALO, N_MOD, d)
    tables = _rope_tables(seq, n_ctx)
    na_bias = _na_bias_tables(na_rel_bias, seq // GRID_W, n_ctx)

    for i in range(depth):
        last = i == depth - 1
        n_tiles = n_lat_tiles if last else n_lat_tiles + 1
        lambda_init = 0.8 - 0.6 * math.exp(-0.3 * i)
        lw = _layer_weights(i, p)
        mods = mods_all[i]
        g1 = norm1_g[i][None]
        g2 = norm2_g[i][None]
        (qna, kna, vna, qdf, kdf, vdf, qgq, kgq, vgq, qml, kml, vml) = _inproj(x_all, mods, g1, lw, tables)
        ya = _attention("na", qna, kna, vna, seq, n_tiles, extras=(na_bias[i],))
        yb = _attention("diff", qdf, kdf, vdf, seq, n_tiles, lambda_init, extras=(diff_lambda[i], lw["sub_g"]))
        yg = _attention("sm", qgq, kgq, vgq, seq, n_tiles)
        ym = _attention("sm", qml, kml, vml, seq, n_tiles)
        x_all = _merge(x_all, mods, g1, (ya, yb, yg, ym), lw, n_tiles)
        x_all = _ffn(x_all, mods, g2, lw, seq, n_tiles, seq if last else t_all)
    return x_all
```

```python
import functools
import math

import jax
import jax.numpy as jnp
from jax import lax
from jax.experimental import pallas as pl
from jax.experimental.pallas import tpu as pltpu

D_MODEL = 1024
GRID_W = 64
HEAD_DIM = 64
EPS = 1e-6
ROPE_THETA = 10000.0
NEG_INF = -1e30
NA_WIN_R = 8
NA_WIN_C = 16
DIFF_DH = 32
MLA_Q_RANK = 192
MLA_KV_RANK = 128
MLA_NOPE = 64
MLA_ROPE = 32
MLA_V = 64
MLA_QK = MLA_NOPE + MLA_ROPE
N_HEADS = 4
GQA_KV_HEADS = 2
N_MOD = 6
D_FF = 2816
IN_SIZES = (256, 256, 256, 256, 256, 256, 256, 128, 128, MLA_Q_RANK, MLA_KV_RANK, MLA_ROPE)

LANES = 128
MXU_DIM = 256
TM = 256
FF_CHUNK = 256
HALO = 8
LOG2E = math.log2(math.e)
NA_ROWS_PER_TILE = TM // GRID_W
NA_WIN_ROWS = NA_WIN_R + NA_ROWS_PER_TILE
VMEM_LIMIT = 56 * 1024 * 1024

BF16 = jnp.bfloat16
F32 = jnp.float32


def _cparams(sem):
    return pltpu.CompilerParams(dimension_semantics=sem, vmem_limit_bytes=VMEM_LIMIT)


def _dot(a, b):
    return jnp.dot(a, b, preferred_element_type=F32)


def _sigmoid(z):
    return 1.0 / (1.0 + jnp.exp(-z))


def _row_rmsnorm(x, g):
    ms = jnp.mean(x * x, axis=-1, keepdims=True)
    return x * lax.rsqrt(ms + EPS) * g


def _mod_kernel(c_ref, w_ref, b_ref, o_ref):
    c = c_ref[...]
    act = (c * _sigmoid(c)).astype(BF16)
    o_ref[0] = _dot(act, w_ref[0]) + b_ref[0]


def _mod_all(cc, w_mod, b_mod):
    n_layers, _, n_out = w_mod.shape
    rows = cc.shape[0]
    tn = 1024
    return pl.pallas_call(
        _mod_kernel,
        grid=(n_layers, n_out // tn),
        in_specs=[
            pl.BlockSpec((rows, D_MODEL), lambda l, j: (0, 0)),
            pl.BlockSpec((1, D_MODEL, tn), lambda l, j: (l, 0, j)),
            pl.BlockSpec((1, 1, tn), lambda l, j: (l, 0, j)),
        ],
        out_specs=pl.BlockSpec((1, rows, tn), lambda l, j: (l, 0, j)),
        out_shape=jax.ShapeDtypeStruct((n_layers, rows, n_out), F32),
        compiler_params=_cparams(("arbitrary", "arbitrary")),
        name="mod_all",
    )(cc, w_mod, b_mod)


def _head_rmsnorm(y, n0, split=None, n1=None):
    lane = lax.broadcasted_iota(jnp.int32, (1, LANES), 1)
    outs = []
    for c in range(y.shape[1] // LANES):
        yc = y[:, c * LANES:(c + 1) * LANES]
        sq = yc * yc
        if split is None:
            ms = jnp.sum(sq, axis=-1, keepdims=True) * (1.0 / n0)
        else:
            low = lane < split
            ms = jnp.where(low, jnp.sum(jnp.where(low, sq, 0.0), axis=-1, keepdims=True) * (1.0 / n0),
                           jnp.sum(jnp.where(low, 0.0, sq), axis=-1, keepdims=True) * (1.0 / n1))
        outs.append(yc * lax.rsqrt(ms + EPS))
    return outs[0] if len(outs) == 1 else jnp.concatenate(outs, axis=1)


def _rope(y, cos, sin_signed, half):
    outs = []
    lane = lax.broadcasted_iota(jnp.int32, (y.shape[0], LANES), 1)
    first = jnp.bitwise_and(lane, 2 * half - 1) < half
    for c in range(y.shape[1] // LANES):
        yc = y[:, c * LANES:(c + 1) * LANES]
        fwd = pltpu.roll(yc, LANES - half, 1)
        bwd = pltpu.roll(yc, half, 1)
        partner = jnp.where(first, fwd, bwd)
        outs.append(yc * cos + partner * sin_signed)
    return outs[0] if len(outs) == 1 else jnp.concatenate(outs, axis=1)


def _store_q(ref, y):
    for h in range(y.shape[1] // LANES):
        ref[0, h] = y[:, h * LANES:(h + 1) * LANES].astype(ref.dtype)


def _store_kt(ref, y):
    for h in range(y.shape[1] // LANES):
        ref[0, h] = y[:, h * LANES:(h + 1) * LANES].T.astype(ref.dtype)


def _store_v(ref, y):
    lane = lax.broadcasted_iota(jnp.int32, (1, LANES), 1)
    ones_col = jnp.where(lane == HEAD_DIM, 1.0, 0.0)
    for h in range(y.shape[1] // LANES):
        ref[0, h] = (y[:, h * LANES:(h + 1) * LANES] + ones_col).astype(ref.dtype)


def _inproj_kernel(x_ref, mod_ref, g1_ref, w_ref, wqb_ref, wkvb_ref, gv_ref, tab_ref,
                   qna_ref, kna_ref, vna_ref, qdf_ref, kdf_ref, vdf_ref,
                   qgq_ref, kgq_ref, vgq_ref, qml_ref, kml_ref, vml_ref):
    x = x_ref[0]
    mod = mod_ref[0]
    h = (_row_rmsnorm(x, g1_ref[...]) * (1.0 + mod[1:2]) + mod[0:1]).astype(BF16)

    one64 = dict(n0=HEAD_DIM)
    two32 = dict(n0=DIFF_DH, split=DIFF_DH, n1=DIFF_DH)
    nope_rope = dict(n0=MLA_NOPE, split=MLA_NOPE, n1=MLA_ROPE)
    tab = tab_ref[...]
    cos_d, sin_d = tab[:, 0:128], tab[:, 128:256]
    cos_g, sin_g = tab[:, 256:384], tab[:, 384:512]
    cos_m, sin_m = tab[:, 512:640], tab[:, 640:768]

    def proj(lo, width):
        return _dot(h, w_ref[:, lo:lo + width])

    def normed(y, groups, row):
        return _head_rmsnorm(y, **groups) * gv_ref[row:row + 1, :y.shape[1]]

    _store_q(qna_ref, normed(proj(0, 512), one64, 0))
    _store_kt(kna_ref, normed(proj(512, 512), one64, 1))
    _store_v(vna_ref, proj(1024, 512))
    _store_q(qdf_ref, _rope(normed(proj(1536, 512), two32, 2), cos_d, sin_d, DIFF_DH // 2))
    _store_kt(kdf_ref, _rope(normed(proj(2048, 512), two32, 3), cos_d, sin_d, DIFF_DH // 2))
    _store_v(vdf_ref, proj(2560, 512))
    _store_q(qgq_ref, _rope(normed(proj(3072, 512), one64, 4), cos_g, sin_g, HEAD_DIM // 2))
    _store_kt(kgq_ref, _rope(normed(proj(3584, 256), one64, 5), cos_g, sin_g, HEAD_DIM // 2))
    _store_v(vgq_ref, proj(3840, 256))
    qa = proj(4096, 256)
    ms_qa = jnp.sum(qa * qa, axis=-1, keepdims=True) * (1.0 / MLA_Q_RANK)
    qa_n = (qa * lax.rsqrt(ms_qa + EPS) * gv_ref[8:9, :256]).astype(BF16)
    q_ml = _dot(qa_n, wqb_ref[...])
    _store_q(qml_ref, _rope(normed(q_ml, nope_rope, 6), cos_m, sin_m, MLA_ROPE // 2))
    kva = proj(4352, 128)
    kva_n = _row_rmsnorm(kva, gv_ref[9:10, :128]).astype(BF16)
    kv = _dot(kva_n, wkvb_ref[...])
    k_nope = normed(kv[:, :512], one64, 7)
    kr = proj(4480, 128)
    kr_r = _rope(normed(kr, dict(n0=MLA_ROPE), 10), cos_m, sin_m, MLA_ROPE // 2)
    _store_kt(kml_ref, k_nope + jnp.concatenate([kr_r] * N_HEADS, axis=1))
    _store_v(vml_ref, kv[:, 512:])


def _inproj(x_all, mods, g1, lw, tables):
    bsz, t_all, _ = x_all.shape
    nt = t_all // TM
    ctx_tile = nt - 1
    ctx_row = mods.shape[0] - HALO

    def full(a):
        nd = a.ndim
        return pl.BlockSpec(a.shape, lambda t, b: (0,) * nd)

    def rows_spec(nh):
        return (pl.BlockSpec((1, nh, TM, LANES), lambda t, b: (b, 0, t, 0)),
                jax.ShapeDtypeStruct((bsz, nh, t_all, LANES), BF16))

    def cols_spec(nh):
        return (pl.BlockSpec((1, nh, LANES, TM), lambda t, b: (b, 0, 0, t)),
                jax.ShapeDtypeStruct((bsz, nh, LANES, t_all), BF16))

    outs = []
    for n_kv in (N_HEADS, N_HEADS, GQA_KV_HEADS, N_HEADS):
        outs += [rows_spec(N_HEADS), cols_spec(n_kv), rows_spec(n_kv)]
    return pl.pallas_call(
        _inproj_kernel,
        grid=(nt, bsz),
        in_specs=[
            pl.BlockSpec((1, TM, D_MODEL), lambda t, b: (b, t, 0)),
            pl.BlockSpec((1, N_MOD, D_MODEL), lambda t, b: (jnp.where(t == ctx_tile, ctx_row, b), 0, 0)),
            full(g1), full(lw["w_all"]), full(lw["w_qb"]), full(lw["w_kvb"]), full(lw["gvec"]),
            pl.BlockSpec((TM, tables.shape[1]), lambda t, b: (t, 0)),
        ],
        out_specs=[o[0] for o in outs],
        out_shape=[o[1] for o in outs],
        compiler_params=_cparams(("arbitrary", "arbitrary")),
        name="inproj",
    )(x_all, mods, g1, lw["w_all"], lw["w_qb"], lw["w_kvb"], lw["gvec"], tables)


def _softmax_pv(q, kt, v, bias=None):
    s = _dot(q, kt)
    if bias is not None:
        s = s + bias
    p = jnp.exp2(s - jnp.max(s, axis=-1, keepdims=True)).astype(BF16)
    o = _dot(p, v)
    return o[:, :HEAD_DIM] * (1.0 / o[:, HEAD_DIM:HEAD_DIM + 1])


def _attn_kernel(seq, lambda_init, qna_ref, kna_ref, vna_ref, qdf_ref, kdf_ref, vdf_ref,
                 qgq_ref, kgq_ref, vgq_ref, qml_ref, kml_ref, vml_ref, bias_ref, lam_ref, sub_ref, o_ref):
    qi = pl.program_id(1)
    t_all = vna_ref.shape[2]
    n_lat_tiles = seq // TM
    lp = lam_ref[...]
    lam = (jnp.exp(jnp.sum(lp[0:1] * lp[1:2], axis=-1, keepdims=True))
           - jnp.exp(jnp.sum(lp[2:3] * lp[3:4], axis=-1, keepdims=True)) + lambda_init)
    lane = lax.broadcasted_iota(jnp.int32, (TM, LANES), 1)

    def plain_heads(q_ref, kt_ref, v_ref, k_lo):
        kv_shift = 0 if v_ref.shape[1] == N_HEADS else 1
        return [_softmax_pv(q_ref[0, h], kt_ref[0, h >> kv_shift, :, k_lo:t_all],
                            v_ref[0, h >> kv_shift, k_lo:t_all, :]) for h in range(N_HEADS)]

    def diff_heads(k_lo):
        outs = []
        for h in range(N_HEADS):
            kt = kdf_ref[0, h, :, k_lo:t_all]
            v = vdf_ref[0, h, k_lo:t_all, :]
            qf = qdf_ref[0, h].astype(F32)
            q0 = jnp.where(lane < DIFF_DH, qf, 0.0).astype(BF16)
            q1 = jnp.where(lane >= DIFF_DH, qf, 0.0).astype(BF16)
            o = _softmax_pv(q0, kt, v) - lam * _softmax_pv(q1, kt, v)
            ms = jnp.mean(o * o, axis=-1, keepdims=True)
            outs.append(o * lax.rsqrt(ms + EPS) * sub_ref[...] * (1.0 - lambda_init))
        return outs

    def band_heads():
        ws = pl.multiple_of(jnp.clip(NA_ROWS_PER_TILE * qi - NA_WIN_R // 2, 0,
                                     seq // GRID_W - NA_WIN_ROWS) * GRID_W, TM)
        n_win = NA_WIN_ROWS * GRID_W
        outs = []
        for h in range(N_HEADS):
            kt = jnp.concatenate([kna_ref[0, h, :, pl.ds(ws, n_win)], kna_ref[0, h, :, seq:t_all]], axis=1)
            v = jnp.concatenate([vna_ref[0, h, pl.ds(ws, n_win), :], vna_ref[0, h, seq:t_all, :]], axis=0)
            outs.append(_softmax_pv(qna_ref[0, h], kt, v, bias_ref[0, h]))
        return outs

    def run(k_lo, latent):
        outs = band_heads() if latent else plain_heads(qna_ref, kna_ref, vna_ref, k_lo)
        outs += diff_heads(k_lo)
        outs += plain_heads(qgq_ref, kgq_ref, vgq_ref, k_lo)
        outs += plain_heads(qml_ref, kml_ref, vml_ref, k_lo)
        o_ref[0] = jnp.concatenate(outs, axis=1).astype(o_ref.dtype)

    @pl.when(qi < n_lat_tiles)
    def _():
        run(0, True)

    @pl.when(qi >= n_lat_tiles)
    def _():
        run(seq, False)


def _attention(qkv, bias, lam_p, sub_g, seq, n_q_tiles, lambda_init):
    bsz, _, t_all, _ = qkv[0].shape
    n_lat_tiles = seq // TM
    n_cls = bias.shape[0]
    in_specs = []
    for j in range(0, len(qkv), 3):
        hk = qkv[j + 2].shape[1]
        in_specs += [
            pl.BlockSpec((1, N_HEADS, TM, LANES), lambda b, i: (b, 0, i, 0)),
            pl.BlockSpec((1, hk, LANES, t_all), lambda b, i: (b, 0, 0, 0)),
            pl.BlockSpec((1, hk, t_all, LANES), lambda b, i: (b, 0, 0, 0)),
        ]
    in_specs += [
        pl.BlockSpec((1,) + bias.shape[1:],
                     lambda b, i: (jnp.where(i == 0, 0, jnp.where(i >= n_lat_tiles - 1, n_cls - 1, 1)), 0, 0, 0)),
        pl.BlockSpec(lam_p.shape, lambda b, i: (0, 0)),
        pl.BlockSpec(sub_g.shape, lambda b, i: (0, 0)),
    ]
    width = 4 * N_HEADS * HEAD_DIM
    return pl.pallas_call(
        functools.partial(_attn_kernel, seq, lambda_init),
        grid=(bsz, n_q_tiles),
        in_specs=in_specs,
        out_specs=pl.BlockSpec((1, TM, width), lambda b, i: (b, i, 0)),
        out_shape=jax.ShapeDtypeStruct((bsz, t_all, width), BF16),
        compiler_params=_cparams(("arbitrary", "arbitrary")),
        name="attn",
    )(*qkv, bias, lam_p, sub_g)


def _merge_kernel(x_ref, mod_ref, g1_ref, y_ref, wg_ref, wb_ref, wo_ref, o_ref):
    x = x_ref[0]
    mod = mod_ref[0]
    h = (_row_rmsnorm(x, g1_ref[...]) * (1.0 + mod[1:2]) + mod[0:1]).astype(BF16)
    acc = None
    width = wb_ref.shape[1]
    for i in range(wb_ref.shape[0]):
        term = _sigmoid(_dot(h, wg_ref[i])) * _dot(y_ref[0, :, i * width:(i + 1) * width], wb_ref[i])
        acc = term if acc is None else acc + term
    o_ref[0] = x + mod[2:3] * _dot(acc.astype(BF16), wo_ref[...])


def _merge(x_all, mods, g1, y, lw, n_tiles):
    bsz, t_all, _ = x_all.shape
    ctx_tile = t_all // TM - 1
    ctx_row = mods.shape[0] - HALO

    def full(a):
        nd = a.ndim
        return pl.BlockSpec(a.shape, lambda b, t: (0,) * nd)

    tok = lambda w: pl.BlockSpec((1, TM, w), lambda b, t: (b, t, 0))
    return pl.pallas_call(
        _merge_kernel,
        grid=(bsz, n_tiles),
        in_specs=[
            tok(D_MODEL),
            pl.BlockSpec((1, N_MOD, D_MODEL), lambda b, t: (jnp.where(t == ctx_tile, ctx_row, b), 0, 0)),
            full(g1), tok(y.shape[2]),
            full(lw["w_gate"]), full(lw["w_branch"]), full(lw["w_out"]),
        ],
        out_specs=tok(D_MODEL),
        out_shape=jax.ShapeDtypeStruct(x_all.shape, F32),
        input_output_aliases={0: 0},
        compiler_params=_cparams(("arbitrary", "arbitrary")),
        name="merge",
    )(x_all, mods, g1, y, lw["w_gate"], lw["w_branch"], lw["w_out"])


def _ffn_kernel(seq, t_all, x_ref, xp_ref, xn_ref, mod_ref, g2_ref, wu_ref, cw_ref, wd_ref, o_ref, act_scr):
    t = pl.program_id(1)
    x = x_ref[0]
    mod = mod_ref[0]
    x_ext = jnp.concatenate([x, xp_ref[0], xn_ref[0]], axis=0)
    hx = (_row_rmsnorm(x_ext, g2_ref[...]) * (1.0 + mod[4:5]) + mod[3:4]).astype(BF16)

    row = lax.broadcasted_iota(jnp.int32, (TM, FF_CHUNK), 0)
    g_first = t * TM
    has_prev = jnp.logical_and(g_first != 0, g_first != seq)
    has_next = jnp.logical_and(g_first + TM != seq, g_first + TM != t_all)
    first_row = row == 0
    last_row = row == TM - 1

    def conv(lo):
        u_ext = _dot(hx, wu_ref[:, lo:lo + FF_CHUNK])
        c = cw_ref[:, lo:lo + FF_CHUNK]
        u = u_ext[0:TM]
        prev_row = jnp.where(has_prev, u_ext[TM + HALO - 1:TM + HALO], 0.0)
        next_row = jnp.where(has_next, u_ext[TM + HALO:TM + HALO + 1], 0.0)
        up = jnp.where(first_row, prev_row, pltpu.roll(u, 1, 0))
        dn = jnp.where(last_row, next_row, pltpu.roll(u, TM - 1, 0))
        return c[3:4] + up * c[0:1] + u * c[1:2] + dn * c[2:3]

    for j in range(D_FF // FF_CHUNK):
        lo = j * FF_CHUNK
        gate = conv(lo)
        val = conv(D_FF + lo)
        act_scr[:, lo:lo + FF_CHUNK] = (gate * _sigmoid(gate) * val).astype(BF16)
    o_ref[0] = x + mod[5:6] * _dot(act_scr[...], wd_ref[...])


def _ffn(x_all, mods, g2, lw, seq, n_tiles, out_rows):
    bsz, t_all, _ = x_all.shape
    nt_all = t_all // TM
    ctx_tile = nt_all - 1
    ctx_row = mods.shape[0] - HALO
    blocks_per_tile = TM // HALO
    last_block = t_all // HALO - 1

    def full(a):
        nd = a.ndim
        return pl.BlockSpec(a.shape, lambda b, t: (0,) * nd)

    return pl.pallas_call(
        functools.partial(_ffn_kernel, seq, t_all),
        grid=(bsz, n_tiles),
        in_specs=[
            pl.BlockSpec((1, TM, D_MODEL), lambda b, t: (b, t, 0)),
            pl.BlockSpec((1, HALO, D_MODEL), lambda b, t: (b, jnp.maximum(t * blocks_per_tile - 1, 0), 0)),
            pl.BlockSpec((1, HALO, D_MODEL), lambda b, t: (b, jnp.minimum((t + 1) * blocks_per_tile, last_block), 0)),
            pl.BlockSpec((1, N_MOD, D_MODEL), lambda b, t: (jnp.where(t == ctx_tile, ctx_row, b), 0, 0)),
            full(g2), full(lw["w_up"]), full(lw["conv"]), full(lw["w_down"]),
        ],
        out_specs=pl.BlockSpec((1, TM, D_MODEL), lambda b, t: (b, t, 0)),
        out_shape=jax.ShapeDtypeStruct((bsz, out_rows, D_MODEL), F32),
        scratch_shapes=[pltpu.VMEM((TM, D_FF), BF16)],
        compiler_params=_cparams(("arbitrary", "arbitrary")),
        name="ffn",
    )(x_all, x_all, x_all, mods, g2, lw["w_up"], lw["conv"], lw["w_down"])


def _pad_heads(w, n_heads, d):
    lead = w.shape[:-1]
    w = w.reshape(lead + (n_heads, d))
    w = jnp.pad(w, [(0, 0)] * len(lead) + [(0, 0), (0, LANES - d)])
    return w.reshape(lead + (n_heads * LANES,))


def _pad_lanes(v, width):
    return jnp.pad(v, (0, width - v.shape[0]))


def _rope_table(n_lat, n_ctx, rot_dim, lane0, lanes_used):
    t = jnp.arange(n_lat)
    row = (t // GRID_W).astype(F32)
    col = (t % GRID_W).astype(F32)
    n_axis = rot_dim // 4
    inv_freq = ROPE_THETA ** (-jnp.arange(n_axis, dtype=F32) / n_axis)
    ang = jnp.concatenate([row[:, None] * inv_freq, col[:, None] * inv_freq], axis=-1)
    cos_g = jnp.concatenate([jnp.cos(ang), jnp.cos(ang)], axis=-1)
    sin_g = jnp.concatenate([-jnp.sin(ang), jnp.sin(ang)], axis=-1)
    reps = lanes_used // rot_dim
    cos = jnp.ones((n_lat, LANES), F32).at[:, lane0:lane0 + lanes_used].set(jnp.tile(cos_g, (1, reps)))
    sin = jnp.zeros((n_lat, LANES), F32).at[:, lane0:lane0 + lanes_used].set(jnp.tile(sin_g, (1, reps)))
    cos = jnp.concatenate([cos, jnp.ones((n_ctx, LANES), F32)], axis=0)
    sin = jnp.concatenate([sin, jnp.zeros((n_ctx, LANES), F32)], axis=0)
    return cos, sin


def _rope_tables(n_lat, n_ctx):
    cd, sd = _rope_table(n_lat, n_ctx, DIFF_DH, 0, 2 * DIFF_DH)
    cg, sg = _rope_table(n_lat, n_ctx, HEAD_DIM, 0, HEAD_DIM)
    cm, sm = _rope_table(n_lat, n_ctx, MLA_ROPE, MLA_NOPE, MLA_ROPE)
    return jnp.concatenate([cd, sd, cg, sg, cm, sm], axis=1)


def _na_bias_tables(rel_bias, rows, n_ctx):
    cq = jnp.arange(GRID_W)
    col_start = jnp.clip(cq - NA_WIN_C // 2, 0, GRID_W - NA_WIN_C)
    ck = jnp.arange(GRID_W)
    col_in = (ck[None, :] >= col_start[:, None]) & (ck[None, :] < col_start[:, None] + NA_WIN_C)
    col_idx = jnp.clip(ck[None, :] - cq[:, None], -(NA_WIN_C - 1), NA_WIN_C - 1) + NA_WIN_C - 1
    col_sel = jax.nn.one_hot(col_idx.reshape(-1), 2 * NA_WIN_C - 1, dtype=F32)
    tables = []
    for r0 in (0, NA_ROWS_PER_TILE, rows - NA_ROWS_PER_TILE):
        ws = min(max(r0 - NA_WIN_R // 2, 0), rows - NA_WIN_ROWS)
        r = r0 + jnp.arange(NA_ROWS_PER_TILE)
        kr = ws + jnp.arange(NA_WIN_ROWS)
        rs = jnp.clip(r - NA_WIN_R // 2, 0, rows - NA_WIN_R)
        in_band = (kr[None, :] >= rs[:, None]) & (kr[None, :] < rs[:, None] + NA_WIN_R)
        row_idx = jnp.clip(kr[None, :] - r[:, None] + NA_WIN_R - 1, 0, 2 * NA_WIN_R - 2)
        row_sel = jax.nn.one_hot(row_idx.reshape(-1), 2 * NA_WIN_R - 1, dtype=F32)
        b = jnp.einsum("pa,lhac,qc->lhpq", row_sel, rel_bias, col_sel, precision=lax.Precision.HIGHEST)
        b = b.reshape(rel_bias.shape[:2] + (NA_ROWS_PER_TILE, NA_WIN_ROWS, GRID_W, GRID_W)).transpose(0, 1, 2, 4, 3, 5)
        ok = in_band[:, None, :, None] & col_in[None, :, None, :]
        b = jnp.where(ok[None, None], b * LOG2E, NEG_INF)
        b = b.reshape(rel_bias.shape[:2] + (TM, NA_WIN_ROWS * GRID_W))
        tables.append(jnp.concatenate([b, jnp.zeros(rel_bias.shape[:2] + (TM, n_ctx), F32)], axis=-1))
    return jnp.stack(tables, axis=1)


def _layer_weights(i, p):
    w_in = p["w_in"][i]
    offs = [0]
    for n in IN_SIZES:
        offs.append(offs[-1] + n)
    seg = [w_in[:, offs[j]:offs[j + 1]] for j in range(len(IN_SIZES))]
    kr_cols = jnp.pad(seg[11], ((0, 0), (MLA_NOPE, LANES - MLA_NOPE - MLA_ROPE)))
    w_all = jnp.concatenate(
        [_pad_heads(seg[0], 4, 64), _pad_heads(seg[1], 4, 64), _pad_heads(seg[2], 4, 64),
         _pad_heads(seg[3], 4, 64), _pad_heads(seg[4], 4, 64), _pad_heads(seg[5], 4, 64),
         _pad_heads(seg[6], 4, 64), _pad_heads(seg[7], 2, 64), _pad_heads(seg[8], 2, 64),
         jnp.pad(seg[9], ((0, 0), (0, 256 - MLA_Q_RANK))), seg[10], kr_cols], axis=1).astype(BF16)

    w_qb = jnp.pad(_pad_heads(p["w_mla_qb"][i], N_HEADS, MLA_QK), ((0, 256 - MLA_Q_RANK), (0, 0))).astype(BF16)
    kvb = p["w_mla_kvb"][i].reshape(MLA_KV_RANK, N_HEADS, MLA_NOPE + MLA_V)
    w_kvb = jnp.concatenate([_pad_heads(kvb[:, :, :MLA_NOPE].reshape(MLA_KV_RANK, -1), N_HEADS, MLA_NOPE),
                             _pad_heads(kvb[:, :, MLA_NOPE:].reshape(MLA_KV_RANK, -1), N_HEADS, MLA_V)],
                            axis=1).astype(BF16)

    def head_gain(g, n_heads, scale=1.0):
        return _pad_lanes(jnp.tile(_pad_lanes(g * scale, LANES), n_heads), 512)

    dq = jnp.tile(p["diff_qk_g"][i], (1, 2))
    mq = p["mla_qk_g"][i]
    gvec = jnp.stack([
        head_gain(p["na_qk_g"][i, 0], 4, HEAD_DIM ** -0.5 * LOG2E), head_gain(p["na_qk_g"][i, 1], 4),
        head_gain(dq[0], 4, DIFF_DH ** -0.5 * LOG2E), head_gain(dq[1], 4),
        head_gain(p["gqa_qk_g"][i, 0], 4, HEAD_DIM ** -0.5 * LOG2E), head_gain(p["gqa_qk_g"][i, 1], 2),
        head_gain(mq[0], 4, MLA_QK ** -0.5 * LOG2E), head_gain(_pad_lanes(mq[1, :MLA_NOPE], MLA_QK), 4),
        _pad_lanes(p["mla_qa_g"][i], 512), _pad_lanes(p["mla_kva_g"][i], 512),
        _pad_lanes(jnp.pad(mq[1, MLA_NOPE:], (MLA_NOPE, 0)), 512),
    ] + [jnp.zeros((512,), F32)] * 5)
    conv = jnp.concatenate([p["conv_w"][i], p["conv_b"][i][None], jnp.zeros((4, 2 * D_FF), F32)], axis=0)
    return dict(
        w_all=w_all, w_qb=w_qb, w_kvb=w_kvb, gvec=gvec,
        w_gate=p["w_gate"][i].astype(BF16), w_branch=p["w_branch"][i].astype(BF16), w_out=p["w_out"][i].astype(BF16),
        w_up=p["w_up"][i].astype(BF16), conv=conv, w_down=p["w_down"][i].astype(BF16),
        sub_g=p["diff_subln_g"][i][None],
    )


def kernel(x, c, ctx, c_ctx, w_mod, b_mod, norm1_g, norm2_g, w_in, na_qk_g, na_rel_bias, diff_qk_g, diff_lambda, diff_subln_g, gqa_qk_g, mla_qa_g, mla_kva_g, w_mla_qb, w_mla_kvb, mla_qk_g, w_gate, w_branch, w_out, w_up, conv_w, conv_b, w_down):
    p = dict(w_in=w_in, na_qk_g=na_qk_g, diff_qk_g=diff_qk_g, gqa_qk_g=gqa_qk_g, mla_qa_g=mla_qa_g,
             mla_kva_g=mla_kva_g, w_mla_qb=w_mla_qb, w_mla_kvb=w_mla_kvb, mla_qk_g=mla_qk_g, w_gate=w_gate,
             w_branch=w_branch, w_out=w_out, w_up=w_up, conv_w=conv_w, conv_b=conv_b, w_down=w_down,
             diff_subln_g=diff_subln_g)
    bsz, seq, d = x.shape
    n_ctx = ctx.shape[1]
    depth = w_mod.shape[0]
    assert d == D_MODEL and seq % TM == 0 and n_ctx == TM and seq // GRID_W >= NA_WIN_ROWS
    t_all = seq + n_ctx
    n_lat_tiles = seq // TM

    x_all = jnp.concatenate([x, ctx], axis=1)
    cc = jnp.concatenate([c, c_ctx[None], jnp.zeros((HALO - 1, d), F32)], axis=0)
    mods_all = _mod_all(cc, w_mod.astype(BF16), b_mod[:, None, :])
    mods_all = mods_all.reshape(depth, bsz + HALO, N_MOD, d)
    tables = _rope_tables(seq, n_ctx)
    na_bias = _na_bias_tables(na_rel_bias, seq // GRID_W, n_ctx)

    for i in range(depth):
        last = i == depth - 1
        n_tiles = n_lat_tiles if last else n_lat_tiles + 1
        lambda_init = 0.8 - 0.6 * math.exp(-0.3 * i)
        lw = _layer_weights(i, p)
        mods = mods_all[i]
        g1 = norm1_g[i][None]
        g2 = norm2_g[i][None]
        qkv = _inproj(x_all, mods, g1, lw, tables)
        y = _attention(qkv, na_bias[i], diff_lambda[i], lw["sub_g"], seq, n_tiles, lambda_init)
        x_all = _merge(x_all, mods, g1, y, lw, n_tiles)
        x_all = _ffn(x_all, mods, g2, lw, seq, n_tiles, seq if last else t_all)
    return x_all
```

```python
import functools
import math

import jax
import jax.numpy as jnp
from jax import lax
from jax.experimental import pallas as pl
from jax.experimental.pallas import tpu as pltpu

D_MODEL = 1024
GRID_W = 64
HEAD_DIM = 64
EPS = 1e-6
ROPE_THETA = 10000.0
NEG_INF = -1e30
NA_WIN_R = 8
NA_WIN_C = 16
DIFF_DH = 32
MLA_Q_RANK = 192
MLA_KV_RANK = 128
MLA_NOPE = 64
MLA_ROPE = 32
MLA_V = 64
MLA_QK = MLA_NOPE + MLA_ROPE
N_HEADS = 4
GQA_KV_HEADS = 2
N_MOD = 6
D_FF = 2816
IN_SIZES = (256, 256, 256, 256, 256, 256, 256, 128, 128, MLA_Q_RANK, MLA_KV_RANK, MLA_ROPE)

LANES = 128
MXU_DIM = 256
TM = 256
FF_CHUNK = 256
HALO = 8
LOG2E = math.log2(math.e)
BRANCH_LAYOUT = ((256, 256, 4), (256, 256, 4), (256, 128, 2), (512, 512, 4))
NA_ROWS_PER_TILE = TM // GRID_W
NA_WIN_ROWS = NA_WIN_R + NA_ROWS_PER_TILE
VMEM_LIMIT = 56 * 1024 * 1024

BF16 = jnp.bfloat16
F32 = jnp.float32


def _cparams(sem):
    return pltpu.CompilerParams(dimension_semantics=sem, vmem_limit_bytes=VMEM_LIMIT)


def _dot(a, b):
    return jnp.dot(a, b, preferred_element_type=F32)


def _sigmoid(z):
    return 1.0 / (1.0 + jnp.exp(-z))


def _row_rmsnorm(x, g):
    ms = jnp.mean(x * x, axis=-1, keepdims=True)
    return x * lax.rsqrt(ms + EPS) * g


def _mod_kernel(c_ref, w_ref, b_ref, o_ref):
    c = c_ref[...]
    act = (c * _sigmoid(c)).astype(BF16)
    o_ref[0] = _dot(act, w_ref[0]) + b_ref[0]


def _mod_all(cc, w_mod, b_mod):
    n_layers, _, n_out = w_mod.shape
    rows = cc.shape[0]
    tn = 1024
    return pl.pallas_call(
        _mod_kernel,
        grid=(n_layers, n_out // tn),
        in_specs=[
            pl.BlockSpec((rows, D_MODEL), lambda l, j: (0, 0)),
            pl.BlockSpec((1, D_MODEL, tn), lambda l, j: (l, 0, j)),
            pl.BlockSpec((1, 1, tn), lambda l, j: (l, 0, j)),
        ],
        out_specs=pl.BlockSpec((1, rows, tn), lambda l, j: (l, 0, j)),
        out_shape=jax.ShapeDtypeStruct((n_layers, rows, n_out), F32),
        compiler_params=_cparams(("arbitrary", "arbitrary")),
        name="mod_all",
    )(cc, w_mod, b_mod)


def _head_rmsnorm(y, groups, y_rot=None, g=None, g_rot=None, cos=None, sin_signed=None):
    lane = lax.broadcasted_iota(jnp.int32, (1, LANES), 1)
    outs = []
    for c in range(y.shape[1] // LANES):
        yc = y[:, c * LANES:(c + 1) * LANES]
        sq = yc * yc
        if len(groups) == 1:
            ms = jnp.sum(sq, axis=-1, keepdims=True) * (1.0 / groups[0][1])
        else:
            ms = 0.0
            for first, size in groups:
                shift = size.bit_length() - 1
                member = lax.shift_right_logical(lane, shift) == (first >> shift)
                ms = jnp.where(member, jnp.sum(jnp.where(member, sq, 0.0), axis=-1, keepdims=True) * (1.0 / size), ms)
        r = lax.rsqrt(ms + EPS)
        outs.append(yc * r if y_rot is None else
                    (yc * r * g[:, c * LANES:(c + 1) * LANES]) * cos
                    + (y_rot[:, c * LANES:(c + 1) * LANES] * r * g_rot[:, c * LANES:(c + 1) * LANES]) * sin_signed)
    return outs[0] if len(outs) == 1 else jnp.concatenate(outs, axis=1)


def _store_q(ref, y):
    ref[0] = y.astype(ref.dtype)


def _store_kt(ref, y):
    ref[0] = y.T.astype(ref.dtype)


def _store_v(ref, y):
    lane = lax.broadcasted_iota(jnp.int32, (1, LANES), 1)
    ones_col = jnp.where(lane == HEAD_DIM, 1.0, 0.0)
    for h in range(y.shape[1] // LANES):
        ref[0, h] = (y[:, h * LANES:(h + 1) * LANES] + ones_col).astype(ref.dtype)


def _inproj_kernel(x_ref, mod_ref, g1_ref, w_ref, wqb_ref, wkvb_ref, gv_ref, tab_ref,
                   qna_ref, kna_ref, vna_ref, qdf_ref, kdf_ref, vdf_ref,
                   qgq_ref, kgq_ref, vgq_ref, qml_ref, kml_ref, vml_ref):
    x = x_ref[0]
    mod = mod_ref[0]
    h = (_row_rmsnorm(x, g1_ref[...]) * (1.0 + mod[1:2]) + mod[0:1]).astype(BF16)

    heads64 = ((0, 64), (64, 64))
    maps32 = ((0, 32), (32, 32), (64, 32), (96, 32))
    nope_only = ((0, MLA_NOPE),)
    nope_rope = ((0, MLA_NOPE), (MLA_NOPE, MLA_ROPE))
    rope_only = ((MLA_NOPE, MLA_ROPE),)
    tab = tab_ref[...]
    cos_d, sin_d = tab[:, 0:128], tab[:, 128:256]
    cos_g, sin_g = tab[:, 256:384], tab[:, 384:512]
    cos_m, sin_m = tab[:, 512:640], tab[:, 640:768]

    def proj(lo, width):
        return _dot(h, w_ref[:, lo:lo + width])

    def gain(row, width):
        return gv_ref[row:row + 1, :width]

    def normed(y, groups, row):
        return _head_rmsnorm(y, groups) * gain(row, y.shape[1])

    def roped(y, y_rot, groups, row, row_rot, cos, sin):
        w = y.shape[1]
        return _head_rmsnorm(y, groups, y_rot=y_rot, g=gain(row, w), g_rot=gain(row_rot, w), cos=cos, sin_signed=sin)

    _store_q(qna_ref, normed(proj(0, 256), heads64, 0))
    _store_kt(kna_ref, normed(proj(256, 256), heads64, 1))
    _store_v(vna_ref, proj(512, 512))
    _store_q(qdf_ref, roped(proj(1024, 256), proj(3200, 256), maps32, 2, 11, cos_d, sin_d))
    _store_kt(kdf_ref, roped(proj(1280, 256), proj(3456, 256), maps32, 3, 12, cos_d, sin_d))
    _store_v(vdf_ref, proj(1536, 512))
    _store_q(qgq_ref, roped(proj(2048, 256), proj(3712, 256), heads64, 4, 13, cos_g, sin_g))
    _store_kt(kgq_ref, roped(proj(2304, 128), proj(3968, 128), heads64, 5, 14, cos_g, sin_g))
    _store_v(vgq_ref, proj(2432, 256))
    qa = proj(2688, 256)
    ms_qa = jnp.sum(qa * qa, axis=-1, keepdims=True) * (1.0 / MLA_Q_RANK)
    qa_n = (qa * lax.rsqrt(ms_qa + EPS) * gain(8, 256)).astype(BF16)
    q_ml = _dot(qa_n, wqb_ref[:, :512])
    q_ml_rot = _dot(qa_n, wqb_ref[:, 512:])
    _store_q(qml_ref, roped(q_ml, q_ml_rot, nope_rope, 6, 15, cos_m, sin_m))
    kva = proj(2944, 128)
    kva_n = _row_rmsnorm(kva, gain(9, 128)).astype(BF16)
    kv = _dot(kva_n, wkvb_ref[...])
    k_nope = normed(kv[:, :512], nope_only, 7)
    kr_r = roped(proj(3072, 128), proj(4096, 128), rope_only, 10, 16, cos_m, sin_m)
    _store_kt(kml_ref, k_nope + jnp.concatenate([kr_r] * N_HEADS, axis=1))
    _store_v(vml_ref, kv[:, 512:])


def _inproj(x_all, mods, g1, lw, tables):
    bsz, t_all, _ = x_all.shape
    nt = t_all // TM
    ctx_tile = nt - 1
    ctx_row = mods.shape[0] - HALO

    def full(a):
        nd = a.ndim
        return pl.BlockSpec(a.shape, lambda t, b: (0,) * nd)

    def q_spec(width):
        return (pl.BlockSpec((1, TM, width), lambda t, b: (b, t, 0)),
                jax.ShapeDtypeStruct((bsz, t_all, width), BF16))

    def kt_spec(width):
        return (pl.BlockSpec((1, width, TM), lambda t, b: (b, 0, t)),
                jax.ShapeDtypeStruct((bsz, width, t_all), BF16))

    def v_spec(nh):
        return (pl.BlockSpec((1, nh, TM, LANES), lambda t, b: (b, 0, t, 0)),
                jax.ShapeDtypeStruct((bsz, nh, t_all, LANES), BF16))

    outs = []
    for wq, wk, n_kv in BRANCH_LAYOUT:
        outs += [q_spec(wq), kt_spec(wk), v_spec(n_kv)]
    return pl.pallas_call(
        _inproj_kernel,
        grid=(nt, bsz),
        in_specs=[
            pl.BlockSpec((1, TM, D_MODEL), lambda t, b: (b, t, 0)),
            pl.BlockSpec((1, N_MOD, D_MODEL), lambda t, b: (jnp.where(t == ctx_tile, ctx_row, b), 0, 0)),
            full(g1), full(lw["w_all"]), full(lw["w_qb"]), full(lw["w_kvb"]), full(lw["gvec"]),
            pl.BlockSpec((TM, tables.shape[1]), lambda t, b: (t, 0)),
        ],
        out_specs=[o[0] for o in outs],
        out_shape=[o[1] for o in outs],
        compiler_params=_cparams(("arbitrary", "arbitrary")),
        name="inproj",
    )(x_all, mods, g1, lw["w_all"], lw["w_qb"], lw["w_kvb"], lw["gvec"], tables)


def _softmax_pv(q, kt, v, bias=None):
    s = _dot(q, kt)
    if bias is not None:
        s = s + bias
    p = jnp.exp2(s - jnp.max(s, axis=-1, keepdims=True)).astype(BF16)
    o = _dot(p, v)
    return o[:, :HEAD_DIM] * (1.0 / o[:, HEAD_DIM:HEAD_DIM + 1])


def _attn_kernel(seq, lambda_init, qna_ref, kna_ref, vna_ref, qdf_ref, kdf_ref, vdf_ref,
                 qgq_ref, kgq_ref, vgq_ref, qml_ref, kml_ref, vml_ref, bias_ref, lam_ref, sub_ref, o_ref):
    qi = pl.program_id(1)
    t_all = vna_ref.shape[2]
    n_lat_tiles = seq // TM
    lp = lam_ref[...]
    lam = (jnp.exp(jnp.sum(lp[0:1] * lp[1:2], axis=-1, keepdims=True))
           - jnp.exp(jnp.sum(lp[2:3] * lp[3:4], axis=-1, keepdims=True)) + lambda_init)
    lane = lax.broadcasted_iota(jnp.int32, (TM, LANES), 1)

    def q_lanes(q_ref, tile, first, size):
        shift = size.bit_length() - 1
        qf = q_ref[0, :, tile * LANES:(tile + 1) * LANES].astype(F32)
        return jnp.where(lax.shift_right_logical(lane, shift) == (first >> shift), qf, 0.0).astype(BF16)

    def tile_rows(tile):
        return slice(tile * LANES, (tile + 1) * LANES)

    def paired_heads(q_ref, kt_ref, v_ref, k_lo):
        return [_softmax_pv(q_lanes(q_ref, h // 2, HEAD_DIM * (h % 2), HEAD_DIM),
                            kt_ref[0, tile_rows(h // 2), k_lo:t_all], v_ref[0, h, k_lo:t_all, :])
                for h in range(N_HEADS)]

    def grouped_heads(k_lo):
        outs = []
        for h in range(N_HEADS):
            g = h // (N_HEADS // GQA_KV_HEADS)
            outs.append(_softmax_pv(q_lanes(qgq_ref, h % 2, HEAD_DIM * g, HEAD_DIM),
                                    kgq_ref[0, :, k_lo:t_all], vgq_ref[0, g, k_lo:t_all, :]))
        return outs

    def latent_heads(k_lo):
        return [_softmax_pv(qml_ref[0, :, tile_rows(h)], kml_ref[0, tile_rows(h), k_lo:t_all],
                            vml_ref[0, h, k_lo:t_all, :]) for h in range(N_HEADS)]

    def diff_heads(k_lo):
        outs = []
        for h in range(N_HEADS):
            kt = kdf_ref[0, tile_rows(h // 2), k_lo:t_all]
            v = vdf_ref[0, h, k_lo:t_all, :]
            first = HEAD_DIM * (h % 2)
            o = (_softmax_pv(q_lanes(qdf_ref, h // 2, first, DIFF_DH), kt, v)
                 - lam * _softmax_pv(q_lanes(qdf_ref, h // 2, first + DIFF_DH, DIFF_DH), kt, v))
            ms = jnp.mean(o * o, axis=-1, keepdims=True)
            outs.append(o * lax.rsqrt(ms + EPS) * sub_ref[...] * (1.0 - lambda_init))
        return outs

    def band_heads():
        ws = pl.multiple_of(jnp.clip(NA_ROWS_PER_TILE * qi - NA_WIN_R // 2, 0,
                                     seq // GRID_W - NA_WIN_ROWS) * GRID_W, TM)
        n_win = NA_WIN_ROWS * GRID_W
        outs = []
        for h in range(N_HEADS):
            rows = tile_rows(h // 2)
            kt = jnp.concatenate([kna_ref[0, rows, pl.ds(ws, n_win)], kna_ref[0, rows, seq:t_all]], axis=1)
            v = jnp.concatenate([vna_ref[0, h, pl.ds(ws, n_win), :], vna_ref[0, h, seq:t_all, :]], axis=0)
            outs.append(_softmax_pv(q_lanes(qna_ref, h // 2, HEAD_DIM * (h % 2), HEAD_DIM), kt, v, bias_ref[0, h]))
        return outs

    def run(k_lo, latent):
        outs = band_heads() if latent else paired_heads(qna_ref, kna_ref, vna_ref, k_lo)
        outs += diff_heads(k_lo)
        outs += grouped_heads(k_lo)
        outs += latent_heads(k_lo)
        o_ref[0] = jnp.concatenate(outs, axis=1).astype(o_ref.dtype)

    @pl.when(qi < n_lat_tiles)
    def _():
        run(0, True)

    @pl.when(qi >= n_lat_tiles)
    def _():
        run(seq, False)


def _attention(qkv, bias, lam_p, sub_g, seq, n_q_tiles, lambda_init):
    bsz, t_all, _ = qkv[0].shape
    n_lat_tiles = seq // TM
    n_cls = bias.shape[0]
    in_specs = []
    for wq, wk, n_kv in BRANCH_LAYOUT:
        in_specs += [
            pl.BlockSpec((1, TM, wq), lambda b, i: (b, i, 0)),
            pl.BlockSpec((1, wk, t_all), lambda b, i: (b, 0, 0)),
            pl.BlockSpec((1, n_kv, t_all, LANES), lambda b, i: (b, 0, 0, 0)),
        ]
    in_specs += [
        pl.BlockSpec((1,) + bias.shape[1:],
                     lambda b, i: (jnp.where(i == 0, 0, jnp.where(i >= n_lat_tiles - 1, n_cls - 1, 1)), 0, 0, 0)),
        pl.BlockSpec(lam_p.shape, lambda b, i: (0, 0)),
        pl.BlockSpec(sub_g.shape, lambda b, i: (0, 0)),
    ]
    width = 4 * N_HEADS * HEAD_DIM
    return pl.pallas_call(
        functools.partial(_attn_kernel, seq, lambda_init),
        grid=(bsz, n_q_tiles),
        in_specs=in_specs,
        out_specs=pl.BlockSpec((1, TM, width), lambda b, i: (b, i, 0)),
        out_shape=jax.ShapeDtypeStruct((bsz, t_all, width), BF16),
        compiler_params=_cparams(("arbitrary", "arbitrary")),
        name="attn",
    )(*qkv, bias, lam_p, sub_g)


def _merge_kernel(x_ref, mod_ref, g1_ref, y_ref, wg_ref, wb_ref, wo_ref, o_ref):
    x = x_ref[0]
    mod = mod_ref[0]
    h = (_row_rmsnorm(x, g1_ref[...]) * (1.0 + mod[1:2]) + mod[0:1]).astype(BF16)
    acc = None
    width = wb_ref.shape[1]
    for i in range(wb_ref.shape[0]):
        term = _sigmoid(_dot(h, wg_ref[i])) * _dot(y_ref[0, :, i * width:(i + 1) * width], wb_ref[i])
        acc = term if acc is None else acc + term
    o_ref[0] = x + mod[2:3] * _dot(acc.astype(BF16), wo_ref[...])


def _merge(x_all, mods, g1, y, lw, n_tiles):
    bsz, t_all, _ = x_all.shape
    ctx_tile = t_all // TM - 1
    ctx_row = mods.shape[0] - HALO

    def full(a):
        nd = a.ndim
        return pl.BlockSpec(a.shape, lambda b, t: (0,) * nd)

    tok = lambda w: pl.BlockSpec((1, TM, w), lambda b, t: (b, t, 0))
    return pl.pallas_call(
        _merge_kernel,
        grid=(bsz, n_tiles),
        in_specs=[
            tok(D_MODEL),
            pl.BlockSpec((1, N_MOD, D_MODEL), lambda b, t: (jnp.where(t == ctx_tile, ctx_row, b), 0, 0)),
            full(g1), tok(y.shape[2]),
            full(lw["w_gate"]), full(lw["w_branch"]), full(lw["w_out"]),
        ],
        out_specs=tok(D_MODEL),
        out_shape=jax.ShapeDtypeStruct(x_all.shape, F32),
        input_output_aliases={0: 0},
        compiler_params=_cparams(("arbitrary", "arbitrary")),
        name="merge",
    )(x_all, mods, g1, y, lw["w_gate"], lw["w_branch"], lw["w_out"])


def _ffn_kernel(seq, t_all, x_ref, xp_ref, xn_ref, mod_ref, g2_ref, wu_ref, cw_ref, wd_ref, o_ref, act_scr):
    t = pl.program_id(1)
    x = x_ref[0]
    mod = mod_ref[0]
    x_ext = jnp.concatenate([x, xp_ref[0], xn_ref[0]], axis=0)
    hx = (_row_rmsnorm(x_ext, g2_ref[...]) * (1.0 + mod[4:5]) + mod[3:4]).astype(BF16)

    row = lax.broadcasted_iota(jnp.int32, (TM, FF_CHUNK), 0)
    g_first = t * TM
    has_prev = jnp.logical_and(g_first != 0, g_first != seq)
    has_next = jnp.logical_and(g_first + TM != seq, g_first + TM != t_all)
    first_row = row == 0
    last_row = row == TM - 1

    def conv(lo):
        u_ext = _dot(hx, wu_ref[:, lo:lo + FF_CHUNK])
        c = cw_ref[:, lo:lo + FF_CHUNK]
        u = u_ext[0:TM]
        prev_row = jnp.where(has_prev, u_ext[TM + HALO - 1:TM + HALO], 0.0)
        next_row = jnp.where(has_next, u_ext[TM + HALO:TM + HALO + 1], 0.0)
        up = jnp.where(first_row, prev_row, pltpu.roll(u, 1, 0))
        dn = jnp.where(last_row, next_row, pltpu.roll(u, TM - 1, 0))
        return c[3:4] + up * c[0:1] + u * c[1:2] + dn * c[2:3]

    for j in range(D_FF // FF_CHUNK):
        lo = j * FF_CHUNK
        gate = conv(lo)
        val = conv(D_FF + lo)
        act_scr[:, lo:lo + FF_CHUNK] = (gate * _sigmoid(gate) * val).astype(BF16)
    o_ref[0] = x + mod[5:6] * _dot(act_scr[...], wd_ref[...])


def _ffn(x_all, mods, g2, lw, seq, n_tiles, out_rows):
    bsz, t_all, _ = x_all.shape
    nt_all = t_all // TM
    ctx_tile = nt_all - 1
    ctx_row = mods.shape[0] - HALO
    blocks_per_tile = TM // HALO
    last_block = t_all // HALO - 1

    def full(a):
        nd = a.ndim
        return pl.BlockSpec(a.shape, lambda b, t: (0,) * nd)

    return pl.pallas_call(
        functools.partial(_ffn_kernel, seq, t_all),
        grid=(bsz, n_tiles),
        in_specs=[
            pl.BlockSpec((1, TM, D_MODEL), lambda b, t: (b, t, 0)),
            pl.BlockSpec((1, HALO, D_MODEL), lambda b, t: (b, jnp.maximum(t * blocks_per_tile - 1, 0), 0)),
            pl.BlockSpec((1, HALO, D_MODEL), lambda b, t: (b, jnp.minimum((t + 1) * blocks_per_tile, last_block), 0)),
            pl.BlockSpec((1, N_MOD, D_MODEL), lambda b, t: (jnp.where(t == ctx_tile, ctx_row, b), 0, 0)),
            full(g2), full(lw["w_up"]), full(lw["conv"]), full(lw["w_down"]),
        ],
        out_specs=pl.BlockSpec((1, TM, D_MODEL), lambda b, t: (b, t, 0)),
        out_shape=jax.ShapeDtypeStruct((bsz, out_rows, D_MODEL), F32),
        scratch_shapes=[pltpu.VMEM((TM, D_FF), BF16)],
        compiler_params=_cparams(("arbitrary", "arbitrary")),
        name="ffn",
    )(x_all, x_all, x_all, mods, g2, lw["w_up"], lw["conv"], lw["w_down"])


def _pad_heads(w, n_heads, d):
    lead = w.shape[:-1]
    w = w.reshape(lead + (n_heads, d))
    w = jnp.pad(w, [(0, 0)] * len(lead) + [(0, 0), (0, LANES - d)])
    return w.reshape(lead + (n_heads * LANES,))


def _pad_lanes(v, width):
    return jnp.pad(v, (0, width - v.shape[0]))


def _swap_halves(w, lane0, used, half):
    lead = w.shape[:-1]
    tiles = w.reshape(lead + (w.shape[-1] // LANES, LANES))
    grp = tiles[..., lane0:lane0 + used].reshape(lead + (tiles.shape[-2], used // (2 * half), 2, half))
    grp = grp[..., ::-1, :].reshape(lead + (tiles.shape[-2], used))
    out = jnp.pad(grp, [(0, 0)] * (len(lead) + 1) + [(lane0, LANES - lane0 - used)])
    return out.reshape(w.shape)


def _rope_table(n_lat, n_ctx, rot_dim, lane0, lanes_used):
    t = jnp.arange(n_lat)
    row = (t // GRID_W).astype(F32)
    col = (t % GRID_W).astype(F32)
    n_axis = rot_dim // 4
    inv_freq = ROPE_THETA ** (-jnp.arange(n_axis, dtype=F32) / n_axis)
    ang = jnp.concatenate([row[:, None] * inv_freq, col[:, None] * inv_freq], axis=-1)
    cos_g = jnp.concatenate([jnp.cos(ang), jnp.cos(ang)], axis=-1)
    sin_g = jnp.concatenate([-jnp.sin(ang), jnp.sin(ang)], axis=-1)
    reps = lanes_used // rot_dim
    cos = jnp.ones((n_lat, LANES), F32).at[:, lane0:lane0 + lanes_used].set(jnp.tile(cos_g, (1, reps)))
    sin = jnp.zeros((n_lat, LANES), F32).at[:, lane0:lane0 + lanes_used].set(jnp.tile(sin_g, (1, reps)))
    cos = jnp.concatenate([cos, jnp.ones((n_ctx, LANES), F32)], axis=0)
    sin = jnp.concatenate([sin, jnp.zeros((n_ctx, LANES), F32)], axis=0)
    return cos, sin


def _rope_tables(n_lat, n_ctx):
    cd, sd = _rope_table(n_lat, n_ctx, DIFF_DH, 0, LANES)
    cg, sg = _rope_table(n_lat, n_ctx, HEAD_DIM, 0, LANES)
    cm, sm = _rope_table(n_lat, n_ctx, MLA_ROPE, MLA_NOPE, MLA_ROPE)
    return jnp.concatenate([cd, sd, cg, sg, cm, sm], axis=1)


def _na_bias_tables(rel_bias, rows, n_ctx):
    cq = jnp.arange(GRID_W)
    col_start = jnp.clip(cq - NA_WIN_C // 2, 0, GRID_W - NA_WIN_C)
    ck = jnp.arange(GRID_W)
    col_in = (ck[None, :] >= col_start[:, None]) & (ck[None, :] < col_start[:, None] + NA_WIN_C)
    col_idx = jnp.clip(ck[None, :] - cq[:, None], -(NA_WIN_C - 1), NA_WIN_C - 1) + NA_WIN_C - 1
    col_sel = jax.nn.one_hot(col_idx.reshape(-1), 2 * NA_WIN_C - 1, dtype=F32)
    tables = []
    for r0 in (0, NA_ROWS_PER_TILE, rows - NA_ROWS_PER_TILE):
        ws = min(max(r0 - NA_WIN_R // 2, 0), rows - NA_WIN_ROWS)
        r = r0 + jnp.arange(NA_ROWS_PER_TILE)
        kr = ws + jnp.arange(NA_WIN_ROWS)
        rs = jnp.clip(r - NA_WIN_R // 2, 0, rows - NA_WIN_R)
        in_band = (kr[None, :] >= rs[:, None]) & (kr[None, :] < rs[:, None] + NA_WIN_R)
        row_idx = jnp.clip(kr[None, :] - r[:, None] + NA_WIN_R - 1, 0, 2 * NA_WIN_R - 2)
        row_sel = jax.nn.one_hot(row_idx.reshape(-1), 2 * NA_WIN_R - 1, dtype=F32)
        b = jnp.einsum("pa,lhac,qc->lhpq", row_sel, rel_bias, col_sel, precision=lax.Precision.HIGHEST)
        b = b.reshape(rel_bias.shape[:2] + (NA_ROWS_PER_TILE, NA_WIN_ROWS, GRID_W, GRID_W)).transpose(0, 1, 2, 4, 3, 5)
        ok = in_band[:, None, :, None] & col_in[None, :, None, :]
        b = jnp.where(ok[None, None], b * LOG2E, NEG_INF)
        b = b.reshape(rel_bias.shape[:2] + (TM, NA_WIN_ROWS * GRID_W))
        tables.append(jnp.concatenate([b, jnp.zeros(rel_bias.shape[:2] + (TM, n_ctx), F32)], axis=-1))
    return jnp.stack(tables, axis=1)


def _layer_weights(i, p):
    w_in = p["w_in"][i]
    offs = [0]
    for n in IN_SIZES:
        offs.append(offs[-1] + n)
    seg = [w_in[:, offs[j]:offs[j + 1]] for j in range(len(IN_SIZES))]
    kr_cols = jnp.pad(seg[11], ((0, 0), (MLA_NOPE, LANES - MLA_NOPE - MLA_ROPE)))
    rot_d = functools.partial(_swap_halves, lane0=0, used=LANES, half=DIFF_DH // 2)
    rot_g = functools.partial(_swap_halves, lane0=0, used=LANES, half=HEAD_DIM // 2)
    rot_m = functools.partial(_swap_halves, lane0=MLA_NOPE, used=MLA_ROPE, half=MLA_ROPE // 2)
    df_q, df_k, gq_k = seg[3], seg[4], seg[7]
    gq_q = seg[6].reshape(D_MODEL, 2, 2, HEAD_DIM).transpose(0, 2, 1, 3).reshape(D_MODEL, N_HEADS * HEAD_DIM)
    w_all = jnp.concatenate(
        [seg[0], seg[1], _pad_heads(seg[2], 4, 64),
         df_q, df_k, _pad_heads(seg[5], 4, 64), gq_q, gq_k, _pad_heads(seg[8], 2, 64),
         jnp.pad(seg[9], ((0, 0), (0, 256 - MLA_Q_RANK))), seg[10], kr_cols,
         rot_d(df_q), rot_d(df_k), rot_g(gq_q), rot_g(gq_k), rot_m(kr_cols)], axis=1).astype(BF16)

    w_qb = jnp.pad(_pad_heads(p["w_mla_qb"][i], N_HEADS, MLA_QK), ((0, 256 - MLA_Q_RANK), (0, 0)))
    w_qb = jnp.concatenate([w_qb, rot_m(w_qb)], axis=1).astype(BF16)
    kvb = p["w_mla_kvb"][i].reshape(MLA_KV_RANK, N_HEADS, MLA_NOPE + MLA_V)
    w_kvb = jnp.concatenate([_pad_heads(kvb[:, :, :MLA_NOPE].reshape(MLA_KV_RANK, -1), N_HEADS, MLA_NOPE),
                             _pad_heads(kvb[:, :, MLA_NOPE:].reshape(MLA_KV_RANK, -1), N_HEADS, MLA_V)],
                            axis=1).astype(BF16)

    def head_gain(g, n_heads, scale=1.0):
        return _pad_lanes(jnp.tile(_pad_lanes(g * scale, LANES), n_heads), 512)

    def dense_gain(g, reps, scale=1.0):
        return _pad_lanes(jnp.tile(g * scale, reps), 512)

    dg = p["diff_qk_g"][i]
    mq = p["mla_qk_g"][i]
    rows = [
        dense_gain(p["na_qk_g"][i, 0], 4, HEAD_DIM ** -0.5 * LOG2E), dense_gain(p["na_qk_g"][i, 1], 4),
        dense_gain(dg[0], 8, DIFF_DH ** -0.5 * LOG2E), dense_gain(dg[1], 8),
        dense_gain(p["gqa_qk_g"][i, 0], 4, HEAD_DIM ** -0.5 * LOG2E), dense_gain(p["gqa_qk_g"][i, 1], 2),
        head_gain(mq[0], 4, MLA_QK ** -0.5 * LOG2E), head_gain(_pad_lanes(mq[1, :MLA_NOPE], MLA_QK), 4),
        _pad_lanes(p["mla_qa_g"][i], 512), _pad_lanes(p["mla_kva_g"][i], 512),
        _pad_lanes(jnp.pad(mq[1, MLA_NOPE:], (MLA_NOPE, 0)), 512),
    ]
    rows += [rot_d(rows[2]), rot_d(rows[3]), rot_g(rows[4]), rot_g(rows[5]), rot_m(rows[6]), rot_m(rows[10])]
    gvec = jnp.stack(rows + [jnp.zeros((512,), F32)] * (24 - len(rows)))
    conv = jnp.concatenate([p["conv_w"][i], p["conv_b"][i][None], jnp.zeros((4, 2 * D_FF), F32)], axis=0)
    return dict(
        w_all=w_all, w_qb=w_qb, w_kvb=w_kvb, gvec=gvec,
        w_gate=p["w_gate"][i].astype(BF16), w_branch=p["w_branch"][i].astype(BF16), w_out=p["w_out"][i].astype(BF16),
        w_up=p["w_up"][i].astype(BF16), conv=conv, w_down=p["w_down"][i].astype(BF16),
        sub_g=p["diff_subln_g"][i][None],
    )


def kernel(x, c, ctx, c_ctx, w_mod, b_mod, norm1_g, norm2_g, w_in, na_qk_g, na_rel_bias, diff_qk_g, diff_lambda, diff_subln_g, gqa_qk_g, mla_qa_g, mla_kva_g, w_mla_qb, w_mla_kvb, mla_qk_g, w_gate, w_branch, w_out, w_up, conv_w, conv_b, w_down):
    p = dict(w_in=w_in, na_qk_g=na_qk_g, diff_qk_g=diff_qk_g, gqa_qk_g=gqa_qk_g, mla_qa_g=mla_qa_g,
             mla_kva_g=mla_kva_g, w_mla_qb=w_mla_qb, w_mla_kvb=w_mla_kvb, mla_qk_g=mla_qk_g, w_gate=w_gate,
             w_branch=w_branch, w_out=w_out, w_up=w_up, conv_w=conv_w, conv_b=conv_b, w_down=w_down,
             diff_subln_g=diff_subln_g)
    bsz, seq, d = x.shape
    n_ctx = ctx.shape[1]
    depth = w_mod.shape[0]
    assert d == D_MODEL and seq % TM == 0 and n_ctx == TM and seq // GRID_W >= NA_WIN_ROWS
    t_all = seq + n_ctx
    n_lat_tiles = seq // TM

    x_all = jnp.concatenate([x, ctx], axis=1)
    cc = jnp.concatenate([c, c_ctx[None], jnp.zeros((HALO - 1, d), F32)], axis=0)
    mods_all = _mod_all(cc, w_mod.astype(BF16), b_mod[:, None, :])
    mods_all = mods_all.reshape(depth, bsz + HALO, N_MOD, d)
    tables = _rope_tables(seq, n_ctx)
    na_bias = _na_bias_tables(na_rel_bias, seq // GRID_W, n_ctx)

    for i in range(depth):
        last = i == depth - 1
        n_tiles = n_lat_tiles if last else n_lat_tiles + 1
        lambda_init = 0.8 - 0.6 * math.exp(-0.3 * i)
        lw = _layer_weights(i, p)
        mods = mods_all[i]
        g1 = norm1_g[i][None]
        g2 = norm2_g[i][None]
        qkv = _inproj(x_all, mods, g1, lw, tables)
        y = _attention(qkv, na_bias[i], diff_lambda[i], lw["sub_g"], seq, n_tiles, lambda_init)
        x_all = _merge(x_all, mods, g1, y, lw, n_tiles)
        x_all = _ffn(x_all, mods, g2, lw, seq, n_tiles, seq if last else t_all)
    return x_all
```

```python
import functools
import math

import jax
import jax.numpy as jnp
from jax import lax
from jax.experimental import pallas as pl
from jax.experimental.pallas import tpu as pltpu

D_MODEL = 1024
GRID_W = 64
HEAD_DIM = 64
EPS = 1e-6
ROPE_THETA = 10000.0
NEG_INF = -1e30
NA_WIN_R = 8
NA_WIN_C = 16
DIFF_DH = 32
MLA_Q_RANK = 192
MLA_KV_RANK = 128
MLA_NOPE = 64
MLA_ROPE = 32
MLA_V = 64
MLA_QK = MLA_NOPE + MLA_ROPE
N_HEADS = 4
GQA_KV_HEADS = 2
N_MOD = 6
D_FF = 2816
IN_SIZES = (256, 256, 256, 256, 256, 256, 256, 128, 128, MLA_Q_RANK, MLA_KV_RANK, MLA_ROPE)

LANES = 128
MXU_DIM = 256
TM = 256
FF_CHUNK = 256
HALO = 8
LOG2E = math.log2(math.e)
BRANCH_LAYOUT = ((256, 256, 4), (256, 256, 4), (256, 128, 2), (512, 512, 4))
NA_ROWS_PER_TILE = TM // GRID_W
NA_WIN_ROWS = NA_WIN_R + NA_ROWS_PER_TILE
VMEM_LIMIT = 56 * 1024 * 1024

BF16 = jnp.bfloat16
F32 = jnp.float32


def _cparams(sem):
    return pltpu.CompilerParams(dimension_semantics=sem, vmem_limit_bytes=VMEM_LIMIT)


def _dot(a, b):
    return jnp.dot(a, b, preferred_element_type=F32)


def _sigmoid(z):
    return 1.0 / (1.0 + jnp.exp(-z))


def _row_rmsnorm(x, g):
    ms = jnp.mean(x * x, axis=-1, keepdims=True)
    return x * lax.rsqrt(ms + EPS) * g


def _mod_kernel(c_ref, w_ref, b_ref, o_ref):
    c = c_ref[...]
    act = (c * _sigmoid(c)).astype(BF16)
    o_ref[0] = _dot(act, w_ref[0]) + b_ref[0]


def _mod_all(cc, w_mod, b_mod):
    n_layers, _, n_out = w_mod.shape
    rows = cc.shape[0]
    tn = 1024
    return pl.pallas_call(
        _mod_kernel,
        grid=(n_layers, n_out // tn),
        in_specs=[
            pl.BlockSpec((rows, D_MODEL), lambda l, j: (0, 0)),
            pl.BlockSpec((1, D_MODEL, tn), lambda l, j: (l, 0, j)),
            pl.BlockSpec((1, 1, tn), lambda l, j: (l, 0, j)),
        ],
        out_specs=pl.BlockSpec((1, rows, tn), lambda l, j: (l, 0, j)),
        out_shape=jax.ShapeDtypeStruct((n_layers, rows, n_out), F32),
        compiler_params=_cparams(("arbitrary", "arbitrary")),
        name="mod_all",
    )(cc, w_mod, b_mod)


def _head_rmsnorm(y, groups, y_rot=None, g=None, g_rot=None, cos=None, sin_signed=None):
    lane = lax.broadcasted_iota(jnp.int32, (1, LANES), 1)
    outs = []
    for c in range(y.shape[1] // LANES):
        yc = y[:, c * LANES:(c + 1) * LANES]
        sq = yc * yc
        if len(groups) == 1:
            ms = jnp.sum(sq, axis=-1, keepdims=True) * (1.0 / groups[0][1])
        else:
            ms = 0.0
            for first, size in groups:
                shift = size.bit_length() - 1
                member = lax.shift_right_logical(lane, shift) == (first >> shift)
                ms = jnp.where(member, jnp.sum(jnp.where(member, sq, 0.0), axis=-1, keepdims=True) * (1.0 / size), ms)
        r = lax.rsqrt(ms + EPS)
        outs.append(yc * r if y_rot is None else
                    (yc * r * g[:, c * LANES:(c + 1) * LANES]) * cos
                    + (y_rot[:, c * LANES:(c + 1) * LANES] * r * g_rot[:, c * LANES:(c + 1) * LANES]) * sin_signed)
    return outs[0] if len(outs) == 1 else jnp.concatenate(outs, axis=1)


def _store_q(ref, y):
    ref[0] = y.astype(ref.dtype)


def _store_kt(ref, y):
    ref[0] = y.T.astype(ref.dtype)


def _store_v(ref, y):
    lane = lax.broadcasted_iota(jnp.int32, (1, 2 * LANES), 1)
    ones_col = jnp.where(jnp.bitwise_and(lane, LANES - 1) == HEAD_DIM, 1.0, 0.0)
    for j in range(y.shape[1] // (2 * LANES)):
        ref[0, j] = (y[:, 2 * j * LANES:2 * (j + 1) * LANES] + ones_col).astype(ref.dtype)


def _inproj_kernel(x_ref, mod_ref, g1_ref, w_ref, wqb_ref, wkvb_ref, gv_ref, tab_ref,
                   qna_ref, kna_ref, vna_ref, qdf_ref, kdf_ref, vdf_ref,
                   qgq_ref, kgq_ref, vgq_ref, qml_ref, kml_ref, vml_ref):
    x = x_ref[0]
    mod = mod_ref[0]
    h = (_row_rmsnorm(x, g1_ref[...]) * (1.0 + mod[1:2]) + mod[0:1]).astype(BF16)

    heads64 = ((0, 64), (64, 64))
    maps32 = ((0, 32), (32, 32), (64, 32), (96, 32))
    nope_only = ((0, MLA_NOPE),)
    nope_rope = ((0, MLA_NOPE), (MLA_NOPE, MLA_ROPE))
    rope_only = ((MLA_NOPE, MLA_ROPE),)
    tab = tab_ref[...]
    cos_d, sin_d = tab[:, 0:128], tab[:, 128:256]
    cos_g, sin_g = tab[:, 256:384], tab[:, 384:512]
    cos_m, sin_m = tab[:, 512:640], tab[:, 640:768]

    def proj(lo, width):
        return _dot(h, w_ref[:, lo:lo + width])

    def gain(row, width):
        return gv_ref[row:row + 1, :width]

    def normed(y, groups, row):
        return _head_rmsnorm(y, groups) * gain(row, y.shape[1])

    def roped(y, y_rot, groups, row, row_rot, cos, sin):
        w = y.shape[1]
        return _head_rmsnorm(y, groups, y_rot=y_rot, g=gain(row, w), g_rot=gain(row_rot, w), cos=cos, sin_signed=sin)

    _store_q(qna_ref, normed(proj(0, 256), heads64, 0))
    _store_kt(kna_ref, normed(proj(256, 256), heads64, 1))
    _store_v(vna_ref, proj(512, 512))
    _store_q(qdf_ref, roped(proj(1024, 256), proj(3328, 256), maps32, 2, 11, cos_d, sin_d))
    _store_kt(kdf_ref, roped(proj(1280, 256), proj(3584, 256), maps32, 3, 12, cos_d, sin_d))
    _store_v(vdf_ref, proj(1536, 512))
    _store_q(qgq_ref, roped(proj(2048, 256), proj(3840, 256), heads64, 4, 13, cos_g, sin_g))
    k_gq = proj(2304, 256)
    _store_kt(kgq_ref, roped(k_gq[:, :LANES], k_gq[:, LANES:], heads64, 5, 14, cos_g, sin_g))
    _store_v(vgq_ref, proj(2560, 256))
    qa = proj(2816, 256)
    ms_qa = jnp.sum(qa * qa, axis=-1, keepdims=True) * (1.0 / MLA_Q_RANK)
    qa_n = (qa * lax.rsqrt(ms_qa + EPS) * gain(8, 256)).astype(BF16)
    q_ml = _dot(qa_n, wqb_ref[:, :512])
    q_ml_rot = _dot(qa_n, wqb_ref[:, 512:])
    _store_q(qml_ref, roped(q_ml, q_ml_rot, nope_rope, 6, 15, cos_m, sin_m))
    kva_kr = proj(3072, 256)
    kva_n = _row_rmsnorm(kva_kr[:, :LANES], gain(9, 128)).astype(BF16)
    kv = _dot(kva_n, wkvb_ref[...])
    k_nope = normed(kv[:, :512], nope_only, 7)
    kr_rot = proj(4096, 256)[:, :LANES]
    kr_r = roped(kva_kr[:, LANES:], kr_rot, rope_only, 10, 16, cos_m, sin_m)
    _store_kt(kml_ref, k_nope + jnp.concatenate([kr_r] * N_HEADS, axis=1))
    _store_v(vml_ref, kv[:, 512:])


def _inproj(x_all, mods, g1, lw, tables):
    bsz, t_all, _ = x_all.shape
    nt = t_all // TM
    ctx_tile = nt - 1
    ctx_row = mods.shape[0] - HALO

    def full(a):
        nd = a.ndim
        return pl.BlockSpec(a.shape, lambda t, b: (0,) * nd)

    def q_spec(width):
        return (pl.BlockSpec((1, TM, width), lambda t, b: (b, t, 0)),
                jax.ShapeDtypeStruct((bsz, t_all, width), BF16))

    def kt_spec(width):
        return (pl.BlockSpec((1, width, TM), lambda t, b: (b, 0, t)),
                jax.ShapeDtypeStruct((bsz, width, t_all), BF16))

    def v_spec(nh):
        return (pl.BlockSpec((1, nh // 2, TM, 2 * LANES), lambda t, b: (b, 0, t, 0)),
                jax.ShapeDtypeStruct((bsz, nh // 2, t_all, 2 * LANES), BF16))

    outs = []
    for wq, wk, n_kv in BRANCH_LAYOUT:
        outs += [q_spec(wq), kt_spec(wk), v_spec(n_kv)]
    return pl.pallas_call(
        _inproj_kernel,
        grid=(nt, bsz),
        in_specs=[
            pl.BlockSpec((1, TM, D_MODEL), lambda t, b: (b, t, 0)),
            pl.BlockSpec((1, N_MOD, D_MODEL), lambda t, b: (jnp.where(t == ctx_tile, ctx_row, b), 0, 0)),
            full(g1), full(lw["w_all"]), full(lw["w_qb"]), full(lw["w_kvb"]), full(lw["gvec"]),
            pl.BlockSpec((TM, tables.shape[1]), lambda t, b: (t, 0)),
        ],
        out_specs=[o[0] for o in outs],
        out_shape=[o[1] for o in outs],
        compiler_params=_cparams(("arbitrary", "arbitrary")),
        name="inproj",
    )(x_all, mods, g1, lw["w_all"], lw["w_qb"], lw["w_kvb"], lw["gvec"], tables)


def _softmax_pv(q, kt, v, half, bias=None):
    s = _dot(q, kt)
    if bias is not None:
        s = s + bias
    p = jnp.exp2(s - jnp.max(s, axis=-1, keepdims=True)).astype(BF16)
    o = _dot(p, v)[:, half * LANES:(half + 1) * LANES]
    return o[:, :HEAD_DIM] * (1.0 / o[:, HEAD_DIM:HEAD_DIM + 1])


def _attn_kernel(seq, lambda_init, qna_ref, kna_ref, vna_ref, qdf_ref, kdf_ref, vdf_ref,
                 qgq_ref, kgq_ref, vgq_ref, qml_ref, kml_ref, vml_ref, bias_ref, lam_ref, sub_ref, o_ref):
    qi = pl.program_id(1)
    t_all = vna_ref.shape[2]
    n_lat_tiles = seq // TM
    lp = lam_ref[...]
    lam = (jnp.exp(jnp.sum(lp[0:1] * lp[1:2], axis=-1, keepdims=True))
           - jnp.exp(jnp.sum(lp[2:3] * lp[3:4], axis=-1, keepdims=True)) + lambda_init)
    lane = lax.broadcasted_iota(jnp.int32, (TM, LANES), 1)

    def q_lanes(q_ref, tile, first, size):
        shift = size.bit_length() - 1
        qf = q_ref[0, :, tile * LANES:(tile + 1) * LANES].astype(F32)
        return jnp.where(lax.shift_right_logical(lane, shift) == (first >> shift), qf, 0.0).astype(BF16)

    def tile_rows(tile):
        return slice(tile * LANES, (tile + 1) * LANES)

    def paired_heads(q_ref, kt_ref, v_ref, k_lo):
        return [_softmax_pv(q_lanes(q_ref, h // 2, HEAD_DIM * (h % 2), HEAD_DIM),
                            kt_ref[0, tile_rows(h // 2), k_lo:t_all], v_ref[0, h // 2, k_lo:t_all, :], h % 2)
                for h in range(N_HEADS)]

    def grouped_heads(k_lo):
        outs = []
        for h in range(N_HEADS):
            g = h // (N_HEADS // GQA_KV_HEADS)
            outs.append(_softmax_pv(q_lanes(qgq_ref, h % 2, HEAD_DIM * g, HEAD_DIM),
                                    kgq_ref[0, :, k_lo:t_all], vgq_ref[0, 0, k_lo:t_all, :], g))
        return outs

    def latent_heads(k_lo):
        return [_softmax_pv(qml_ref[0, :, tile_rows(h)], kml_ref[0, tile_rows(h), k_lo:t_all],
                            vml_ref[0, h // 2, k_lo:t_all, :], h % 2) for h in range(N_HEADS)]

    def diff_heads(k_lo):
        outs = []
        for h in range(N_HEADS):
            kt = kdf_ref[0, tile_rows(h // 2), k_lo:t_all]
            v = vdf_ref[0, h // 2, k_lo:t_all, :]
            first = HEAD_DIM * (h % 2)
            o = (_softmax_pv(q_lanes(qdf_ref, h // 2, first, DIFF_DH), kt, v, h % 2)
                 - lam * _softmax_pv(q_lanes(qdf_ref, h // 2, first + DIFF_DH, DIFF_DH), kt, v, h % 2))
            ms = jnp.mean(o * o, axis=-1, keepdims=True)
            outs.append(o * lax.rsqrt(ms + EPS) * sub_ref[...] * (1.0 - lambda_init))
        return outs

    def band_heads():
        ws = pl.multiple_of(jnp.clip(NA_ROWS_PER_TILE * qi - NA_WIN_R // 2, 0,
                                     seq // GRID_W - NA_WIN_ROWS) * GRID_W, TM)
        n_win = NA_WIN_ROWS * GRID_W
        outs = []
        for h in range(N_HEADS):
            rows = tile_rows(h // 2)
            kt = jnp.concatenate([kna_ref[0, rows, pl.ds(ws, n_win)], kna_ref[0, rows, seq:t_all]], axis=1)
            v = jnp.concatenate([vna_ref[0, h // 2, pl.ds(ws, n_win), :], vna_ref[0, h // 2, seq:t_all, :]], axis=0)
            outs.append(_softmax_pv(q_lanes(qna_ref, h // 2, HEAD_DIM * (h % 2), HEAD_DIM), kt, v, h % 2,
                                    bias_ref[0, h]))
        return outs

    def run(k_lo, latent):
        outs = band_heads() if latent else paired_heads(qna_ref, kna_ref, vna_ref, k_lo)
        outs += diff_heads(k_lo)
        outs += grouped_heads(k_lo)
        outs += latent_heads(k_lo)
        o_ref[0] = jnp.concatenate(outs, axis=1).astype(o_ref.dtype)

    @pl.when(qi < n_lat_tiles)
    def _():
        run(0, True)

    @pl.when(qi >= n_lat_tiles)
    def _():
        run(seq, False)


def _attention(qkv, bias, lam_p, sub_g, seq, n_q_tiles, lambda_init):
    bsz, t_all, _ = qkv[0].shape
    n_lat_tiles = seq // TM
    n_cls = bias.shape[0]
    in_specs = []
    for wq, wk, n_kv in BRANCH_LAYOUT:
        in_specs += [
            pl.BlockSpec((1, TM, wq), lambda b, i: (b, i, 0)),
            pl.BlockSpec((1, wk, t_all), lambda b, i: (b, 0, 0)),
            pl.BlockSpec((1, n_kv // 2, t_all, 2 * LANES), lambda b, i: (b, 0, 0, 0)),
        ]
    in_specs += [
        pl.BlockSpec((1,) + bias.shape[1:],
                     lambda b, i: (jnp.where(i == 0, 0, jnp.where(i >= n_lat_tiles - 1, n_cls - 1, 1)), 0, 0, 0)),
        pl.BlockSpec(lam_p.shape, lambda b, i: (0, 0)),
        pl.BlockSpec(sub_g.shape, lambda b, i: (0, 0)),
    ]
    width = 4 * N_HEADS * HEAD_DIM
    return pl.pallas_call(
        functools.partial(_attn_kernel, seq, lambda_init),
        grid=(bsz, n_q_tiles),
        in_specs=in_specs,
        out_specs=pl.BlockSpec((1, TM, width), lambda b, i: (b, i, 0)),
        out_shape=jax.ShapeDtypeStruct((bsz, t_all, width), BF16),
        compiler_params=_cparams(("arbitrary", "arbitrary")),
        name="attn",
    )(*qkv, bias, lam_p, sub_g)


def _merge_kernel(x_ref, mod_ref, g1_ref, y_ref, wg_ref, wb_ref, wo_ref, o_ref):
    x = x_ref[0]
    mod = mod_ref[0]
    h = (_row_rmsnorm(x, g1_ref[...]) * (1.0 + mod[1:2]) + mod[0:1]).astype(BF16)
    acc = None
    width = wb_ref.shape[1]
    for i in range(wb_ref.shape[0]):
        term = _sigmoid(_dot(h, wg_ref[i])) * _dot(y_ref[0, :, i * width:(i + 1) * width], wb_ref[i])
        acc = term if acc is None else acc + term
    o_ref[0] = x + mod[2:3] * _dot(acc.astype(BF16), wo_ref[...])


def _merge(x_all, mods, g1, y, lw, n_tiles):
    bsz, t_all, _ = x_all.shape
    ctx_tile = t_all // TM - 1
    ctx_row = mods.shape[0] - HALO

    def full(a):
        nd = a.ndim
        return pl.BlockSpec(a.shape, lambda b, t: (0,) * nd)

    tok = lambda w: pl.BlockSpec((1, TM, w), lambda b, t: (b, t, 0))
    return pl.pallas_call(
        _merge_kernel,
        grid=(bsz, n_tiles),
        in_specs=[
            tok(D_MODEL),
            pl.BlockSpec((1, N_MOD, D_MODEL), lambda b, t: (jnp.where(t == ctx_tile, ctx_row, b), 0, 0)),
            full(g1), tok(y.shape[2]),
            full(lw["w_gate"]), full(lw["w_branch"]), full(lw["w_out"]),
        ],
        out_specs=tok(D_MODEL),
        out_shape=jax.ShapeDtypeStruct(x_all.shape, F32),
        input_output_aliases={0: 0},
        compiler_params=_cparams(("arbitrary", "arbitrary")),
        name="merge",
    )(x_all, mods, g1, y, lw["w_gate"], lw["w_branch"], lw["w_out"])


def _ffn_kernel(seq, t_all, x_ref, xp_ref, xn_ref, mod_ref, g2_ref, wu_ref, cw_ref, wd_ref, o_ref, act_scr):
    t = pl.program_id(1)
    x = x_ref[0]
    mod = mod_ref[0]
    x_ext = jnp.concatenate([x, xp_ref[0], xn_ref[0]], axis=0)
    hx = (_row_rmsnorm(x_ext, g2_ref[...]) * (1.0 + mod[4:5]) + mod[3:4]).astype(BF16)

    row = lax.broadcasted_iota(jnp.int32, (TM, FF_CHUNK), 0)
    g_first = t * TM
    has_prev = jnp.logical_and(g_first != 0, g_first != seq)
    has_next = jnp.logical_and(g_first + TM != seq, g_first + TM != t_all)
    first_row = row == 0
    last_row = row == TM - 1

    def conv(lo):
        u_ext = _dot(hx, wu_ref[:, lo:lo + FF_CHUNK])
        c = cw_ref[:, lo:lo + FF_CHUNK]
        u = u_ext[0:TM]
        prev_row = jnp.where(has_prev, u_ext[TM + HALO - 1:TM + HALO], 0.0)
        next_row = jnp.where(has_next, u_ext[TM + HALO:TM + HALO + 1], 0.0)
        up = jnp.where(first_row, prev_row, pltpu.roll(u, 1, 0))
        dn = jnp.where(last_row, next_row, pltpu.roll(u, TM - 1, 0))
        return c[3:4] + up * c[0:1] + u * c[1:2] + dn * c[2:3]

    for j in range(D_FF // FF_CHUNK):
        lo = j * FF_CHUNK
        gate = conv(lo)
        val = conv(D_FF + lo)
        act_scr[:, lo:lo + FF_CHUNK] = (gate * _sigmoid(gate) * val).astype(BF16)
    o_ref[0] = x + mod[5:6] * _dot(act_scr[...], wd_ref[...])


def _ffn(x_all, mods, g2, lw, seq, n_tiles, out_rows):
    bsz, t_all, _ = x_all.shape
    nt_all = t_all // TM
    ctx_tile = nt_all - 1
    ctx_row = mods.shape[0] - HALO
    blocks_per_tile = TM // HALO
    last_block = t_all // HALO - 1

    def full(a):
        nd = a.ndim
        return pl.BlockSpec(a.shape, lambda b, t: (0,) * nd)

    return pl.pallas_call(
        functools.partial(_ffn_kernel, seq, t_all),
        grid=(bsz, n_tiles),
        in_specs=[
            pl.BlockSpec((1, TM, D_MODEL), lambda b, t: (b, t, 0)),
            pl.BlockSpec((1, HALO, D_MODEL), lambda b, t: (b, jnp.maximum(t * blocks_per_tile - 1, 0), 0)),
            pl.BlockSpec((1, HALO, D_MODEL), lambda b, t: (b, jnp.minimum((t + 1) * blocks_per_tile, last_block), 0)),
            pl.BlockSpec((1, N_MOD, D_MODEL), lambda b, t: (jnp.where(t == ctx_tile, ctx_row, b), 0, 0)),
            full(g2), full(lw["w_up"]), full(lw["conv"]), full(lw["w_down"]),
        ],
        out_specs=pl.BlockSpec((1, TM, D_MODEL), lambda b, t: (b, t, 0)),
        out_shape=jax.ShapeDtypeStruct((bsz, out_rows, D_MODEL), F32),
        scratch_shapes=[pltpu.VMEM((TM, D_FF), BF16)],
        compiler_params=_cparams(("arbitrary", "arbitrary")),
        name="ffn",
    )(x_all, x_all, x_all, mods, g2, lw["w_up"], lw["conv"], lw["w_down"])


def _pad_heads(w, n_heads, d):
    lead = w.shape[:-1]
    w = w.reshape(lead + (n_heads, d))
    w = jnp.pad(w, [(0, 0)] * len(lead) + [(0, 0), (0, LANES - d)])
    return w.reshape(lead + (n_heads * LANES,))


def _pad_lanes(v, width):
    return jnp.pad(v, (0, width - v.shape[0]))


def _swap_halves(w, lane0, used, half):
    lead = w.shape[:-1]
    tiles = w.reshape(lead + (w.shape[-1] // LANES, LANES))
    grp = tiles[..., lane0:lane0 + used].reshape(lead + (tiles.shape[-2], used // (2 * half), 2, half))
    grp = grp[..., ::-1, :].reshape(lead + (tiles.shape[-2], used))
    out = jnp.pad(grp, [(0, 0)] * (len(lead) + 1) + [(lane0, LANES - lane0 - used)])
    return out.reshape(w.shape)


def _rope_table(n_lat, n_ctx, rot_dim, lane0, lanes_used):
    t = jnp.arange(n_lat)
    row = (t // GRID_W).astype(F32)
    col = (t % GRID_W).astype(F32)
    n_axis = rot_dim // 4
    inv_freq = ROPE_THETA ** (-jnp.arange(n_axis, dtype=F32) / n_axis)
    ang = jnp.concatenate([row[:, None] * inv_freq, col[:, None] * inv_freq], axis=-1)
    cos_g = jnp.concatenate([jnp.cos(ang), jnp.cos(ang)], axis=-1)
    sin_g = jnp.concatenate([-jnp.sin(ang), jnp.sin(ang)], axis=-1)
    reps = lanes_used // rot_dim
    cos = jnp.ones((n_lat, LANES), F32).at[:, lane0:lane0 + lanes_used].set(jnp.tile(cos_g, (1, reps)))
    sin = jnp.zeros((n_lat, LANES), F32).at[:, lane0:lane0 + lanes_used].set(jnp.tile(sin_g, (1, reps)))
    cos = jnp.concatenate([cos, jnp.ones((n_ctx, LANES), F32)], axis=0)
    sin = jnp.concatenate([sin, jnp.zeros((n_ctx, LANES), F32)], axis=0)
    return cos, sin


def _rope_tables(n_lat, n_ctx):
    cd, sd = _rope_table(n_lat, n_ctx, DIFF_DH, 0, LANES)
    cg, sg = _rope_table(n_lat, n_ctx, HEAD_DIM, 0, LANES)
    cm, sm = _rope_table(n_lat, n_ctx, MLA_ROPE, MLA_NOPE, MLA_ROPE)
    return jnp.concatenate([cd, sd, cg, sg, cm, sm], axis=1)


def _na_bias_tables(rel_bias, rows, n_ctx):
    cq = jnp.arange(GRID_W)
    col_start = jnp.clip(cq - NA_WIN_C // 2, 0, GRID_W - NA_WIN_C)
    ck = jnp.arange(GRID_W)
    col_in = (ck[None, :] >= col_start[:, None]) & (ck[None, :] < col_start[:, None] + NA_WIN_C)
    col_idx = jnp.clip(ck[None, :] - cq[:, None], -(NA_WIN_C - 1), NA_WIN_C - 1) + NA_WIN_C - 1
    col_sel = jax.nn.one_hot(col_idx.reshape(-1), 2 * NA_WIN_C - 1, dtype=F32)
    tables = []
    for r0 in (0, NA_ROWS_PER_TILE, rows - NA_ROWS_PER_TILE):
        ws = min(max(r0 - NA_WIN_R // 2, 0), rows - NA_WIN_ROWS)
        r = r0 + jnp.arange(NA_ROWS_PER_TILE)
        kr = ws + jnp.arange(NA_WIN_ROWS)
        rs = jnp.clip(r - NA_WIN_R // 2, 0, rows - NA_WIN_R)
        in_band = (kr[None, :] >= rs[:, None]) & (kr[None, :] < rs[:, None] + NA_WIN_R)
        row_idx = jnp.clip(kr[None, :] - r[:, None] + NA_WIN_R - 1, 0, 2 * NA_WIN_R - 2)
        row_sel = jax.nn.one_hot(row_idx.reshape(-1), 2 * NA_WIN_R - 1, dtype=F32)
        b = jnp.einsum("pa,lhac,qc->lhpq", row_sel, rel_bias, col_sel, precision=lax.Precision.HIGHEST)
        b = b.reshape(rel_bias.shape[:2] + (NA_ROWS_PER_TILE, NA_WIN_ROWS, GRID_W, GRID_W)).transpose(0, 1, 2, 4, 3, 5)
        ok = in_band[:, None, :, None] & col_in[None, :, None, :]
        b = jnp.where(ok[None, None], b * LOG2E, NEG_INF)
        b = b.reshape(rel_bias.shape[:2] + (TM, NA_WIN_ROWS * GRID_W))
        tables.append(jnp.concatenate([b, jnp.zeros(rel_bias.shape[:2] + (TM, n_ctx), F32)], axis=-1))
    return jnp.stack(tables, axis=1)


def _layer_weights(i, p):
    w_in = p["w_in"][i]
    offs = [0]
    for n in IN_SIZES:
        offs.append(offs[-1] + n)
    seg = [w_in[:, offs[j]:offs[j + 1]] for j in range(len(IN_SIZES))]
    kr_cols = jnp.pad(seg[11], ((0, 0), (MLA_NOPE, LANES - MLA_NOPE - MLA_ROPE)))
    rot_d = functools.partial(_swap_halves, lane0=0, used=LANES, half=DIFF_DH // 2)
    rot_g = functools.partial(_swap_halves, lane0=0, used=LANES, half=HEAD_DIM // 2)
    rot_m = functools.partial(_swap_halves, lane0=MLA_NOPE, used=MLA_ROPE, half=MLA_ROPE // 2)
    df_q, df_k, gq_k = seg[3], seg[4], seg[7]
    gq_q = seg[6].reshape(D_MODEL, 2, 2, HEAD_DIM).transpose(0, 2, 1, 3).reshape(D_MODEL, N_HEADS * HEAD_DIM)
    w_all = jnp.concatenate(
        [seg[0], seg[1], _pad_heads(seg[2], 4, 64),
         df_q, df_k, _pad_heads(seg[5], 4, 64), gq_q, gq_k, rot_g(gq_k), _pad_heads(seg[8], 2, 64),
         jnp.pad(seg[9], ((0, 0), (0, 256 - MLA_Q_RANK))), seg[10], kr_cols,
         rot_d(df_q), rot_d(df_k), rot_g(gq_q), rot_m(kr_cols), jnp.zeros_like(kr_cols)], axis=1).astype(BF16)

    w_qb = jnp.pad(_pad_heads(p["w_mla_qb"][i], N_HEADS, MLA_QK), ((0, 256 - MLA_Q_RANK), (0, 0)))
    w_qb = jnp.concatenate([w_qb, rot_m(w_qb)], axis=1).astype(BF16)
    kvb = p["w_mla_kvb"][i].reshape(MLA_KV_RANK, N_HEADS, MLA_NOPE + MLA_V)
    w_kvb = jnp.concatenate([_pad_heads(kvb[:, :, :MLA_NOPE].reshape(MLA_KV_RANK, -1), N_HEADS, MLA_NOPE),
                             _pad_heads(kvb[:, :, MLA_NOPE:].reshape(MLA_KV_RANK, -1), N_HEADS, MLA_V)],
                            axis=1).astype(BF16)

    def head_gain(g, n_heads, scale=1.0):
        return _pad_lanes(jnp.tile(_pad_lanes(g * scale, LANES), n_heads), 512)

    def dense_gain(g, reps, scale=1.0):
        return _pad_lanes(jnp.tile(g * scale, reps), 512)

    dg = p["diff_qk_g"][i]
    mq = p["mla_qk_g"][i]
    rows = [
        dense_gain(p["na_qk_g"][i, 0], 4, HEAD_DIM ** -0.5 * LOG2E), dense_gain(p["na_qk_g"][i, 1], 4),
        dense_gain(dg[0], 8, DIFF_DH ** -0.5 * LOG2E), dense_gain(dg[1], 8),
        dense_gain(p["gqa_qk_g"][i, 0], 4, HEAD_DIM ** -0.5 * LOG2E), dense_gain(p["gqa_qk_g"][i, 1], 2),
        head_gain(mq[0], 4, MLA_QK ** -0.5 * LOG2E), head_gain(_pad_lanes(mq[1, :MLA_NOPE], MLA_QK), 4),
        _pad_lanes(p["mla_qa_g"][i], 512), _pad_lanes(p["mla_kva_g"][i], 512),
        _pad_lanes(jnp.pad(mq[1, MLA_NOPE:], (MLA_NOPE, 0)), 512),
    ]
    rows += [rot_d(rows[2]), rot_d(rows[3]), rot_g(rows[4]), rot_g(rows[5]), rot_m(rows[6]), rot_m(rows[10])]
    gvec = jnp.stack(rows + [jnp.zeros((512,), F32)] * (24 - len(rows)))
    conv = jnp.concatenate([p["conv_w"][i], p["conv_b"][i][None], jnp.zeros((4, 2 * D_FF), F32)], axis=0)
    return dict(
        w_all=w_all, w_qb=w_qb, w_kvb=w_kvb, gvec=gvec,
        w_gate=p["w_gate"][i].astype(BF16), w_branch=p["w_branch"][i].astype(BF16), w_out=p["w_out"][i].astype(BF16),
        w_up=p["w_up"][i].astype(BF16), conv=conv, w_down=p["w_down"][i].astype(BF16),
        sub_g=p["diff_subln_g"][i][None],
    )


def kernel(x, c, ctx, c_ctx, w_mod, b_mod, norm1_g, norm2_g, w_in, na_qk_g, na_rel_bias, diff_qk_g, diff_lambda, diff_subln_g, gqa_qk_g, mla_qa_g, mla_kva_g, w_mla_qb, w_mla_kvb, mla_qk_g, w_gate, w_branch, w_out, w_up, conv_w, conv_b, w_down):
    p = dict(w_in=w_in, na_qk_g=na_qk_g, diff_qk_g=diff_qk_g, gqa_qk_g=gqa_qk_g, mla_qa_g=mla_qa_g,
             mla_kva_g=mla_kva_g, w_mla_qb=w_mla_qb, w_mla_kvb=w_mla_kvb, mla_qk_g=mla_qk_g, w_gate=w_gate,
             w_branch=w_branch, w_out=w_out, w_up=w_up, conv_w=conv_w, conv_b=conv_b, w_down=w_down,
             diff_subln_g=diff_subln_g)
    bsz, seq, d = x.shape
    n_ctx = ctx.shape[1]
    depth = w_mod.shape[0]
    assert d == D_MODEL and seq % TM == 0 and n_ctx == TM and seq // GRID_W >= NA_WIN_ROWS
    t_all = seq + n_ctx
    n_lat_tiles = seq // TM

    x_all = jnp.concatenate([x, ctx], axis=1)
    cc = jnp.concatenate([c, c_ctx[None], jnp.zeros((HALO - 1, d), F32)], axis=0)
    mods_all = _mod_all(cc, w_mod.astype(BF16), b_mod[:, None, :])
    mods_all = mods_all.reshape(depth, bsz + HALO, N_MOD, d)
    tables = _rope_tables(seq, n_ctx)
    na_bias = _na_bias_tables(na_rel_bias, seq // GRID_W, n_ctx)

    for i in range(depth):
        last = i == depth - 1
        n_tiles = n_lat_tiles if last else n_lat_tiles + 1
        lambda_init = 0.8 - 0.6 * math.exp(-0.3 * i)
        lw = _layer_weights(i, p)
        mods = mods_all[i]
        g1 = norm1_g[i][None]
        g2 = norm2_g[i][None]
        qkv = _inproj(x_all, mods, g1, lw, tables)
        y = _attention(qkv, na_bias[i], diff_lambda[i], lw["sub_g"], seq, n_tiles, lambda_init)
        x_all = _merge(x_all, mods, g1, y, lw, n_tiles)
        x_all = _ffn(x_all, mods, g2, lw, seq, n_tiles, seq if last else t_all)
    return x_all
```

```python
import functools
import math

import jax
import jax.numpy as jnp
from jax import lax
from jax.experimental import pallas as pl
from jax.experimental.pallas import tpu as pltpu

D_MODEL = 1024
GRID_W = 64
HEAD_DIM = 64
EPS = 1e-6
ROPE_THETA = 10000.0
NEG_INF = -1e30
NA_WIN_R = 8
NA_WIN_C = 16
DIFF_DH = 32
MLA_Q_RANK = 192
MLA_KV_RANK = 128
MLA_NOPE = 64
MLA_ROPE = 32
MLA_V = 64
MLA_QK = MLA_NOPE + MLA_ROPE
N_HEADS = 4
GQA_KV_HEADS = 2
N_MOD = 6
D_FF = 2816
IN_SIZES = (256, 256, 256, 256, 256, 256, 256, 128, 128, MLA_Q_RANK, MLA_KV_RANK, MLA_ROPE)

LANES = 128
MXU_DIM = 256
TM = 256
FF_CHUNK = 256
HALO = 8
LOG2E = math.log2(math.e)
BRANCH_LAYOUT = ((256, 256, 4), (256, 256, 4), (256, 128, 2), (512, 512, 4))
NA_ROWS_PER_TILE = TM // GRID_W
NA_WIN_ROWS = NA_WIN_R + NA_ROWS_PER_TILE
VMEM_LIMIT = 56 * 1024 * 1024

BF16 = jnp.bfloat16
F32 = jnp.float32


def _cparams(sem):
    return pltpu.CompilerParams(dimension_semantics=sem, vmem_limit_bytes=VMEM_LIMIT)


def _layer_spec(a, layer):
    nd = a.ndim
    return pl.BlockSpec((1,) + a.shape[1:], lambda *_: (layer,) + (0,) * (nd - 1))


def _mod_spec(mods, layer, row_of):
    return pl.BlockSpec((1, 1) + mods.shape[2:], lambda *g: (layer, row_of(*g), 0, 0))


def _dot(a, b):
    return jnp.dot(a, b, preferred_element_type=F32)


def _sigmoid(z):
    return 1.0 / (1.0 + jnp.exp(-z))


def _row_rmsnorm(x, g):
    ms = jnp.mean(x * x, axis=-1, keepdims=True)
    return x * lax.rsqrt(ms + EPS) * g


def _mod_kernel(c_ref, w_ref, b_ref, o_ref):
    c = c_ref[...]
    act = (c * _sigmoid(c)).astype(BF16)
    o_ref[0] = _dot(act, w_ref[0]) + b_ref[0]


def _mod_all(cc, w_mod, b_mod):
    n_layers, _, n_out = w_mod.shape
    rows = cc.shape[0]
    tn = 1024
    return pl.pallas_call(
        _mod_kernel,
        grid=(n_layers, n_out // tn),
        in_specs=[
            pl.BlockSpec((rows, D_MODEL), lambda l, j: (0, 0)),
            pl.BlockSpec((1, D_MODEL, tn), lambda l, j: (l, 0, j)),
            pl.BlockSpec((1, 1, tn), lambda l, j: (l, 0, j)),
        ],
        out_specs=pl.BlockSpec((1, rows, tn), lambda l, j: (l, 0, j)),
        out_shape=jax.ShapeDtypeStruct((n_layers, rows, n_out), F32),
        compiler_params=_cparams(("arbitrary", "arbitrary")),
        name="mod_all",
    )(cc, w_mod, b_mod)


def _head_rmsnorm(y, groups, y_rot=None, g=None, g_rot=None, cos=None, sin_signed=None):
    lane = lax.broadcasted_iota(jnp.int32, (1, LANES), 1)
    outs = []
    for c in range(y.shape[1] // LANES):
        yc = y[:, c * LANES:(c + 1) * LANES]
        sq = yc * yc
        if len(groups) == 1:
            ms = jnp.sum(sq, axis=-1, keepdims=True) * (1.0 / groups[0][1])
        else:
            ms = 0.0
            for first, size in groups:
                shift = size.bit_length() - 1
                member = lax.shift_right_logical(lane, shift) == (first >> shift)
                ms = jnp.where(member, jnp.sum(jnp.where(member, sq, 0.0), axis=-1, keepdims=True) * (1.0 / size), ms)
        r = lax.rsqrt(ms + EPS)
        outs.append(yc * r if y_rot is None else
                    (yc * r * g[:, c * LANES:(c + 1) * LANES]) * cos
                    + (y_rot[:, c * LANES:(c + 1) * LANES] * r * g_rot[:, c * LANES:(c + 1) * LANES]) * sin_signed)
    return outs[0] if len(outs) == 1 else jnp.concatenate(outs, axis=1)


def _store_q(ref, y):
    ref[0] = y.astype(ref.dtype)


def _store_kt(ref, y):
    ref[0] = y.T.astype(ref.dtype)


def _store_v(ref, y):
    lane = lax.broadcasted_iota(jnp.int32, (1, 2 * LANES), 1)
    ones_col = jnp.where(jnp.bitwise_and(lane, LANES - 1) == HEAD_DIM, 1.0, 0.0)
    for j in range(y.shape[1] // (2 * LANES)):
        ref[0, j] = (y[:, 2 * j * LANES:2 * (j + 1) * LANES] + ones_col).astype(ref.dtype)


def _inproj_kernel(x_ref, mod_ref, g1_ref, w_ref, wqb_ref, wkvb_ref, gv_ref, tab_ref,
                   qna_ref, kna_ref, vna_ref, qdf_ref, kdf_ref, vdf_ref,
                   qgq_ref, kgq_ref, vgq_ref, qml_ref, kml_ref, vml_ref):
    g1_ref, w_ref, wqb_ref, wkvb_ref, gv_ref = (r.at[0] for r in (g1_ref, w_ref, wqb_ref, wkvb_ref, gv_ref))
    x = x_ref[0]
    mod = mod_ref[0, 0]
    h = (_row_rmsnorm(x, g1_ref[...]) * (1.0 + mod[1:2]) + mod[0:1]).astype(BF16)

    heads64 = ((0, 64), (64, 64))
    maps32 = ((0, 32), (32, 32), (64, 32), (96, 32))
    nope_only = ((0, MLA_NOPE),)
    nope_rope = ((0, MLA_NOPE), (MLA_NOPE, MLA_ROPE))
    rope_only = ((MLA_NOPE, MLA_ROPE),)
    tab = tab_ref[...]
    cos_d, sin_d = tab[:, 0:128], tab[:, 128:256]
    cos_g, sin_g = tab[:, 256:384], tab[:, 384:512]
    cos_m, sin_m = tab[:, 512:640], tab[:, 640:768]

    def proj(lo, width):
        return _dot(h, w_ref[:, lo:lo + width])

    def gain(row, width):
        return gv_ref[row:row + 1, :width]

    def normed(y, groups, row):
        return _head_rmsnorm(y, groups) * gain(row, y.shape[1])

    def roped(y, y_rot, groups, row, row_rot, cos, sin):
        w = y.shape[1]
        return _head_rmsnorm(y, groups, y_rot=y_rot, g=gain(row, w), g_rot=gain(row_rot, w), cos=cos, sin_signed=sin)

    _store_q(qna_ref, normed(proj(0, 256), heads64, 0))
    _store_kt(kna_ref, normed(proj(256, 256), heads64, 1))
    _store_v(vna_ref, proj(512, 512))
    _store_q(qdf_ref, roped(proj(1024, 256), proj(3328, 256), maps32, 2, 11, cos_d, sin_d))
    _store_kt(kdf_ref, roped(proj(1280, 256), proj(3584, 256), maps32, 3, 12, cos_d, sin_d))
    _store_v(vdf_ref, proj(1536, 512))
    _store_q(qgq_ref, roped(proj(2048, 256), proj(3840, 256), heads64, 4, 13, cos_g, sin_g))
    k_gq = proj(2304, 256)
    _store_kt(kgq_ref, roped(k_gq[:, :LANES], k_gq[:, LANES:], heads64, 5, 14, cos_g, sin_g))
    _store_v(vgq_ref, proj(2560, 256))
    qa = proj(2816, 256)
    ms_qa = jnp.sum(qa * qa, axis=-1, keepdims=True) * (1.0 / MLA_Q_RANK)
    qa_n = (qa * lax.rsqrt(ms_qa + EPS) * gain(8, 256)).astype(BF16)
    q_ml = _dot(qa_n, wqb_ref[:, :512])
    q_ml_rot = _dot(qa_n, wqb_ref[:, 512:])
    _store_q(qml_ref, roped(q_ml, q_ml_rot, nope_rope, 6, 15, cos_m, sin_m))
    kva_kr = proj(3072, 256)
    kva_n = _row_rmsnorm(kva_kr[:, :LANES], gain(9, 128)).astype(BF16)
    kv = _dot(kva_n, wkvb_ref[...])
    k_nope = normed(kv[:, :512], nope_only, 7)
    kr_rot = proj(4096, 256)[:, :LANES]
    kr_r = roped(kva_kr[:, LANES:], kr_rot, rope_only, 10, 16, cos_m, sin_m)
    _store_kt(kml_ref, k_nope + jnp.concatenate([kr_r] * N_HEADS, axis=1))
    _store_v(vml_ref, kv[:, 512:])


def _inproj(x_all, mods, lw, tables, layer):
    bsz, t_all, _ = x_all.shape
    nt = t_all // TM
    ctx_tile = nt - 1
    ctx_row = mods.shape[1] - HALO
    weights = (lw["g1"], lw["w_all"], lw["w_qb"], lw["w_kvb"], lw["gvec"])

    def q_spec(width):
        return (pl.BlockSpec((1, TM, width), lambda t, b: (b, t, 0)),
                jax.ShapeDtypeStruct((bsz, t_all, width), BF16))

    def kt_spec(width):
        return (pl.BlockSpec((1, width, TM), lambda t, b: (b, 0, t)),
                jax.ShapeDtypeStruct((bsz, width, t_all), BF16))

    def v_spec(nh):
        return (pl.BlockSpec((1, nh // 2, TM, 2 * LANES), lambda t, b: (b, 0, t, 0)),
                jax.ShapeDtypeStruct((bsz, nh // 2, t_all, 2 * LANES), BF16))

    outs = []
    for wq, wk, n_kv in BRANCH_LAYOUT:
        outs += [q_spec(wq), kt_spec(wk), v_spec(n_kv)]
    return pl.pallas_call(
        _inproj_kernel,
        grid=(nt, bsz),
        in_specs=[
            pl.BlockSpec((1, TM, D_MODEL), lambda t, b: (b, t, 0)),
            _mod_spec(mods, layer, lambda t, b: jnp.where(t == ctx_tile, ctx_row, b)),
            *[_layer_spec(w, layer) for w in weights],
            pl.BlockSpec((TM, tables.shape[1]), lambda t, b: (t, 0)),
        ],
        out_specs=[o[0] for o in outs],
        out_shape=[o[1] for o in outs],
        compiler_params=_cparams(("arbitrary", "arbitrary")),
        name="inproj",
    )(x_all, mods, *weights, tables)


def _softmax_pv(q, kt, v, half, bias=None):
    s = _dot(q, kt)
    if bias is not None:
        s = s + bias
    p = jnp.exp2(s - jnp.max(s, axis=-1, keepdims=True)).astype(BF16)
    o = _dot(p, v)[:, half * LANES:(half + 1) * LANES]
    return o[:, :HEAD_DIM] * (1.0 / o[:, HEAD_DIM:HEAD_DIM + 1])


def _attn_kernel(seq, lambda_init, qna_ref, kna_ref, vna_ref, qdf_ref, kdf_ref, vdf_ref,
                 qgq_ref, kgq_ref, vgq_ref, qml_ref, kml_ref, vml_ref, bias_ref, lam_ref, sub_ref, o_ref):
    bias_ref, lam_ref, sub_ref = bias_ref.at[0], lam_ref.at[0], sub_ref.at[0]
    qi = pl.program_id(1)
    t_all = vna_ref.shape[2]
    n_lat_tiles = seq // TM
    lp = lam_ref[...]
    lam = (jnp.exp(jnp.sum(lp[0:1] * lp[1:2], axis=-1, keepdims=True))
           - jnp.exp(jnp.sum(lp[2:3] * lp[3:4], axis=-1, keepdims=True)) + lambda_init)
    lane = lax.broadcasted_iota(jnp.int32, (TM, LANES), 1)

    def q_lanes(q_ref, tile, first, size):
        shift = size.bit_length() - 1
        qf = q_ref[0, :, tile * LANES:(tile + 1) * LANES].astype(F32)
        return jnp.where(lax.shift_right_logical(lane, shift) == (first >> shift), qf, 0.0).astype(BF16)

    def tile_rows(tile):
        return slice(tile * LANES, (tile + 1) * LANES)

    def paired_heads(q_ref, kt_ref, v_ref, k_lo):
        return [_softmax_pv(q_lanes(q_ref, h // 2, HEAD_DIM * (h % 2), HEAD_DIM),
                            kt_ref[0, tile_rows(h // 2), k_lo:t_all], v_ref[0, h // 2, k_lo:t_all, :], h % 2)
                for h in range(N_HEADS)]

    def grouped_heads(k_lo):
        outs = []
        for h in range(N_HEADS):
            g = h // (N_HEADS // GQA_KV_HEADS)
            outs.append(_softmax_pv(q_lanes(qgq_ref, h % 2, HEAD_DIM * g, HEAD_DIM),
                                    kgq_ref[0, :, k_lo:t_all], vgq_ref[0, 0, k_lo:t_all, :], g))
        return outs

    def latent_heads(k_lo):
        return [_softmax_pv(qml_ref[0, :, tile_rows(h)], kml_ref[0, tile_rows(h), k_lo:t_all],
                            vml_ref[0, h // 2, k_lo:t_all, :], h % 2) for h in range(N_HEADS)]

    def diff_heads(k_lo):
        outs = []
        for h in range(N_HEADS):
            kt = kdf_ref[0, tile_rows(h // 2), k_lo:t_all]
            v = vdf_ref[0, h // 2, k_lo:t_all, :]
            first = HEAD_DIM * (h % 2)
            o = (_softmax_pv(q_lanes(qdf_ref, h // 2, first, DIFF_DH), kt, v, h % 2)
                 - lam * _softmax_pv(q_lanes(qdf_ref, h // 2, first + DIFF_DH, DIFF_DH), kt, v, h % 2))
            ms = jnp.mean(o * o, axis=-1, keepdims=True)
            outs.append(o * lax.rsqrt(ms + EPS) * sub_ref[...] * (1.0 - lambda_init))
        return outs

    def band_heads():
        ws = pl.multiple_of(jnp.clip(NA_ROWS_PER_TILE * qi - NA_WIN_R // 2, 0,
                                     seq // GRID_W - NA_WIN_ROWS) * GRID_W, TM)
        n_win = NA_WIN_ROWS * GRID_W
        outs = []
        for h in range(N_HEADS):
            rows = tile_rows(h // 2)
            kt = jnp.concatenate([kna_ref[0, rows, pl.ds(ws, n_win)], kna_ref[0, rows, seq:t_all]], axis=1)
            v = jnp.concatenate([vna_ref[0, h // 2, pl.ds(ws, n_win), :], vna_ref[0, h // 2, seq:t_all, :]], axis=0)
            outs.append(_softmax_pv(q_lanes(qna_ref, h // 2, HEAD_DIM * (h % 2), HEAD_DIM), kt, v, h % 2,
                                    bias_ref[0, h]))
        return outs

    def run(k_lo, latent):
        outs = band_heads() if latent else paired_heads(qna_ref, kna_ref, vna_ref, k_lo)
        outs += diff_heads(k_lo)
        outs += grouped_heads(k_lo)
        outs += latent_heads(k_lo)
        o_ref[0] = jnp.concatenate(outs, axis=1).astype(o_ref.dtype)

    @pl.when(qi < n_lat_tiles)
    def _():
        run(0, True)

    @pl.when(qi >= n_lat_tiles)
    def _():
        run(seq, False)


def _attention(qkv, bias, lam_p, sub_g, seq, n_q_tiles, lambda_init, layer):
    bsz, t_all, _ = qkv[0].shape
    n_lat_tiles = seq // TM
    n_cls = bias.shape[1]
    in_specs = []
    for wq, wk, n_kv in BRANCH_LAYOUT:
        in_specs += [
            pl.BlockSpec((1, TM, wq), lambda b, i: (b, i, 0)),
            pl.BlockSpec((1, wk, t_all), lambda b, i: (b, 0, 0)),
            pl.BlockSpec((1, n_kv // 2, t_all, 2 * LANES), lambda b, i: (b, 0, 0, 0)),
        ]
    in_specs += [
        pl.BlockSpec((1, 1) + bias.shape[2:],
                     lambda b, i: (layer, jnp.where(i == 0, 0, jnp.where(i >= n_lat_tiles - 1, n_cls - 1, 1)), 0, 0, 0)),
        _layer_spec(lam_p, layer),
        _layer_spec(sub_g, layer),
    ]
    width = 4 * N_HEADS * HEAD_DIM
    return pl.pallas_call(
        functools.partial(_attn_kernel, seq, lambda_init),
        grid=(bsz, n_q_tiles),
        in_specs=in_specs,
        out_specs=pl.BlockSpec((1, TM, width), lambda b, i: (b, i, 0)),
        out_shape=jax.ShapeDtypeStruct((bsz, t_all, width), BF16),
        compiler_params=_cparams(("arbitrary", "arbitrary")),
        name="attn",
    )(*qkv, bias, lam_p, sub_g)


def _merge_kernel(x_ref, mod_ref, g1_ref, y_ref, wg_ref, wb_ref, wo_ref, o_ref):
    g1_ref, wg_ref, wb_ref, wo_ref = (r.at[0] for r in (g1_ref, wg_ref, wb_ref, wo_ref))
    x = x_ref[0]
    mod = mod_ref[0, 0]
    h = (_row_rmsnorm(x, g1_ref[...]) * (1.0 + mod[1:2]) + mod[0:1]).astype(BF16)
    acc = None
    width = wb_ref.shape[1]
    for i in range(wb_ref.shape[0]):
        term = _sigmoid(_dot(h, wg_ref[i])) * _dot(y_ref[0, :, i * width:(i + 1) * width], wb_ref[i])
        acc = term if acc is None else acc + term
    o_ref[0] = x + mod[2:3] * _dot(acc.astype(BF16), wo_ref[...])


def _merge(x_all, mods, y, lw, n_tiles, layer):
    bsz, t_all, _ = x_all.shape
    ctx_tile = t_all // TM - 1
    ctx_row = mods.shape[1] - HALO
    tok = lambda w: pl.BlockSpec((1, TM, w), lambda b, t: (b, t, 0))
    return pl.pallas_call(
        _merge_kernel,
        grid=(bsz, n_tiles),
        in_specs=[
            tok(D_MODEL),
            _mod_spec(mods, layer, lambda b, t: jnp.where(t == ctx_tile, ctx_row, b)),
            _layer_spec(lw["g1"], layer), tok(y.shape[2]),
            _layer_spec(lw["w_gate"], layer), _layer_spec(lw["w_branch"], layer), _layer_spec(lw["w_out"], layer),
        ],
        out_specs=tok(D_MODEL),
        out_shape=jax.ShapeDtypeStruct(x_all.shape, F32),
        input_output_aliases={0: 0},
        compiler_params=_cparams(("arbitrary", "arbitrary")),
        name="merge",
    )(x_all, mods, lw["g1"], y, lw["w_gate"], lw["w_branch"], lw["w_out"])


def _ffn_kernel(seq, t_all, x_ref, xp_ref, xn_ref, mod_ref, g2_ref, wu_ref, cw_ref, wd_ref, o_ref, act_scr):
    g2_ref, wu_ref, cw_ref, wd_ref = (r.at[0] for r in (g2_ref, wu_ref, cw_ref, wd_ref))
    t = pl.program_id(1)
    x = x_ref[0]
    mod = mod_ref[0, 0]
    x_ext = jnp.concatenate([x, xp_ref[0], xn_ref[0]], axis=0)
    hx = (_row_rmsnorm(x_ext, g2_ref[...]) * (1.0 + mod[4:5]) + mod[3:4]).astype(BF16)

    row = lax.broadcasted_iota(jnp.int32, (TM, FF_CHUNK), 0)
    g_first = t * TM
    has_prev = jnp.logical_and(g_first != 0, g_first != seq)
    has_next = jnp.logical_and(g_first + TM != seq, g_first + TM != t_all)
    first_row = row == 0
    last_row = row == TM - 1

    def conv(lo):
        u_ext = _dot(hx, wu_ref[:, lo:lo + FF_CHUNK])
        c = cw_ref[:, lo:lo + FF_CHUNK]
        u = u_ext[0:TM]
        prev_row = jnp.where(has_prev, u_ext[TM + HALO - 1:TM + HALO], 0.0)
        next_row = jnp.where(has_next, u_ext[TM + HALO:TM + HALO + 1], 0.0)
        up = jnp.where(first_row, prev_row, pltpu.roll(u, 1, 0))
        dn = jnp.where(last_row, next_row, pltpu.roll(u, TM - 1, 0))
        return c[3:4] + up * c[0:1] + u * c[1:2] + dn * c[2:3]

    for j in range(D_FF // FF_CHUNK):
        lo = j * FF_CHUNK
        gate = conv(lo)
        val = conv(D_FF + lo)
        act_scr[:, lo:lo + FF_CHUNK] = (gate * _sigmoid(gate) * val).astype(BF16)
    o_ref[0] = x + mod[5:6] * _dot(act_scr[...], wd_ref[...])


def _ffn(x_all, mods, lw, seq, n_tiles, out_rows, layer):
    bsz, t_all, _ = x_all.shape
    nt_all = t_all // TM
    ctx_tile = nt_all - 1
    ctx_row = mods.shape[1] - HALO
    blocks_per_tile = TM // HALO
    last_block = t_all // HALO - 1
    weights = (lw["g2"], lw["w_up"], lw["conv"], lw["w_down"])
    return pl.pallas_call(
        functools.partial(_ffn_kernel, seq, t_all),
        grid=(bsz, n_tiles),
        in_specs=[
            pl.BlockSpec((1, TM, D_MODEL), lambda b, t: (b, t, 0)),
            pl.BlockSpec((1, HALO, D_MODEL), lambda b, t: (b, jnp.maximum(t * blocks_per_tile - 1, 0), 0)),
            pl.BlockSpec((1, HALO, D_MODEL), lambda b, t: (b, jnp.minimum((t + 1) * blocks_per_tile, last_block), 0)),
            _mod_spec(mods, layer, lambda b, t: jnp.where(t == ctx_tile, ctx_row, b)),
            *[_layer_spec(w, layer) for w in weights],
        ],
        out_specs=pl.BlockSpec((1, TM, D_MODEL), lambda b, t: (b, t, 0)),
        out_shape=jax.ShapeDtypeStruct((bsz, out_rows, D_MODEL), F32),
        scratch_shapes=[pltpu.VMEM((TM, D_FF), BF16)],
        compiler_params=_cparams(("arbitrary", "arbitrary")),
        name="ffn",
    )(x_all, x_all, x_all, mods, *weights)


def _pad_heads(w, n_heads, d):
    lead = w.shape[:-1]
    w = w.reshape(lead + (n_heads, d))
    w = jnp.pad(w, [(0, 0)] * len(lead) + [(0, 0), (0, LANES - d)])
    return w.reshape(lead + (n_heads * LANES,))


def _pad_last(a, before, after):
    return jnp.pad(a, [(0, 0)] * (a.ndim - 1) + [(before, after)])


def _pad_lanes(v, width):
    return _pad_last(v, 0, width - v.shape[-1])


def _swap_halves(w, lane0, used, half):
    lead = w.shape[:-1]
    tiles = w.reshape(lead + (w.shape[-1] // LANES, LANES))
    grp = tiles[..., lane0:lane0 + used].reshape(lead + (tiles.shape[-2], used // (2 * half), 2, half))
    grp = grp[..., ::-1, :].reshape(lead + (tiles.shape[-2], used))
    out = jnp.pad(grp, [(0, 0)] * (len(lead) + 1) + [(lane0, LANES - lane0 - used)])
    return out.reshape(w.shape)


def _rope_table(n_lat, n_ctx, rot_dim, lane0, lanes_used):
    t = jnp.arange(n_lat)
    row = (t // GRID_W).astype(F32)
    col = (t % GRID_W).astype(F32)
    n_axis = rot_dim // 4
    inv_freq = ROPE_THETA ** (-jnp.arange(n_axis, dtype=F32) / n_axis)
    ang = jnp.concatenate([row[:, None] * inv_freq, col[:, None] * inv_freq], axis=-1)
    cos_g = jnp.concatenate([jnp.cos(ang), jnp.cos(ang)], axis=-1)
    sin_g = jnp.concatenate([-jnp.sin(ang), jnp.sin(ang)], axis=-1)
    reps = lanes_used // rot_dim
    cos = jnp.ones((n_lat, LANES), F32).at[:, lane0:lane0 + lanes_used].set(jnp.tile(cos_g, (1, reps)))
    sin = jnp.zeros((n_lat, LANES), F32).at[:, lane0:lane0 + lanes_used].set(jnp.tile(sin_g, (1, reps)))
    cos = jnp.concatenate([cos, jnp.ones((n_ctx, LANES), F32)], axis=0)
    sin = jnp.concatenate([sin, jnp.zeros((n_ctx, LANES), F32)], axis=0)
    return cos, sin


def _rope_tables(n_lat, n_ctx):
    cd, sd = _rope_table(n_lat, n_ctx, DIFF_DH, 0, LANES)
    cg, sg = _rope_table(n_lat, n_ctx, HEAD_DIM, 0, LANES)
    cm, sm = _rope_table(n_lat, n_ctx, MLA_ROPE, MLA_NOPE, MLA_ROPE)
    return jnp.concatenate([cd, sd, cg, sg, cm, sm], axis=1)


def _na_bias_tables(rel_bias, rows, n_ctx):
    cq = jnp.arange(GRID_W)
    col_start = jnp.clip(cq - NA_WIN_C // 2, 0, GRID_W - NA_WIN_C)
    ck = jnp.arange(GRID_W)
    col_in = (ck[None, :] >= col_start[:, None]) & (ck[None, :] < col_start[:, None] + NA_WIN_C)
    col_idx = jnp.clip(ck[None, :] - cq[:, None], -(NA_WIN_C - 1), NA_WIN_C - 1) + NA_WIN_C - 1
    col_sel = jax.nn.one_hot(col_idx.reshape(-1), 2 * NA_WIN_C - 1, dtype=F32)
    tables = []
    for r0 in (0, NA_ROWS_PER_TILE, rows - NA_ROWS_PER_TILE):
        ws = min(max(r0 - NA_WIN_R // 2, 0), rows - NA_WIN_ROWS)
        r = r0 + jnp.arange(NA_ROWS_PER_TILE)
        kr = ws + jnp.arange(NA_WIN_ROWS)
        rs = jnp.clip(r - NA_WIN_R // 2, 0, rows - NA_WIN_R)
        in_band = (kr[None, :] >= rs[:, None]) & (kr[None, :] < rs[:, None] + NA_WIN_R)
        row_idx = jnp.clip(kr[None, :] - r[:, None] + NA_WIN_R - 1, 0, 2 * NA_WIN_R - 2)
        row_sel = jax.nn.one_hot(row_idx.reshape(-1), 2 * NA_WIN_R - 1, dtype=F32)
        b = jnp.einsum("pa,lhac,qc->lhpq", row_sel, rel_bias, col_sel, precision=lax.Precision.HIGHEST)
        b = b.reshape(rel_bias.shape[:2] + (NA_ROWS_PER_TILE, NA_WIN_ROWS, GRID_W, GRID_W)).transpose(0, 1, 2, 4, 3, 5)
        ok = in_band[:, None, :, None] & col_in[None, :, None, :]
        b = jnp.where(ok[None, None], b * LOG2E, NEG_INF)
        b = b.reshape(rel_bias.shape[:2] + (TM, NA_WIN_ROWS * GRID_W))
        tables.append(jnp.concatenate([b, jnp.zeros(rel_bias.shape[:2] + (TM, n_ctx), F32)], axis=-1))
    return jnp.stack(tables, axis=1)


def _prep_weights(p):
    w_in = p["w_in"]
    depth = w_in.shape[0]
    offs = [0]
    for n in IN_SIZES:
        offs.append(offs[-1] + n)
    seg = [w_in[..., offs[j]:offs[j + 1]] for j in range(len(IN_SIZES))]
    kr_cols = _pad_last(seg[11], MLA_NOPE, LANES - MLA_NOPE - MLA_ROPE)
    rot_d = functools.partial(_swap_halves, lane0=0, used=LANES, half=DIFF_DH // 2)
    rot_g = functools.partial(_swap_halves, lane0=0, used=LANES, half=HEAD_DIM // 2)
    rot_m = functools.partial(_swap_halves, lane0=MLA_NOPE, used=MLA_ROPE, half=MLA_ROPE // 2)
    df_q, df_k, gq_k = seg[3], seg[4], seg[7]
    gq_q = seg[6].reshape(depth, D_MODEL, 2, 2, HEAD_DIM).swapaxes(2, 3).reshape(depth, D_MODEL, N_HEADS * HEAD_DIM)
    w_all = jnp.concatenate(
        [seg[0], seg[1], _pad_heads(seg[2], 4, 64),
         df_q, df_k, _pad_heads(seg[5], 4, 64), gq_q, gq_k, rot_g(gq_k), _pad_heads(seg[8], 2, 64),
         _pad_last(seg[9], 0, 256 - MLA_Q_RANK), seg[10], kr_cols,
         rot_d(df_q), rot_d(df_k), rot_g(gq_q), rot_m(kr_cols), jnp.zeros_like(kr_cols)], axis=-1).astype(BF16)

    w_qb = jnp.pad(_pad_heads(p["w_mla_qb"], N_HEADS, MLA_QK), ((0, 0), (0, 256 - MLA_Q_RANK), (0, 0)))
    w_qb = jnp.concatenate([w_qb, rot_m(w_qb)], axis=-1).astype(BF16)
    kvb = p["w_mla_kvb"].reshape(depth, MLA_KV_RANK, N_HEADS, MLA_NOPE + MLA_V)
    w_kvb = jnp.concatenate(
        [_pad_heads(kvb[..., :MLA_NOPE].reshape(depth, MLA_KV_RANK, -1), N_HEADS, MLA_NOPE),
         _pad_heads(kvb[..., MLA_NOPE:].reshape(depth, MLA_KV_RANK, -1), N_HEADS, MLA_V)], axis=-1).astype(BF16)

    def head_gain(g, n_heads, scale=1.0):
        return _pad_lanes(jnp.tile(_pad_lanes(g * scale, LANES), (1, n_heads)), 512)

    def dense_gain(g, reps, scale=1.0):
        return _pad_lanes(jnp.tile(g * scale, (1, reps)), 512)

    dg = p["diff_qk_g"]
    mq = p["mla_qk_g"]
    rows = [
        dense_gain(p["na_qk_g"][:, 0], 4, HEAD_DIM ** -0.5 * LOG2E), dense_gain(p["na_qk_g"][:, 1], 4),
        dense_gain(dg[:, 0], 8, DIFF_DH ** -0.5 * LOG2E), dense_gain(dg[:, 1], 8),
        dense_gain(p["gqa_qk_g"][:, 0], 4, HEAD_DIM ** -0.5 * LOG2E), dense_gain(p["gqa_qk_g"][:, 1], 2),
        head_gain(mq[:, 0], 4, MLA_QK ** -0.5 * LOG2E), head_gain(_pad_lanes(mq[:, 1, :MLA_NOPE], MLA_QK), 4),
        _pad_lanes(p["mla_qa_g"], 512), _pad_lanes(p["mla_kva_g"], 512),
        _pad_lanes(_pad_last(mq[:, 1, MLA_NOPE:], MLA_NOPE, 0), 512),
    ]
    rows += [rot_d(rows[2]), rot_d(rows[3]), rot_g(rows[4]), rot_g(rows[5]), rot_m(rows[6]), rot_m(rows[10])]
    gvec = jnp.stack(rows + [jnp.zeros((depth, 512), F32)] * (24 - len(rows)), axis=1)
    conv = jnp.concatenate([p["conv_w"], p["conv_b"][:, None], jnp.zeros((depth, 4, 2 * D_FF), F32)], axis=1)
    return dict(
        w_all=w_all, w_qb=w_qb, w_kvb=w_kvb, gvec=gvec,
        w_gate=p["w_gate"].astype(BF16), w_branch=p["w_branch"].astype(BF16), w_out=p["w_out"].astype(BF16),
        w_up=p["w_up"].astype(BF16), conv=conv, w_down=p["w_down"].astype(BF16),
        sub_g=p["diff_subln_g"][:, None], g1=p["norm1_g"][:, None], g2=p["norm2_g"][:, None],
    )


def kernel(x, c, ctx, c_ctx, w_mod, b_mod, norm1_g, norm2_g, w_in, na_qk_g, na_rel_bias, diff_qk_g, diff_lambda, diff_subln_g, gqa_qk_g, mla_qa_g, mla_kva_g, w_mla_qb, w_mla_kvb, mla_qk_g, w_gate, w_branch, w_out, w_up, conv_w, conv_b, w_down):
    p = dict(w_in=w_in, na_qk_g=na_qk_g, diff_qk_g=diff_qk_g, gqa_qk_g=gqa_qk_g, mla_qa_g=mla_qa_g,
             mla_kva_g=mla_kva_g, w_mla_qb=w_mla_qb, w_mla_kvb=w_mla_kvb, mla_qk_g=mla_qk_g, w_gate=w_gate,
             w_branch=w_branch, w_out=w_out, w_up=w_up, conv_w=conv_w, conv_b=conv_b, w_down=w_down,
             diff_subln_g=diff_subln_g, norm1_g=norm1_g, norm2_g=norm2_g)
    bsz, seq, d = x.shape
    n_ctx = ctx.shape[1]
    depth = w_mod.shape[0]
    assert d == D_MODEL and seq % TM == 0 and n_ctx == TM and seq // GRID_W >= NA_WIN_ROWS
    t_all = seq + n_ctx
    n_lat_tiles = seq // TM

    x_all = jnp.concatenate([x, ctx], axis=1)
    cc = jnp.concatenate([c, c_ctx[None], jnp.zeros((HALO - 1, d), F32)], axis=0)
    mods = _mod_all(cc, w_mod.astype(BF16), b_mod[:, None, :]).reshape(depth, bsz + HALO, N_MOD, d)
    tables = _rope_tables(seq, n_ctx)
    na_bias = _na_bias_tables(na_rel_bias, seq // GRID_W, n_ctx)
    lw = _prep_weights(p)

    for i in range(depth):
        last = i == depth - 1
        n_tiles = n_lat_tiles if last else n_lat_tiles + 1
        lambda_init = 0.8 - 0.6 * math.exp(-0.3 * i)
        qkv = _inproj(x_all, mods, lw, tables, i)
        y = _attention(qkv, na_bias, diff_lambda, lw["sub_g"], seq, n_tiles, lambda_init, i)
        x_all = _merge(x_all, mods, y, lw, n_tiles, i)
        x_all = _ffn(x_all, mods, lw, seq, n_tiles, seq if last else t_all, i)
    return x_all
```

```python
import functools
import math

import jax
import jax.numpy as jnp
from jax import lax
from jax.experimental import pallas as pl
from jax.experimental.pallas import tpu as pltpu

D_MODEL = 1024
GRID_W = 64
HEAD_DIM = 64
EPS = 1e-6
ROPE_THETA = 10000.0
NEG_INF = -1e30
NA_WIN_R = 8
NA_WIN_C = 16
DIFF_DH = 32
MLA_Q_RANK = 192
MLA_KV_RANK = 128
MLA_NOPE = 64
MLA_ROPE = 32
MLA_V = 64
MLA_QK = MLA_NOPE + MLA_ROPE
N_HEADS = 4
GQA_KV_HEADS = 2
N_MOD = 6
D_FF = 2816
IN_SIZES = (256, 256, 256, 256, 256, 256, 256, 128, 128, MLA_Q_RANK, MLA_KV_RANK, MLA_ROPE)

LANES = 128
MXU_DIM = 256
TM = 256
FFN_TM = 512
FF_CHUNK = 256
HALO = 8
LOG2E = math.log2(math.e)
BRANCH_LAYOUT = ((256, 256, 4), (256, 256, 4), (256, 128, 2), (512, 512, 4))
NA_ROWS_PER_TILE = TM // GRID_W
NA_WIN_ROWS = NA_WIN_R + NA_ROWS_PER_TILE
VMEM_LIMIT = 56 * 1024 * 1024

BF16 = jnp.bfloat16
F32 = jnp.float32


def _cparams(sem):
    return pltpu.CompilerParams(dimension_semantics=sem, vmem_limit_bytes=VMEM_LIMIT)


def _layer_spec(a, layer):
    nd = a.ndim
    return pl.BlockSpec((1,) + a.shape[1:], lambda *_: (layer,) + (0,) * (nd - 1))


def _mod_spec(mods, layer, row_of):
    return pl.BlockSpec((1, 1) + mods.shape[2:], lambda *g: (layer, row_of(*g), 0, 0))


def _dot(a, b):
    return jnp.dot(a, b, preferred_element_type=F32)


def _sigmoid(z):
    return 1.0 / (1.0 + jnp.exp(-z))


def _row_rmsnorm(x, g):
    ms = jnp.mean(x * x, axis=-1, keepdims=True)
    return x * lax.rsqrt(ms + EPS) * g


def _mod_kernel(c_ref, w_ref, b_ref, o_ref):
    c = c_ref[...]
    act = (c * _sigmoid(c)).astype(BF16)
    o_ref[0] = _dot(act, w_ref[0]) + b_ref[0]


def _mod_all(cc, w_mod, b_mod):
    n_layers, _, n_out = w_mod.shape
    rows = cc.shape[0]
    tn = 1024
    return pl.pallas_call(
        _mod_kernel,
        grid=(n_layers, n_out // tn),
        in_specs=[
            pl.BlockSpec((rows, D_MODEL), lambda l, j: (0, 0)),
            pl.BlockSpec((1, D_MODEL, tn), lambda l, j: (l, 0, j)),
            pl.BlockSpec((1, 1, tn), lambda l, j: (l, 0, j)),
        ],
        out_specs=pl.BlockSpec((1, rows, tn), lambda l, j: (l, 0, j)),
        out_shape=jax.ShapeDtypeStruct((n_layers, rows, n_out), F32),
        compiler_params=_cparams(("arbitrary", "arbitrary")),
        name="mod_all",
    )(cc, w_mod, b_mod)


def _head_rmsnorm(y, groups, y_rot=None, g=None, g_rot=None, cos=None, sin_signed=None):
    lane = lax.broadcasted_iota(jnp.int32, (1, LANES), 1)
    outs = []
    for c in range(y.shape[1] // LANES):
        yc = y[:, c * LANES:(c + 1) * LANES]
        sq = yc * yc
        if len(groups) == 1:
            ms = jnp.sum(sq, axis=-1, keepdims=True) * (1.0 / groups[0][1])
        else:
            ms = 0.0
            for first, size in groups:
                shift = size.bit_length() - 1
                member = lax.shift_right_logical(lane, shift) == (first >> shift)
                ms = jnp.where(member, jnp.sum(jnp.where(member, sq, 0.0), axis=-1, keepdims=True) * (1.0 / size), ms)
        r = lax.rsqrt(ms + EPS)
        outs.append(yc * r if y_rot is None else
                    (yc * r * g[:, c * LANES:(c + 1) * LANES]) * cos
                    + (y_rot[:, c * LANES:(c + 1) * LANES] * r * g_rot[:, c * LANES:(c + 1) * LANES]) * sin_signed)
    return outs[0] if len(outs) == 1 else jnp.concatenate(outs, axis=1)


def _store_q(ref, y):
    ref[0] = y.astype(ref.dtype)


def _store_kt(ref, y):
    ref[0] = y.T.astype(ref.dtype)


def _store_v(ref, y):
    lane = lax.broadcasted_iota(jnp.int32, (1, 2 * LANES), 1)
    ones_col = jnp.where(jnp.bitwise_and(lane, LANES - 1) == HEAD_DIM, 1.0, 0.0)
    for j in range(y.shape[1] // (2 * LANES)):
        ref[0, j] = (y[:, 2 * j * LANES:2 * (j + 1) * LANES] + ones_col).astype(ref.dtype)


def _inproj_kernel(x_ref, mod_ref, g1_ref, w_ref, wqb_ref, wkvb_ref, gv_ref, tab_ref,
                   qna_ref, kna_ref, vna_ref, qdf_ref, kdf_ref, vdf_ref,
                   qgq_ref, kgq_ref, vgq_ref, qml_ref, kml_ref, vml_ref):
    g1_ref, w_ref, wqb_ref, wkvb_ref, gv_ref = (r.at[0] for r in (g1_ref, w_ref, wqb_ref, wkvb_ref, gv_ref))
    x = x_ref[0]
    mod = mod_ref[0, 0]
    h = (_row_rmsnorm(x, g1_ref[...]) * (1.0 + mod[1:2]) + mod[0:1]).astype(BF16)

    heads64 = ((0, 64), (64, 64))
    maps32 = ((0, 32), (32, 32), (64, 32), (96, 32))
    nope_only = ((0, MLA_NOPE),)
    nope_rope = ((0, MLA_NOPE), (MLA_NOPE, MLA_ROPE))
    rope_only = ((MLA_NOPE, MLA_ROPE),)
    tab = tab_ref[...]
    cos_d, sin_d = tab[:, 0:128], tab[:, 128:256]
    cos_g, sin_g = tab[:, 256:384], tab[:, 384:512]
    cos_m, sin_m = tab[:, 512:640], tab[:, 640:768]

    def proj(lo, width):
        return _dot(h, w_ref[:, lo:lo + width])

    def gain(row, width):
        return gv_ref[row:row + 1, :width]

    def normed(y, groups, row):
        return _head_rmsnorm(y, groups) * gain(row, y.shape[1])

    def roped(y, y_rot, groups, row, row_rot, cos, sin):
        w = y.shape[1]
        return _head_rmsnorm(y, groups, y_rot=y_rot, g=gain(row, w), g_rot=gain(row_rot, w), cos=cos, sin_signed=sin)

    _store_q(qna_ref, normed(proj(0, 256), heads64, 0))
    _store_kt(kna_ref, normed(proj(256, 256), heads64, 1))
    _store_v(vna_ref, proj(512, 512))
    _store_q(qdf_ref, roped(proj(1024, 256), proj(3328, 256), maps32, 2, 11, cos_d, sin_d))
    _store_kt(kdf_ref, roped(proj(1280, 256), proj(3584, 256), maps32, 3, 12, cos_d, sin_d))
    _store_v(vdf_ref, proj(1536, 512))
    _store_q(qgq_ref, roped(proj(2048, 256), proj(3840, 256), heads64, 4, 13, cos_g, sin_g))
    k_gq = proj(2304, 256)
    _store_kt(kgq_ref, roped(k_gq[:, :LANES], k_gq[:, LANES:], heads64, 5, 14, cos_g, sin_g))
    _store_v(vgq_ref, proj(2560, 256))
    qa = proj(2816, 256)
    ms_qa = jnp.sum(qa * qa, axis=-1, keepdims=True) * (1.0 / MLA_Q_RANK)
    qa_n = (qa * lax.rsqrt(ms_qa + EPS) * gain(8, 256)).astype(BF16)
    q_ml = _dot(qa_n, wqb_ref[:, :512])
    q_ml_rot = _dot(qa_n, wqb_ref[:, 512:])
    _store_q(qml_ref, roped(q_ml, q_ml_rot, nope_rope, 6, 15, cos_m, sin_m))
    kva_kr = proj(3072, 256)
    kva_n = _row_rmsnorm(kva_kr[:, :LANES], gain(9, 128)).astype(BF16)
    kv = _dot(kva_n, wkvb_ref[...])
    k_nope = normed(kv[:, :512], nope_only, 7)
    kr_rot = proj(4096, 256)[:, :LANES]
    kr_r = roped(kva_kr[:, LANES:], kr_rot, rope_only, 10, 16, cos_m, sin_m)
    _store_kt(kml_ref, k_nope + jnp.concatenate([kr_r] * N_HEADS, axis=1))
    _store_v(vml_ref, kv[:, 512:])


def _inproj(x_all, mods, lw, tables, layer):
    bsz, t_all, _ = x_all.shape
    nt = t_all // TM
    ctx_tile = nt - 1
    ctx_row = mods.shape[1] - HALO
    weights = (lw["g1"], lw["w_all"], lw["w_qb"], lw["w_kvb"], lw["gvec"])

    def q_spec(width):
        return (pl.BlockSpec((1, TM, width), lambda t, b: (b, t, 0)),
                jax.ShapeDtypeStruct((bsz, t_all, width), BF16))

    def kt_spec(width):
        return (pl.BlockSpec((1, width, TM), lambda t, b: (b, 0, t)),
                jax.ShapeDtypeStruct((bsz, width, t_all), BF16))

    def v_spec(nh):
        return (pl.BlockSpec((1, nh // 2, TM, 2 * LANES), lambda t, b: (b, 0, t, 0)),
                jax.ShapeDtypeStruct((bsz, nh // 2, t_all, 2 * LANES), BF16))

    outs = []
    for wq, wk, n_kv in BRANCH_LAYOUT:
        outs += [q_spec(wq), kt_spec(wk), v_spec(n_kv)]
    return pl.pallas_call(
        _inproj_kernel,
        grid=(nt, bsz),
        in_specs=[
            pl.BlockSpec((1, TM, D_MODEL), lambda t, b: (b, t, 0)),
            _mod_spec(mods, layer, lambda t, b: jnp.where(t == ctx_tile, ctx_row, b)),
            *[_layer_spec(w, layer) for w in weights],
            pl.BlockSpec((TM, tables.shape[1]), lambda t, b: (t, 0)),
        ],
        out_specs=[o[0] for o in outs],
        out_shape=[o[1] for o in outs],
        compiler_params=_cparams(("arbitrary", "arbitrary")),
        name="inproj",
    )(x_all, mods, *weights, tables)


def _softmax_pv(q, kt, v, half, bias=None):
    s = _dot(q, kt)
    if bias is not None:
        s = s + bias
    p = jnp.exp2(s - jnp.max(s, axis=-1, keepdims=True)).astype(BF16)
    o = _dot(p, v)[:, half * LANES:(half + 1) * LANES]
    return o[:, :HEAD_DIM] * (1.0 / o[:, HEAD_DIM:HEAD_DIM + 1])


def _attn_kernel(seq, lambda_init, qna_ref, kna_ref, vna_ref, qdf_ref, kdf_ref, vdf_ref,
                 qgq_ref, kgq_ref, vgq_ref, qml_ref, kml_ref, vml_ref, bias_ref, lam_ref, sub_ref, o_ref):
    bias_ref, lam_ref, sub_ref = bias_ref.at[0], lam_ref.at[0], sub_ref.at[0]
    qi = pl.program_id(1)
    t_all = vna_ref.shape[2]
    n_lat_tiles = seq // TM
    lp = lam_ref[...]
    lam = (jnp.exp(jnp.sum(lp[0:1] * lp[1:2], axis=-1, keepdims=True))
           - jnp.exp(jnp.sum(lp[2:3] * lp[3:4], axis=-1, keepdims=True)) + lambda_init)
    lane = lax.broadcasted_iota(jnp.int32, (TM, LANES), 1)

    def q_lanes(q_ref, tile, first, size):
        shift = size.bit_length() - 1
        qf = q_ref[0, :, tile * LANES:(tile + 1) * LANES].astype(F32)
        return jnp.where(lax.shift_right_logical(lane, shift) == (first >> shift), qf, 0.0).astype(BF16)

    def tile_rows(tile):
        return slice(tile * LANES, (tile + 1) * LANES)

    def paired_heads(q_ref, kt_ref, v_ref, k_lo):
        return [_softmax_pv(q_lanes(q_ref, h // 2, HEAD_DIM * (h % 2), HEAD_DIM),
                            kt_ref[0, tile_rows(h // 2), k_lo:t_all], v_ref[0, h // 2, k_lo:t_all, :], h % 2)
                for h in range(N_HEADS)]

    def grouped_heads(k_lo):
        outs = []
        for h in range(N_HEADS):
            g = h // (N_HEADS // GQA_KV_HEADS)
            outs.append(_softmax_pv(q_lanes(qgq_ref, h % 2, HEAD_DIM * g, HEAD_DIM),
                                    kgq_ref[0, :, k_lo:t_all], vgq_ref[0, 0, k_lo:t_all, :], g))
        return outs

    def latent_heads(k_lo):
        return [_softmax_pv(qml_ref[0, :, tile_rows(h)], kml_ref[0, tile_rows(h), k_lo:t_all],
                            vml_ref[0, h // 2, k_lo:t_all, :], h % 2) for h in range(N_HEADS)]

    def diff_heads(k_lo):
        outs = []
        for h in range(N_HEADS):
            kt = kdf_ref[0, tile_rows(h // 2), k_lo:t_all]
            v = vdf_ref[0, h // 2, k_lo:t_all, :]
            first = HEAD_DIM * (h % 2)
            o = (_softmax_pv(q_lanes(qdf_ref, h // 2, first, DIFF_DH), kt, v, h % 2)
                 - lam * _softmax_pv(q_lanes(qdf_ref, h // 2, first + DIFF_DH, DIFF_DH), kt, v, h % 2))
            ms = jnp.mean(o * o, axis=-1, keepdims=True)
            outs.append(o * lax.rsqrt(ms + EPS) * sub_ref[...] * (1.0 - lambda_init))
        return outs

    def band_heads():
        ws = pl.multiple_of(jnp.clip(NA_ROWS_PER_TILE * qi - NA_WIN_R // 2, 0,
                                     seq // GRID_W - NA_WIN_ROWS) * GRID_W, TM)
        n_win = NA_WIN_ROWS * GRID_W
        outs = []
        for h in range(N_HEADS):
            rows = tile_rows(h // 2)
            kt = jnp.concatenate([kna_ref[0, rows, pl.ds(ws, n_win)], kna_ref[0, rows, seq:t_all]], axis=1)
            v = jnp.concatenate([vna_ref[0, h // 2, pl.ds(ws, n_win), :], vna_ref[0, h // 2, seq:t_all, :]], axis=0)
            outs.append(_softmax_pv(q_lanes(qna_ref, h // 2, HEAD_DIM * (h % 2), HEAD_DIM), kt, v, h % 2,
                                    bias_ref[0, h]))
        return outs

    def run(k_lo, latent):
        outs = band_heads() if latent else paired_heads(qna_ref, kna_ref, vna_ref, k_lo)
        outs += diff_heads(k_lo)
        outs += grouped_heads(k_lo)
        outs += latent_heads(k_lo)
        o_ref[0] = jnp.concatenate(outs, axis=1).astype(o_ref.dtype)

    @pl.when(qi < n_lat_tiles)
    def _():
        run(0, True)

    @pl.when(qi >= n_lat_tiles)
    def _():
        run(seq, False)


def _attention(qkv, bias, lam_p, sub_g, seq, n_q_tiles, lambda_init, layer):
    bsz, t_all, _ = qkv[0].shape
    n_lat_tiles = seq // TM
    n_cls = bias.shape[1]
    in_specs = []
    for wq, wk, n_kv in BRANCH_LAYOUT:
        in_specs += [
            pl.BlockSpec((1, TM, wq), lambda b, i: (b, i, 0)),
            pl.BlockSpec((1, wk, t_all), lambda b, i: (b, 0, 0)),
            pl.BlockSpec((1, n_kv // 2, t_all, 2 * LANES), lambda b, i: (b, 0, 0, 0)),
        ]
    in_specs += [
        pl.BlockSpec((1, 1) + bias.shape[2:],
                     lambda b, i: (layer, jnp.where(i == 0, 0, jnp.where(i >= n_lat_tiles - 1, n_cls - 1, 1)), 0, 0, 0)),
        _layer_spec(lam_p, layer),
        _layer_spec(sub_g, layer),
    ]
    width = 4 * N_HEADS * HEAD_DIM
    return pl.pallas_call(
        functools.partial(_attn_kernel, seq, lambda_init),
        grid=(bsz, n_q_tiles),
        in_specs=in_specs,
        out_specs=pl.BlockSpec((1, TM, width), lambda b, i: (b, i, 0)),
        out_shape=jax.ShapeDtypeStruct((bsz, t_all, width), BF16),
        compiler_params=_cparams(("arbitrary", "arbitrary")),
        name="attn",
    )(*qkv, bias, lam_p, sub_g)


def _merge_kernel(x_ref, mod_ref, g1_ref, y_ref, wg_ref, wb_ref, wo_ref, o_ref):
    g1_ref, wg_ref, wb_ref, wo_ref = (r.at[0] for r in (g1_ref, wg_ref, wb_ref, wo_ref))
    x = x_ref[0]
    mod = mod_ref[0, 0]
    h = (_row_rmsnorm(x, g1_ref[...]) * (1.0 + mod[1:2]) + mod[0:1]).astype(BF16)
    acc = None
    width = wb_ref.shape[1]
    for i in range(wb_ref.shape[0]):
        term = _sigmoid(_dot(h, wg_ref[i])) * _dot(y_ref[0, :, i * width:(i + 1) * width], wb_ref[i])
        acc = term if acc is None else acc + term
    o_ref[0] = x + mod[2:3] * _dot(acc.astype(BF16), wo_ref[...])


def _merge(x_all, mods, y, lw, n_tiles, layer):
    bsz, t_all, _ = x_all.shape
    ctx_tile = t_all // TM - 1
    ctx_row = mods.shape[1] - HALO
    tok = lambda w: pl.BlockSpec((1, TM, w), lambda b, t: (b, t, 0))
    return pl.pallas_call(
        _merge_kernel,
        grid=(bsz, n_tiles),
        in_specs=[
            tok(D_MODEL),
            _mod_spec(mods, layer, lambda b, t: jnp.where(t == ctx_tile, ctx_row, b)),
            _layer_spec(lw["g1"], layer), tok(y.shape[2]),
            _layer_spec(lw["w_gate"], layer), _layer_spec(lw["w_branch"], layer), _layer_spec(lw["w_out"], layer),
        ],
        out_specs=tok(D_MODEL),
        out_shape=jax.ShapeDtypeStruct(x_all.shape, F32),
        input_output_aliases={0: 0},
        compiler_params=_cparams(("arbitrary", "arbitrary")),
        name="merge",
    )(x_all, mods, lw["g1"], y, lw["w_gate"], lw["w_branch"], lw["w_out"])


def _ffn_kernel(seq, t_all, row0, x_ref, xp_ref, xn_ref, mod_ref, g2_ref, wu_ref, cw_ref, wd_ref, *rest):
    o_ref, act_scr = rest[-2:]
    g2_ref, wu_ref, cw_ref, wd_ref = (r.at[0] for r in (g2_ref, wu_ref, cw_ref, wd_ref))
    tm = x_ref.shape[1]
    t = pl.program_id(1)
    x = x_ref[0]
    mod = mod_ref[0, 0]
    x_ext = jnp.concatenate([x, xp_ref[0], xn_ref[0]], axis=0)
    hx = (_row_rmsnorm(x_ext, g2_ref[...]) * (1.0 + mod[4:5]) + mod[3:4]).astype(BF16)

    row = lax.broadcasted_iota(jnp.int32, (tm, FF_CHUNK), 0)
    g_first = row0 + t * tm
    has_prev = jnp.logical_and(g_first != 0, g_first != seq)
    has_next = jnp.logical_and(g_first + tm != seq, g_first + tm != t_all)
    first_row = row == 0
    last_row = row == tm - 1

    def conv(lo):
        u_ext = _dot(hx, wu_ref[:, lo:lo + FF_CHUNK])
        c = cw_ref[:, lo:lo + FF_CHUNK]
        u = u_ext[0:tm]
        prev_row = jnp.where(has_prev, u_ext[tm + HALO - 1:tm + HALO], 0.0)
        next_row = jnp.where(has_next, u_ext[tm + HALO:tm + HALO + 1], 0.0)
        up = jnp.where(first_row, prev_row, pltpu.roll(u, 1, 0))
        dn = jnp.where(last_row, next_row, pltpu.roll(u, tm - 1, 0))
        return c[3:4] + up * c[0:1] + u * c[1:2] + dn * c[2:3]

    for j in range(D_FF // FF_CHUNK):
        lo = j * FF_CHUNK
        gate = conv(lo)
        val = conv(D_FF + lo)
        act_scr[:, lo:lo + FF_CHUNK] = (gate * _sigmoid(gate) * val).astype(BF16)
    o_ref[0] = x + mod[5:6] * _dot(act_scr[...], wd_ref[...])


def _ffn_rows(x_all, mods, lw, seq, layer, tm, row0, n_tiles, out_rows, prev_out=None):
    bsz, t_all, _ = x_all.shape
    ctx_row = mods.shape[1] - HALO
    tile0 = row0 // tm
    blocks_per_tile = tm // HALO
    last_block = t_all // HALO - 1
    weights = (lw["g2"], lw["w_up"], lw["conv"], lw["w_down"])
    mod_row = (lambda b, t: ctx_row) if row0 >= seq else (lambda b, t: b)
    extra = () if prev_out is None else (prev_out,)
    return pl.pallas_call(
        functools.partial(_ffn_kernel, seq, t_all, row0),
        grid=(bsz, n_tiles),
        in_specs=[
            pl.BlockSpec((1, tm, D_MODEL), lambda b, t: (b, tile0 + t, 0)),
            pl.BlockSpec((1, HALO, D_MODEL), lambda b, t: (b, jnp.maximum((tile0 + t) * blocks_per_tile - 1, 0), 0)),
            pl.BlockSpec((1, HALO, D_MODEL),
                         lambda b, t: (b, jnp.minimum((tile0 + t + 1) * blocks_per_tile, last_block), 0)),
            _mod_spec(mods, layer, mod_row),
            *[_layer_spec(w, layer) for w in weights],
            *[pl.BlockSpec(memory_space=pl.ANY) for _ in extra],
        ],
        out_specs=pl.BlockSpec((1, tm, D_MODEL), lambda b, t: (b, tile0 + t, 0)),
        out_shape=jax.ShapeDtypeStruct((bsz, out_rows, D_MODEL), F32),
        input_output_aliases={} if prev_out is None else {4 + len(weights): 0},
        scratch_shapes=[pltpu.VMEM((tm, D_FF), BF16)],
        compiler_params=_cparams(("arbitrary", "arbitrary")),
        name="ffn",
    )(x_all, x_all, x_all, mods, *weights, *extra)


def _ffn(x_all, mods, lw, seq, with_ctx, layer):
    t_all = x_all.shape[1]
    out = _ffn_rows(x_all, mods, lw, seq, layer, FFN_TM, 0, seq // FFN_TM, t_all if with_ctx else seq)
    if with_ctx:
        out = _ffn_rows(x_all, mods, lw, seq, layer, TM, seq, (t_all - seq) // TM, t_all, prev_out=out)
    return out


def _pad_heads(w, n_heads, d):
    lead = w.shape[:-1]
    w = w.reshape(lead + (n_heads, d))
    w = jnp.pad(w, [(0, 0)] * len(lead) + [(0, 0), (0, LANES - d)])
    return w.reshape(lead + (n_heads * LANES,))


def _pad_last(a, before, after):
    return jnp.pad(a, [(0, 0)] * (a.ndim - 1) + [(before, after)])


def _pad_lanes(v, width):
    return _pad_last(v, 0, width - v.shape[-1])


def _swap_halves(w, lane0, used, half):
    lead = w.shape[:-1]
    tiles = w.reshape(lead + (w.shape[-1] // LANES, LANES))
    grp = tiles[..., lane0:lane0 + used].reshape(lead + (tiles.shape[-2], used // (2 * half), 2, half))
    grp = grp[..., ::-1, :].reshape(lead + (tiles.shape[-2], used))
    out = jnp.pad(grp, [(0, 0)] * (len(lead) + 1) + [(lane0, LANES - lane0 - used)])
    return out.reshape(w.shape)


def _rope_table(n_lat, n_ctx, rot_dim, lane0, lanes_used):
    t = jnp.arange(n_lat)
    row = (t // GRID_W).astype(F32)
    col = (t % GRID_W).astype(F32)
    n_axis = rot_dim // 4
    inv_freq = ROPE_THETA ** (-jnp.arange(n_axis, dtype=F32) / n_axis)
    ang = jnp.concatenate([row[:, None] * inv_freq, col[:, None] * inv_freq], axis=-1)
    cos_g = jnp.concatenate([jnp.cos(ang), jnp.cos(ang)], axis=-1)
    sin_g = jnp.concatenate([-jnp.sin(ang), jnp.sin(ang)], axis=-1)
    reps = lanes_used // rot_dim
    cos = jnp.ones((n_lat, LANES), F32).at[:, lane0:lane0 + lanes_used].set(jnp.tile(cos_g, (1, reps)))
    sin = jnp.zeros((n_lat, LANES), F32).at[:, lane0:lane0 + lanes_used].set(jnp.tile(sin_g, (1, reps)))
    cos = jnp.concatenate([cos, jnp.ones((n_ctx, LANES), F32)], axis=0)
    sin = jnp.concatenate([sin, jnp.zeros((n_ctx, LANES), F32)], axis=0)
    return cos, sin


def _rope_tables(n_lat, n_ctx):
    cd, sd = _rope_table(n_lat, n_ctx, DIFF_DH, 0, LANES)
    cg, sg = _rope_table(n_lat, n_ctx, HEAD_DIM, 0, LANES)
    cm, sm = _rope_table(n_lat, n_ctx, MLA_ROPE, MLA_NOPE, MLA_ROPE)
    return jnp.concatenate([cd, sd, cg, sg, cm, sm], axis=1)


def _na_bias_tables(rel_bias, rows, n_ctx):
    cq = jnp.arange(GRID_W)
    col_start = jnp.clip(cq - NA_WIN_C // 2, 0, GRID_W - NA_WIN_C)
    ck = jnp.arange(GRID_W)
    col_in = (ck[None, :] >= col_start[:, None]) & (ck[None, :] < col_start[:, None] + NA_WIN_C)
    col_idx = jnp.clip(ck[None, :] - cq[:, None], -(NA_WIN_C - 1), NA_WIN_C - 1) + NA_WIN_C - 1
    col_sel = jax.nn.one_hot(col_idx.reshape(-1), 2 * NA_WIN_C - 1, dtype=F32)
    tables = []
    for r0 in (0, NA_ROWS_PER_TILE, rows - NA_ROWS_PER_TILE):
        ws = min(max(r0 - NA_WIN_R // 2, 0), rows - NA_WIN_ROWS)
        r = r0 + jnp.arange(NA_ROWS_PER_TILE)
        kr = ws + jnp.arange(NA_WIN_ROWS)
        rs = jnp.clip(r - NA_WIN_R // 2, 0, rows - NA_WIN_R)
        in_band = (kr[None, :] >= rs[:, None]) & (kr[None, :] < rs[:, None] + NA_WIN_R)
        row_idx = jnp.clip(kr[None, :] - r[:, None] + NA_WIN_R - 1, 0, 2 * NA_WIN_R - 2)
        row_sel = jax.nn.one_hot(row_idx.reshape(-1), 2 * NA_WIN_R - 1, dtype=F32)
        b = jnp.einsum("pa,lhac,qc->lhpq", row_sel, rel_bias, col_sel, precision=lax.Precision.HIGHEST)
        b = b.reshape(rel_bias.shape[:2] + (NA_ROWS_PER_TILE, NA_WIN_ROWS, GRID_W, GRID_W)).transpose(0, 1, 2, 4, 3, 5)
        ok = in_band[:, None, :, None] & col_in[None, :, None, :]
        b = jnp.where(ok[None, None], b * LOG2E, NEG_INF)
        b = b.reshape(rel_bias.shape[:2] + (TM, NA_WIN_ROWS * GRID_W))
        tables.append(jnp.concatenate([b, jnp.zeros(rel_bias.shape[:2] + (TM, n_ctx), F32)], axis=-1))
    return jnp.stack(tables, axis=1)


def _prep_weights(p):
    w_in = p["w_in"]
    depth = w_in.shape[0]
    offs = [0]
    for n in IN_SIZES:
        offs.append(offs[-1] + n)
    seg = [w_in[..., offs[j]:offs[j + 1]] for j in range(len(IN_SIZES))]
    kr_cols = _pad_last(seg[11], MLA_NOPE, LANES - MLA_NOPE - MLA_ROPE)
    rot_d = functools.partial(_swap_halves, lane0=0, used=LANES, half=DIFF_DH // 2)
    rot_g = functools.partial(_swap_halves, lane0=0, used=LANES, half=HEAD_DIM // 2)
    rot_m = functools.partial(_swap_halves, lane0=MLA_NOPE, used=MLA_ROPE, half=MLA_ROPE // 2)
    df_q, df_k, gq_k = seg[3], seg[4], seg[7]
    gq_q = seg[6].reshape(depth, D_MODEL, 2, 2, HEAD_DIM).swapaxes(2, 3).reshape(depth, D_MODEL, N_HEADS * HEAD_DIM)
    w_all = jnp.concatenate(
        [seg[0], seg[1], _pad_heads(seg[2], 4, 64),
         df_q, df_k, _pad_heads(seg[5], 4, 64), gq_q, gq_k, rot_g(gq_k), _pad_heads(seg[8], 2, 64),
         _pad_last(seg[9], 0, 256 - MLA_Q_RANK), seg[10], kr_cols,
         rot_d(df_q), rot_d(df_k), rot_g(gq_q), rot_m(kr_cols), jnp.zeros_like(kr_cols)], axis=-1).astype(BF16)

    w_qb = jnp.pad(_pad_heads(p["w_mla_qb"], N_HEADS, MLA_QK), ((0, 0), (0, 256 - MLA_Q_RANK), (0, 0)))
    w_qb = jnp.concatenate([w_qb, rot_m(w_qb)], axis=-1).astype(BF16)
    kvb = p["w_mla_kvb"].reshape(depth, MLA_KV_RANK, N_HEADS, MLA_NOPE + MLA_V)
    w_kvb = jnp.concatenate(
        [_pad_heads(kvb[..., :MLA_NOPE].reshape(depth, MLA_KV_RANK, -1), N_HEADS, MLA_NOPE),
         _pad_heads(kvb[..., MLA_NOPE:].reshape(depth, MLA_KV_RANK, -1), N_HEADS, MLA_V)], axis=-1).astype(BF16)

    def head_gain(g, n_heads, scale=1.0):
        return _pad_lanes(jnp.tile(_pad_lanes(g * scale, LANES), (1, n_heads)), 512)

    def dense_gain(g, reps, scale=1.0):
        return _pad_lanes(jnp.tile(g * scale, (1, reps)), 512)

    dg = p["diff_qk_g"]
    mq = p["mla_qk_g"]
    rows = [
        dense_gain(p["na_qk_g"][:, 0], 4, HEAD_DIM ** -0.5 * LOG2E), dense_gain(p["na_qk_g"][:, 1], 4),
        dense_gain(dg[:, 0], 8, DIFF_DH ** -0.5 * LOG2E), dense_gain(dg[:, 1], 8),
        dense_gain(p["gqa_qk_g"][:, 0], 4, HEAD_DIM ** -0.5 * LOG2E), dense_gain(p["gqa_qk_g"][:, 1], 2),
        head_gain(mq[:, 0], 4, MLA_QK ** -0.5 * LOG2E), head_gain(_pad_lanes(mq[:, 1, :MLA_NOPE], MLA_QK), 4),
        _pad_lanes(p["mla_qa_g"], 512), _pad_lanes(p["mla_kva_g"], 512),
        _pad_lanes(_pad_last(mq[:, 1, MLA_NOPE:], MLA_NOPE, 0), 512),
    ]
    rows += [rot_d(rows[2]), rot_d(rows[3]), rot_g(rows[4]), rot_g(rows[5]), rot_m(rows[6]), rot_m(rows[10])]
    gvec = jnp.stack(rows + [jnp.zeros((depth, 512), F32)] * (24 - len(rows)), axis=1)
    conv = jnp.concatenate([p["conv_w"], p["conv_b"][:, None], jnp.zeros((depth, 4, 2 * D_FF), F32)], axis=1)
    return dict(
        w_all=w_all, w_qb=w_qb, w_kvb=w_kvb, gvec=gvec,
        w_gate=p["w_gate"].astype(BF16), w_branch=p["w_branch"].astype(BF16), w_out=p["w_out"].astype(BF16),
        w_up=p["w_up"].astype(BF16), conv=conv, w_down=p["w_down"].astype(BF16),
        sub_g=p["diff_subln_g"][:, None], g1=p["norm1_g"][:, None], g2=p["norm2_g"][:, None],
    )


def kernel(x, c, ctx, c_ctx, w_mod, b_mod, norm1_g, norm2_g, w_in, na_qk_g, na_rel_bias, diff_qk_g, diff_lambda, diff_subln_g, gqa_qk_g, mla_qa_g, mla_kva_g, w_mla_qb, w_mla_kvb, mla_qk_g, w_gate, w_branch, w_out, w_up, conv_w, conv_b, w_down):
    p = dict(w_in=w_in, na_qk_g=na_qk_g, diff_qk_g=diff_qk_g, gqa_qk_g=gqa_qk_g, mla_qa_g=mla_qa_g,
             mla_kva_g=mla_kva_g, w_mla_qb=w_mla_qb, w_mla_kvb=w_mla_kvb, mla_qk_g=mla_qk_g, w_gate=w_gate,
             w_branch=w_branch, w_out=w_out, w_up=w_up, conv_w=conv_w, conv_b=conv_b, w_down=w_down,
             diff_subln_g=diff_subln_g, norm1_g=norm1_g, norm2_g=norm2_g)
    bsz, seq, d = x.shape
    n_ctx = ctx.shape[1]
    depth = w_mod.shape[0]
    assert d == D_MODEL and seq % TM == 0 and n_ctx == TM and seq // GRID_W >= NA_WIN_ROWS
    t_all = seq + n_ctx
    n_lat_tiles = seq // TM

    x_all = jnp.concatenate([x, ctx], axis=1)
    cc = jnp.concatenate([c, c_ctx[None], jnp.zeros((HALO - 1, d), F32)], axis=0)
    mods = _mod_all(cc, w_mod.astype(BF16), b_mod[:, None, :]).reshape(depth, bsz + HALO, N_MOD, d)
    tables = _rope_tables(seq, n_ctx)
    na_bias = _na_bias_tables(na_rel_bias, seq // GRID_W, n_ctx)
    lw = _prep_weights(p)

    for i in range(depth):
        last = i == depth - 1
        n_tiles = n_lat_tiles if last else n_lat_tiles + 1
        lambda_init = 0.8 - 0.6 * math.exp(-0.3 * i)
        qkv = _inproj(x_all, mods, lw, tables, i)
        y = _attention(qkv, na_bias, diff_lambda, lw["sub_g"], seq, n_tiles, lambda_init, i)
        x_all = _merge(x_all, mods, y, lw, n_tiles, i)
        x_all = _ffn(x_all, mods, lw, seq, not last, i)
    return x_all
```

```python
import functools
import math

import jax
import jax.numpy as jnp
from jax import lax
from jax.experimental import pallas as pl
from jax.experimental.pallas import tpu as pltpu

D_MODEL = 1024
GRID_W = 64
HEAD_DIM = 64
EPS = 1e-6
ROPE_THETA = 10000.0
NEG_INF = -1e30
NA_WIN_R = 8
NA_WIN_C = 16
DIFF_DH = 32
MLA_Q_RANK = 192
MLA_KV_RANK = 128
MLA_NOPE = 64
MLA_ROPE = 32
MLA_V = 64
MLA_QK = MLA_NOPE + MLA_ROPE
N_HEADS = 4
GQA_KV_HEADS = 2
N_MOD = 6
D_FF = 2816
IN_SIZES = (256, 256, 256, 256, 256, 256, 256, 128, 128, MLA_Q_RANK, MLA_KV_RANK, MLA_ROPE)

LANES = 128
MXU_DIM = 256
TM = 256
FFN_TM = 512
ROW_TM = 512
FF_CHUNK = 256
HALO = 8
LOG2E = math.log2(math.e)
BRANCH_LAYOUT = ((256, 256, 4), (256, 256, 4), (256, 128, 2), (512, 512, 4))
NA_ROWS_PER_TILE = TM // GRID_W
NA_WIN_ROWS = NA_WIN_R + NA_ROWS_PER_TILE
VMEM_LIMIT = 56 * 1024 * 1024

BF16 = jnp.bfloat16
F32 = jnp.float32


def _cparams(sem):
    return pltpu.CompilerParams(dimension_semantics=sem, vmem_limit_bytes=VMEM_LIMIT)


def _layer_spec(a, layer):
    nd = a.ndim
    return pl.BlockSpec((1,) + a.shape[1:], lambda *_: (layer,) + (0,) * (nd - 1))


def _mod_spec(mods, layer, row_of):
    return pl.BlockSpec((1, 1) + mods.shape[2:], lambda *g: (layer, row_of(*g), 0, 0))


def _dot(a, b):
    return jnp.dot(a, b, preferred_element_type=F32)


def _sigmoid(z):
    return 1.0 / (1.0 + jnp.exp(-z))


def _row_rmsnorm(x, g):
    ms = jnp.mean(x * x, axis=-1, keepdims=True)
    return x * lax.rsqrt(ms + EPS) * g


def _mod_kernel(c_ref, w_ref, b_ref, o_ref):
    c = c_ref[...]
    act = (c * _sigmoid(c)).astype(BF16)
    o_ref[0] = _dot(act, w_ref[0]) + b_ref[0]


def _mod_all(cc, w_mod, b_mod):
    n_layers, _, n_out = w_mod.shape
    rows = cc.shape[0]
    tn = 1024
    return pl.pallas_call(
        _mod_kernel,
        grid=(n_layers, n_out // tn),
        in_specs=[
            pl.BlockSpec((rows, D_MODEL), lambda l, j: (0, 0)),
            pl.BlockSpec((1, D_MODEL, tn), lambda l, j: (l, 0, j)),
            pl.BlockSpec((1, 1, tn), lambda l, j: (l, 0, j)),
        ],
        out_specs=pl.BlockSpec((1, rows, tn), lambda l, j: (l, 0, j)),
        out_shape=jax.ShapeDtypeStruct((n_layers, rows, n_out), F32),
        compiler_params=_cparams(("arbitrary", "arbitrary")),
        name="mod_all",
    )(cc, w_mod, b_mod)


def _head_rmsnorm(y, groups, y_rot=None, g=None, g_rot=None, cos=None, sin_signed=None):
    lane = lax.broadcasted_iota(jnp.int32, (1, LANES), 1)
    outs = []
    for c in range(y.shape[1] // LANES):
        yc = y[:, c * LANES:(c + 1) * LANES]
        sq = yc * yc
        if len(groups) == 1:
            ms = jnp.sum(sq, axis=-1, keepdims=True) * (1.0 / groups[0][1])
        else:
            ms = 0.0
            for first, size in groups:
                shift = size.bit_length() - 1
                member = lax.shift_right_logical(lane, shift) == (first >> shift)
                ms = jnp.where(member, jnp.sum(jnp.where(member, sq, 0.0), axis=-1, keepdims=True) * (1.0 / size), ms)
        r = lax.rsqrt(ms + EPS)
        outs.append(yc * r if y_rot is None else
                    (yc * r * g[:, c * LANES:(c + 1) * LANES]) * cos
                    + (y_rot[:, c * LANES:(c + 1) * LANES] * r * g_rot[:, c * LANES:(c + 1) * LANES]) * sin_signed)
    return outs[0] if len(outs) == 1 else jnp.concatenate(outs, axis=1)


def _store_q(ref, y):
    ref[0] = y.astype(ref.dtype)


def _store_kt(ref, y):
    ref[0] = y.T.astype(ref.dtype)


def _store_v(ref, y):
    lane = lax.broadcasted_iota(jnp.int32, (1, 2 * LANES), 1)
    ones_col = jnp.where(jnp.bitwise_and(lane, LANES - 1) == HEAD_DIM, 1.0, 0.0)
    for j in range(y.shape[1] // (2 * LANES)):
        ref[0, j] = (y[:, 2 * j * LANES:2 * (j + 1) * LANES] + ones_col).astype(ref.dtype)


def _inproj_kernel(x_ref, mod_ref, g1_ref, w_ref, wqb_ref, wkvb_ref, gv_ref, tab_ref, *rest):
    (qna_ref, kna_ref, vna_ref, qdf_ref, kdf_ref, vdf_ref,
     qgq_ref, kgq_ref, vgq_ref, qml_ref, kml_ref, vml_ref) = rest[-3 * len(BRANCH_LAYOUT):]
    g1_ref, w_ref, wqb_ref, wkvb_ref, gv_ref = (r.at[0] for r in (g1_ref, w_ref, wqb_ref, wkvb_ref, gv_ref))
    x = x_ref[0]
    mod = mod_ref[0, 0]
    h = (_row_rmsnorm(x, g1_ref[...]) * (1.0 + mod[1:2]) + mod[0:1]).astype(BF16)

    heads64 = ((0, 64), (64, 64))
    maps32 = ((0, 32), (32, 32), (64, 32), (96, 32))
    nope_only = ((0, MLA_NOPE),)
    nope_rope = ((0, MLA_NOPE), (MLA_NOPE, MLA_ROPE))
    rope_only = ((MLA_NOPE, MLA_ROPE),)
    tab = tab_ref[...]
    cos_d, sin_d = tab[:, 0:128], tab[:, 128:256]
    cos_g, sin_g = tab[:, 256:384], tab[:, 384:512]
    cos_m, sin_m = tab[:, 512:640], tab[:, 640:768]

    def proj(lo, width):
        return _dot(h, w_ref[:, lo:lo + width])

    def gain(row, width):
        return gv_ref[row:row + 1, :width]

    def normed(y, groups, row):
        return _head_rmsnorm(y, groups) * gain(row, y.shape[1])

    def roped(y, y_rot, groups, row, row_rot, cos, sin):
        w = y.shape[1]
        return _head_rmsnorm(y, groups, y_rot=y_rot, g=gain(row, w), g_rot=gain(row_rot, w), cos=cos, sin_signed=sin)

    _store_q(qna_ref, normed(proj(0, 256), heads64, 0))
    _store_kt(kna_ref, normed(proj(256, 256), heads64, 1))
    _store_v(vna_ref, proj(512, 512))
    _store_q(qdf_ref, roped(proj(1024, 256), proj(3328, 256), maps32, 2, 11, cos_d, sin_d))
    _store_kt(kdf_ref, roped(proj(1280, 256), proj(3584, 256), maps32, 3, 12, cos_d, sin_d))
    _store_v(vdf_ref, proj(1536, 512))
    _store_q(qgq_ref, roped(proj(2048, 256), proj(3840, 256), heads64, 4, 13, cos_g, sin_g))
    k_gq = proj(2304, 256)
    _store_kt(kgq_ref, roped(k_gq[:, :LANES], k_gq[:, LANES:], heads64, 5, 14, cos_g, sin_g))
    _store_v(vgq_ref, proj(2560, 256))
    qa = proj(2816, 256)
    ms_qa = jnp.sum(qa * qa, axis=-1, keepdims=True) * (1.0 / MLA_Q_RANK)
    qa_n = (qa * lax.rsqrt(ms_qa + EPS) * gain(8, 256)).astype(BF16)
    q_ml = _dot(qa_n, wqb_ref[:, :512])
    q_ml_rot = _dot(qa_n, wqb_ref[:, 512:])
    _store_q(qml_ref, roped(q_ml, q_ml_rot, nope_rope, 6, 15, cos_m, sin_m))
    kva_kr = proj(3072, 256)
    kva_n = _row_rmsnorm(kva_kr[:, :LANES], gain(9, 128)).astype(BF16)
    kv = _dot(kva_n, wkvb_ref[...])
    k_nope = normed(kv[:, :512], nope_only, 7)
    kr_rot = proj(4096, 256)[:, :LANES]
    kr_r = roped(kva_kr[:, LANES:], kr_rot, rope_only, 10, 16, cos_m, sin_m)
    _store_kt(kml_ref, k_nope + jnp.concatenate([kr_r] * N_HEADS, axis=1))
    _store_v(vml_ref, kv[:, 512:])


def _inproj_rows(x_all, mods, lw, tables, seq, layer, tm, row0, n_tiles, prev_outs=()):
    bsz, t_all, _ = x_all.shape
    ctx_row = mods.shape[1] - HALO
    tile0 = row0 // tm
    weights = (lw["g1"], lw["w_all"], lw["w_qb"], lw["w_kvb"], lw["gvec"])
    mod_row = (lambda t, b: ctx_row) if row0 >= seq else (lambda t, b: b)

    def q_spec(width):
        return (pl.BlockSpec((1, tm, width), lambda t, b: (b, tile0 + t, 0)),
                jax.ShapeDtypeStruct((bsz, t_all, width), BF16))

    def kt_spec(width):
        return (pl.BlockSpec((1, width, tm), lambda t, b: (b, 0, tile0 + t)),
                jax.ShapeDtypeStruct((bsz, width, t_all), BF16))

    def v_spec(nh):
        return (pl.BlockSpec((1, nh // 2, tm, 2 * LANES), lambda t, b: (b, 0, tile0 + t, 0)),
                jax.ShapeDtypeStruct((bsz, nh // 2, t_all, 2 * LANES), BF16))

    outs = []
    for wq, wk, n_kv in BRANCH_LAYOUT:
        outs += [q_spec(wq), kt_spec(wk), v_spec(n_kv)]
    n_in = 3 + len(weights)
    return pl.pallas_call(
        _inproj_kernel,
        grid=(n_tiles, bsz),
        in_specs=[
            pl.BlockSpec((1, tm, D_MODEL), lambda t, b: (b, tile0 + t, 0)),
            _mod_spec(mods, layer, mod_row),
            *[_layer_spec(w, layer) for w in weights],
            pl.BlockSpec((tm, tables.shape[1]), lambda t, b: (tile0 + t, 0)),
            *[pl.BlockSpec(memory_space=pl.ANY) for _ in prev_outs],
        ],
        out_specs=[o[0] for o in outs],
        out_shape=[o[1] for o in outs],
        input_output_aliases={n_in + j: j for j in range(len(prev_outs))},
        compiler_params=_cparams(("arbitrary", "arbitrary")),
        name="inproj",
    )(x_all, mods, *weights, tables, *prev_outs)


def _inproj(x_all, mods, lw, tables, seq, layer):
    t_all = x_all.shape[1]
    outs = _inproj_rows(x_all, mods, lw, tables, seq, layer, ROW_TM, 0, seq // ROW_TM)
    return _inproj_rows(x_all, mods, lw, tables, seq, layer, TM, seq, (t_all - seq) // TM, prev_outs=outs)


def _softmax_pv(q, kt, v, half, bias=None):
    s = _dot(q, kt)
    if bias is not None:
        s = s + bias
    p = jnp.exp2(s - jnp.max(s, axis=-1, keepdims=True)).astype(BF16)
    o = _dot(p, v)[:, half * LANES:(half + 1) * LANES]
    return o[:, :HEAD_DIM] * (1.0 / o[:, HEAD_DIM:HEAD_DIM + 1])


def _attn_kernel(seq, lambda_init, qna_ref, kna_ref, vna_ref, qdf_ref, kdf_ref, vdf_ref,
                 qgq_ref, kgq_ref, vgq_ref, qml_ref, kml_ref, vml_ref, bias_ref, lam_ref, sub_ref, o_ref):
    bias_ref, lam_ref, sub_ref = bias_ref.at[0], lam_ref.at[0], sub_ref.at[0]
    qi = pl.program_id(1)
    t_all = vna_ref.shape[2]
    n_lat_tiles = seq // TM
    lp = lam_ref[...]
    lam = (jnp.exp(jnp.sum(lp[0:1] * lp[1:2], axis=-1, keepdims=True))
           - jnp.exp(jnp.sum(lp[2:3] * lp[3:4], axis=-1, keepdims=True)) + lambda_init)
    lane = lax.broadcasted_iota(jnp.int32, (TM, LANES), 1)

    def q_lanes(q_ref, tile, first, size):
        shift = size.bit_length() - 1
        qf = q_ref[0, :, tile * LANES:(tile + 1) * LANES].astype(F32)
        return jnp.where(lax.shift_right_logical(lane, shift) == (first >> shift), qf, 0.0).astype(BF16)

    def tile_rows(tile):
        return slice(tile * LANES, (tile + 1) * LANES)

    def paired_heads(q_ref, kt_ref, v_ref, k_lo):
        return [_softmax_pv(q_lanes(q_ref, h // 2, HEAD_DIM * (h % 2), HEAD_DIM),
                            kt_ref[0, tile_rows(h // 2), k_lo:t_all], v_ref[0, h // 2, k_lo:t_all, :], h % 2)
                for h in range(N_HEADS)]

    def grouped_heads(k_lo):
        outs = []
        for h in range(N_HEADS):
            g = h // (N_HEADS // GQA_KV_HEADS)
            outs.append(_softmax_pv(q_lanes(qgq_ref, h % 2, HEAD_DIM * g, HEAD_DIM),
                                    kgq_ref[0, :, k_lo:t_all], vgq_ref[0, 0, k_lo:t_all, :], g))
        return outs

    def latent_heads(k_lo):
        return [_softmax_pv(qml_ref[0, :, tile_rows(h)], kml_ref[0, tile_rows(h), k_lo:t_all],
                            vml_ref[0, h // 2, k_lo:t_all, :], h % 2) for h in range(N_HEADS)]

    def diff_heads(k_lo):
        outs = []
        for h in range(N_HEADS):
            kt = kdf_ref[0, tile_rows(h // 2), k_lo:t_all]
            v = vdf_ref[0, h // 2, k_lo:t_all, :]
            first = HEAD_DIM * (h % 2)
            o = (_softmax_pv(q_lanes(qdf_ref, h // 2, first, DIFF_DH), kt, v, h % 2)
                 - lam * _softmax_pv(q_lanes(qdf_ref, h // 2, first + DIFF_DH, DIFF_DH), kt, v, h % 2))
            ms = jnp.mean(o * o, axis=-1, keepdims=True)
            outs.append(o * lax.rsqrt(ms + EPS) * sub_ref[...] * (1.0 - lambda_init))
        return outs

    def band_heads():
        ws = pl.multiple_of(jnp.clip(NA_ROWS_PER_TILE * qi - NA_WIN_R // 2, 0,
                                     seq // GRID_W - NA_WIN_ROWS) * GRID_W, TM)
        n_win = NA_WIN_ROWS * GRID_W
        outs = []
        for h in range(N_HEADS):
            rows = tile_rows(h // 2)
            kt = jnp.concatenate([kna_ref[0, rows, pl.ds(ws, n_win)], kna_ref[0, rows, seq:t_all]], axis=1)
            v = jnp.concatenate([vna_ref[0, h // 2, pl.ds(ws, n_win), :], vna_ref[0, h // 2, seq:t_all, :]], axis=0)
            outs.append(_softmax_pv(q_lanes(qna_ref, h // 2, HEAD_DIM * (h % 2), HEAD_DIM), kt, v, h % 2,
                                    bias_ref[0, h]))
        return outs

    def run(k_lo, latent):
        outs = band_heads() if latent else paired_heads(qna_ref, kna_ref, vna_ref, k_lo)
        outs += diff_heads(k_lo)
        outs += grouped_heads(k_lo)
        outs += latent_heads(k_lo)
        o_ref[0] = jnp.concatenate(outs, axis=1).astype(o_ref.dtype)

    @pl.when(qi < n_lat_tiles)
    def _():
        run(0, True)

    @pl.when(qi >= n_lat_tiles)
    def _():
        run(seq, False)


def _attention(qkv, bias, lam_p, sub_g, seq, n_q_tiles, lambda_init, layer):
    bsz, t_all, _ = qkv[0].shape
    n_lat_tiles = seq // TM
    n_cls = bias.shape[1]
    in_specs = []
    for wq, wk, n_kv in BRANCH_LAYOUT:
        in_specs += [
            pl.BlockSpec((1, TM, wq), lambda b, i: (b, i, 0)),
            pl.BlockSpec((1, wk, t_all), lambda b, i: (b, 0, 0)),
            pl.BlockSpec((1, n_kv // 2, t_all, 2 * LANES), lambda b, i: (b, 0, 0, 0)),
        ]
    in_specs += [
        pl.BlockSpec((1, 1) + bias.shape[2:],
                     lambda b, i: (layer, jnp.where(i == 0, 0, jnp.where(i >= n_lat_tiles - 1, n_cls - 1, 1)), 0, 0, 0)),
        _layer_spec(lam_p, layer),
        _layer_spec(sub_g, layer),
    ]
    width = 4 * N_HEADS * HEAD_DIM
    return pl.pallas_call(
        functools.partial(_attn_kernel, seq, lambda_init),
        grid=(bsz, n_q_tiles),
        in_specs=in_specs,
        out_specs=pl.BlockSpec((1, TM, width), lambda b, i: (b, i, 0)),
        out_shape=jax.ShapeDtypeStruct((bsz, t_all, width), BF16),
        compiler_params=_cparams(("arbitrary", "arbitrary")),
        name="attn",
    )(*qkv, bias, lam_p, sub_g)


def _merge_kernel(x_ref, mod_ref, g1_ref, y_ref, wg_ref, wb_ref, wo_ref, o_ref):
    g1_ref, wg_ref, wb_ref, wo_ref = (r.at[0] for r in (g1_ref, wg_ref, wb_ref, wo_ref))
    x = x_ref[0]
    mod = mod_ref[0, 0]
    h = (_row_rmsnorm(x, g1_ref[...]) * (1.0 + mod[1:2]) + mod[0:1]).astype(BF16)
    acc = None
    width = wb_ref.shape[1]
    for i in range(wb_ref.shape[0]):
        term = _sigmoid(_dot(h, wg_ref[i])) * _dot(y_ref[0, :, i * width:(i + 1) * width], wb_ref[i])
        acc = term if acc is None else acc + term
    o_ref[0] = x + mod[2:3] * _dot(acc.astype(BF16), wo_ref[...])


def _merge(x_all, mods, y, lw, n_tiles, layer):
    bsz, t_all, _ = x_all.shape
    ctx_tile = t_all // TM - 1
    ctx_row = mods.shape[1] - HALO
    tok = lambda w: pl.BlockSpec((1, TM, w), lambda b, t: (b, t, 0))
    return pl.pallas_call(
        _merge_kernel,
        grid=(bsz, n_tiles),
        in_specs=[
            tok(D_MODEL),
            _mod_spec(mods, layer, lambda b, t: jnp.where(t == ctx_tile, ctx_row, b)),
            _layer_spec(lw["g1"], layer), tok(y.shape[2]),
            _layer_spec(lw["w_gate"], layer), _layer_spec(lw["w_branch"], layer), _layer_spec(lw["w_out"], layer),
        ],
        out_specs=tok(D_MODEL),
        out_shape=jax.ShapeDtypeStruct(x_all.shape, F32),
        input_output_aliases={0: 0},
        compiler_params=_cparams(("arbitrary", "arbitrary")),
        name="merge",
    )(x_all, mods, lw["g1"], y, lw["w_gate"], lw["w_branch"], lw["w_out"])


def _ffn_kernel(seq, t_all, row0, x_ref, xp_ref, xn_ref, mod_ref, g2_ref, wu_ref, cw_ref, wd_ref, *rest):
    o_ref, act_scr = rest[-2:]
    g2_ref, wu_ref, cw_ref, wd_ref = (r.at[0] for r in (g2_ref, wu_ref, cw_ref, wd_ref))
    tm = x_ref.shape[1]
    t = pl.program_id(1)
    x = x_ref[0]
    mod = mod_ref[0, 0]
    x_ext = jnp.concatenate([x, xp_ref[0], xn_ref[0]], axis=0)
    hx = (_row_rmsnorm(x_ext, g2_ref[...]) * (1.0 + mod[4:5]) + mod[3:4]).astype(BF16)

    row = lax.broadcasted_iota(jnp.int32, (tm, FF_CHUNK), 0)
    g_first = row0 + t * tm
    has_prev = jnp.logical_and(g_first != 0, g_first != seq)
    has_next = jnp.logical_and(g_first + tm != seq, g_first + tm != t_all)
    first_row = row == 0
    last_row = row == tm - 1

    def conv(lo):
        u_ext = _dot(hx, wu_ref[:, lo:lo + FF_CHUNK])
        c = cw_ref[:, lo:lo + FF_CHUNK]
        u = u_ext[0:tm]
        prev_row = jnp.where(has_prev, u_ext[tm + HALO - 1:tm + HALO], 0.0)
        next_row = jnp.where(has_next, u_ext[tm + HALO:tm + HALO + 1], 0.0)
        up = jnp.where(first_row, prev_row, pltpu.roll(u, 1, 0))
        dn = jnp.where(last_row, next_row, pltpu.roll(u, tm - 1, 0))
        return c[3:4] + up * c[0:1] + u * c[1:2] + dn * c[2:3]

    for j in range(D_FF // FF_CHUNK):
        lo = j * FF_CHUNK
        gate = conv(lo)
        val = conv(D_FF + lo)
        act_scr[:, lo:lo + FF_CHUNK] = (gate * _sigmoid(gate) * val).astype(BF16)
    o_ref[0] = x + mod[5:6] * _dot(act_scr[...], wd_ref[...])


def _ffn_rows(x_all, mods, lw, seq, layer, tm, row0, n_tiles, out_rows, prev_out=None):
    bsz, t_all, _ = x_all.shape
    ctx_row = mods.shape[1] - HALO
    tile0 = row0 // tm
    blocks_per_tile = tm // HALO
    last_block = t_all // HALO - 1
    weights = (lw["g2"], lw["w_up"], lw["conv"], lw["w_down"])
    mod_row = (lambda b, t: ctx_row) if row0 >= seq else (lambda b, t: b)
    extra = () if prev_out is None else (prev_out,)
    return pl.pallas_call(
        functools.partial(_ffn_kernel, seq, t_all, row0),
        grid=(bsz, n_tiles),
        in_specs=[
            pl.BlockSpec((1, tm, D_MODEL), lambda b, t: (b, tile0 + t, 0)),
            pl.BlockSpec((1, HALO, D_MODEL), lambda b, t: (b, jnp.maximum((tile0 + t) * blocks_per_tile - 1, 0), 0)),
            pl.BlockSpec((1, HALO, D_MODEL),
                         lambda b, t: (b, jnp.minimum((tile0 + t + 1) * blocks_per_tile, last_block), 0)),
            _mod_spec(mods, layer, mod_row),
            *[_layer_spec(w, layer) for w in weights],
            *[pl.BlockSpec(memory_space=pl.ANY) for _ in extra],
        ],
        out_specs=pl.BlockSpec((1, tm, D_MODEL), lambda b, t: (b, tile0 + t, 0)),
        out_shape=jax.ShapeDtypeStruct((bsz, out_rows, D_MODEL), F32),
        input_output_aliases={} if prev_out is None else {4 + len(weights): 0},
        scratch_shapes=[pltpu.VMEM((tm, D_FF), BF16)],
        compiler_params=_cparams(("arbitrary", "arbitrary")),
        name="ffn",
    )(x_all, x_all, x_all, mods, *weights, *extra)


def _ffn(x_all, mods, lw, seq, with_ctx, layer):
    t_all = x_all.shape[1]
    out = _ffn_rows(x_all, mods, lw, seq, layer, FFN_TM, 0, seq // FFN_TM, t_all if with_ctx else seq)
    if with_ctx:
        out = _ffn_rows(x_all, mods, lw, seq, layer, TM, seq, (t_all - seq) // TM, t_all, prev_out=out)
    return out


def _pad_heads(w, n_heads, d):
    lead = w.shape[:-1]
    w = w.reshape(lead + (n_heads, d))
    w = jnp.pad(w, [(0, 0)] * len(lead) + [(0, 0), (0, LANES - d)])
    return w.reshape(lead + (n_heads * LANES,))


def _pad_last(a, before, after):
    return jnp.pad(a, [(0, 0)] * (a.ndim - 1) + [(before, after)])


def _pad_lanes(v, width):
    return _pad_last(v, 0, width - v.shape[-1])


def _swap_halves(w, lane0, used, half):
    lead = w.shape[:-1]
    tiles = w.reshape(lead + (w.shape[-1] // LANES, LANES))
    grp = tiles[..., lane0:lane0 + used].reshape(lead + (tiles.shape[-2], used // (2 * half), 2, half))
    grp = grp[..., ::-1, :].reshape(lead + (tiles.shape[-2], used))
    out = jnp.pad(grp, [(0, 0)] * (len(lead) + 1) + [(lane0, LANES - lane0 - used)])
    return out.reshape(w.shape)


def _rope_table(n_lat, n_ctx, rot_dim, lane0, lanes_used):
    t = jnp.arange(n_lat)
    row = (t // GRID_W).astype(F32)
    col = (t % GRID_W).astype(F32)
    n_axis = rot_dim // 4
    inv_freq = ROPE_THETA ** (-jnp.arange(n_axis, dtype=F32) / n_axis)
    ang = jnp.concatenate([row[:, None] * inv_freq, col[:, None] * inv_freq], axis=-1)
    cos_g = jnp.concatenate([jnp.cos(ang), jnp.cos(ang)], axis=-1)
    sin_g = jnp.concatenate([-jnp.sin(ang), jnp.sin(ang)], axis=-1)
    reps = lanes_used // rot_dim
    cos = jnp.ones((n_lat, LANES), F32).at[:, lane0:lane0 + lanes_used].set(jnp.tile(cos_g, (1, reps)))
    sin = jnp.zeros((n_lat, LANES), F32).at[:, lane0:lane0 + lanes_used].set(jnp.tile(sin_g, (1, reps)))
    cos = jnp.concatenate([cos, jnp.ones((n_ctx, LANES), F32)], axis=0)
    sin = jnp.concatenate([sin, jnp.zeros((n_ctx, LANES), F32)], axis=0)
    return cos, sin


def _rope_tables(n_lat, n_ctx):
    cd, sd = _rope_table(n_lat, n_ctx, DIFF_DH, 0, LANES)
    cg, sg = _rope_table(n_lat, n_ctx, HEAD_DIM, 0, LANES)
    cm, sm = _rope_table(n_lat, n_ctx, MLA_ROPE, MLA_NOPE, MLA_ROPE)
    return jnp.concatenate([cd, sd, cg, sg, cm, sm], axis=1)


def _na_bias_tables(rel_bias, rows, n_ctx):
    cq = jnp.arange(GRID_W)
    col_start = jnp.clip(cq - NA_WIN_C // 2, 0, GRID_W - NA_WIN_C)
    ck = jnp.arange(GRID_W)
    col_in = (ck[None, :] >= col_start[:, None]) & (ck[None, :] < col_start[:, None] + NA_WIN_C)
    col_idx = jnp.clip(ck[None, :] - cq[:, None], -(NA_WIN_C - 1), NA_WIN_C - 1) + NA_WIN_C - 1
    col_sel = jax.nn.one_hot(col_idx.reshape(-1), 2 * NA_WIN_C - 1, dtype=F32)
    tables = []
    for r0 in (0, NA_ROWS_PER_TILE, rows - NA_ROWS_PER_TILE):
        ws = min(max(r0 - NA_WIN_R // 2, 0), rows - NA_WIN_ROWS)
        r = r0 + jnp.arange(NA_ROWS_PER_TILE)
        kr = ws + jnp.arange(NA_WIN_ROWS)
        rs = jnp.clip(r - NA_WIN_R // 2, 0, rows - NA_WIN_R)
        in_band = (kr[None, :] >= rs[:, None]) & (kr[None, :] < rs[:, None] + NA_WIN_R)
        row_idx = jnp.clip(kr[None, :] - r[:, None] + NA_WIN_R - 1, 0, 2 * NA_WIN_R - 2)
        row_sel = jax.nn.one_hot(row_idx.reshape(-1), 2 * NA_WIN_R - 1, dtype=F32)
        b = jnp.einsum("pa,lhac,qc->lhpq", row_sel, rel_bias, col_sel, precision=lax.Precision.HIGHEST)
        b = b.reshape(rel_bias.shape[:2] + (NA_ROWS_PER_TILE, NA_WIN_ROWS, GRID_W, GRID_W)).transpose(0, 1, 2, 4, 3, 5)
        ok = in_band[:, None, :, None] & col_in[None, :, None, :]
        b = jnp.where(ok[None, None], b * LOG2E, NEG_INF)
        b = b.reshape(rel_bias.shape[:2] + (TM, NA_WIN_ROWS * GRID_W))
        tables.append(jnp.concatenate([b, jnp.zeros(rel_bias.shape[:2] + (TM, n_ctx), F32)], axis=-1))
    return jnp.stack(tables, axis=1)


def _prep_weights(p):
    w_in = p["w_in"]
    depth = w_in.shape[0]
    offs = [0]
    for n in IN_SIZES:
        offs.append(offs[-1] + n)
    seg = [w_in[..., offs[j]:offs[j + 1]] for j in range(len(IN_SIZES))]
    kr_cols = _pad_last(seg[11], MLA_NOPE, LANES - MLA_NOPE - MLA_ROPE)
    rot_d = functools.partial(_swap_halves, lane0=0, used=LANES, half=DIFF_DH // 2)
    rot_g = functools.partial(_swap_halves, lane0=0, used=LANES, half=HEAD_DIM // 2)
    rot_m = functools.partial(_swap_halves, lane0=MLA_NOPE, used=MLA_ROPE, half=MLA_ROPE // 2)
    df_q, df_k, gq_k = seg[3], seg[4], seg[7]
    gq_q = seg[6].reshape(depth, D_MODEL, 2, 2, HEAD_DIM).swapaxes(2, 3).reshape(depth, D_MODEL, N_HEADS * HEAD_DIM)
    w_all = jnp.concatenate(
        [seg[0], seg[1], _pad_heads(seg[2], 4, 64),
         df_q, df_k, _pad_heads(seg[5], 4, 64), gq_q, gq_k, rot_g(gq_k), _pad_heads(seg[8], 2, 64),
         _pad_last(seg[9], 0, 256 - MLA_Q_RANK), seg[10], kr_cols,
         rot_d(df_q), rot_d(df_k), rot_g(gq_q), rot_m(kr_cols), jnp.zeros_like(kr_cols)], axis=-1).astype(BF16)

    w_qb = jnp.pad(_pad_heads(p["w_mla_qb"], N_HEADS, MLA_QK), ((0, 0), (0, 256 - MLA_Q_RANK), (0, 0)))
    w_qb = jnp.concatenate([w_qb, rot_m(w_qb)], axis=-1).astype(BF16)
    kvb = p["w_mla_kvb"].reshape(depth, MLA_KV_RANK, N_HEADS, MLA_NOPE + MLA_V)
    w_kvb = jnp.concatenate(
        [_pad_heads(kvb[..., :MLA_NOPE].reshape(depth, MLA_KV_RANK, -1), N_HEADS, MLA_NOPE),
         _pad_heads(kvb[..., MLA_NOPE:].reshape(depth, MLA_KV_RANK, -1), N_HEADS, MLA_V)], axis=-1).astype(BF16)

    def head_gain(g, n_heads, scale=1.0):
        return _pad_lanes(jnp.tile(_pad_lanes(g * scale, LANES), (1, n_heads)), 512)

    def dense_gain(g, reps, scale=1.0):
        return _pad_lanes(jnp.tile(g * scale, (1, reps)), 512)

    dg = p["diff_qk_g"]
    mq = p["mla_qk_g"]
    rows = [
        dense_gain(p["na_qk_g"][:, 0], 4, HEAD_DIM ** -0.5 * LOG2E), dense_gain(p["na_qk_g"][:, 1], 4),
        dense_gain(dg[:, 0], 8, DIFF_DH ** -0.5 * LOG2E), dense_gain(dg[:, 1], 8),
        dense_gain(p["gqa_qk_g"][:, 0], 4, HEAD_DIM ** -0.5 * LOG2E), dense_gain(p["gqa_qk_g"][:, 1], 2),
        head_gain(mq[:, 0], 4, MLA_QK ** -0.5 * LOG2E), head_gain(_pad_lanes(mq[:, 1, :MLA_NOPE], MLA_QK), 4),
        _pad_lanes(p["mla_qa_g"], 512), _pad_lanes(p["mla_kva_g"], 512),
        _pad_lanes(_pad_last(mq[:, 1, MLA_NOPE:], MLA_NOPE, 0), 512),
    ]
    rows += [rot_d(rows[2]), rot_d(rows[3]), rot_g(rows[4]), rot_g(rows[5]), rot_m(rows[6]), rot_m(rows[10])]
    gvec = jnp.stack(rows + [jnp.zeros((depth, 512), F32)] * (24 - len(rows)), axis=1)
    conv = jnp.concatenate([p["conv_w"], p["conv_b"][:, None], jnp.zeros((depth, 4, 2 * D_FF), F32)], axis=1)
    return dict(
        w_all=w_all, w_qb=w_qb, w_kvb=w_kvb, gvec=gvec,
        w_gate=p["w_gate"].astype(BF16), w_branch=p["w_branch"].astype(BF16), w_out=p["w_out"].astype(BF16),
        w_up=p["w_up"].astype(BF16), conv=conv, w_down=p["w_down"].astype(BF16),
        sub_g=p["diff_subln_g"][:, None], g1=p["norm1_g"][:, None], g2=p["norm2_g"][:, None],
    )


def kernel(x, c, ctx, c_ctx, w_mod, b_mod, norm1_g, norm2_g, w_in, na_qk_g, na_rel_bias, diff_qk_g, diff_lambda, diff_subln_g, gqa_qk_g, mla_qa_g, mla_kva_g, w_mla_qb, w_mla_kvb, mla_qk_g, w_gate, w_branch, w_out, w_up, conv_w, conv_b, w_down):
    p = dict(w_in=w_in, na_qk_g=na_qk_g, diff_qk_g=diff_qk_g, gqa_qk_g=gqa_qk_g, mla_qa_g=mla_qa_g,
             mla_kva_g=mla_kva_g, w_mla_qb=w_mla_qb, w_mla_kvb=w_mla_kvb, mla_qk_g=mla_qk_g, w_gate=w_gate,
             w_branch=w_branch, w_out=w_out, w_up=w_up, conv_w=conv_w, conv_b=conv_b, w_down=w_down,
             diff_subln_g=diff_subln_g, norm1_g=norm1_g, norm2_g=norm2_g)
    bsz, seq, d = x.shape
    n_ctx = ctx.shape[1]
    depth = w_mod.shape[0]
    assert d == D_MODEL and seq % TM == 0 and n_ctx == TM and seq // GRID_W >= NA_WIN_ROWS
    t_all = seq + n_ctx
    n_lat_tiles = seq // TM

    x_all = jnp.concatenate([x, ctx], axis=1)
    cc = jnp.concatenate([c, c_ctx[None], jnp.zeros((HALO - 1, d), F32)], axis=0)
    mods = _mod_all(cc, w_mod.astype(BF16), b_mod[:, None, :]).reshape(depth, bsz + HALO, N_MOD, d)
    tables = _rope_tables(seq, n_ctx)
    na_bias = _na_bias_tables(na_rel_bias, seq // GRID_W, n_ctx)
    lw = _prep_weights(p)

    for i in range(depth):
        last = i == depth - 1
        n_tiles = n_lat_tiles if last else n_lat_tiles + 1
        lambda_init = 0.8 - 0.6 * math.exp(-0.3 * i)
        qkv = _inproj(x_all, mods, lw, tables, seq, i)
        y = _attention(qkv, na_bias, diff_lambda, lw["sub_g"], seq, n_tiles, lambda_init, i)
        x_all = _merge(x_all, mods, y, lw, n_tiles, i)
        x_all = _ffn(x_all, mods, lw, seq, not last, i)
    return x_all
```

```python
import functools
import math

import jax
import jax.numpy as jnp
from jax import lax
from jax.experimental import pallas as pl
from jax.experimental.pallas import tpu as pltpu

D_MODEL = 1024
GRID_W = 64
HEAD_DIM = 64
EPS = 1e-6
ROPE_THETA = 10000.0
NEG_INF = -1e30
NA_WIN_R = 8
NA_WIN_C = 16
DIFF_DH = 32
MLA_Q_RANK = 192
MLA_KV_RANK = 128
MLA_NOPE = 64
MLA_ROPE = 32
MLA_V = 64
MLA_QK = MLA_NOPE + MLA_ROPE
N_HEADS = 4
GQA_KV_HEADS = 2
N_MOD = 6
D_FF = 2816
IN_SIZES = (256, 256, 256, 256, 256, 256, 256, 128, 128, MLA_Q_RANK, MLA_KV_RANK, MLA_ROPE)

LANES = 128
MXU_DIM = 256
TM = 256
FFN_TM = 512
ROW_TM = 512
FF_CHUNK = 256
HALO = 8
LOG2E = math.log2(math.e)
BRANCH_LAYOUT = ((256, 256, 4), (256, 256, 4), (256, 128, 2), (512, 512, 4))
NA_ROWS_PER_TILE = TM // GRID_W
NA_WIN_ROWS = NA_WIN_R + NA_ROWS_PER_TILE
VMEM_LIMIT = 56 * 1024 * 1024

BF16 = jnp.bfloat16
F32 = jnp.float32


def _cparams(sem):
    return pltpu.CompilerParams(dimension_semantics=sem, vmem_limit_bytes=VMEM_LIMIT)


def _layer_spec(a, layer):
    nd = a.ndim
    return pl.BlockSpec((1,) + a.shape[1:], lambda *_: (layer,) + (0,) * (nd - 1))


def _mod_spec(mods, layer, row_of):
    return pl.BlockSpec((1, 1) + mods.shape[2:], lambda *g: (layer, row_of(*g), 0, 0))


def _dot(a, b):
    return jnp.dot(a, b, preferred_element_type=F32)


def _sigmoid(z):
    return 1.0 / (1.0 + jnp.exp(-z))


def _row_rmsnorm(x, g):
    ms = jnp.mean(x * x, axis=-1, keepdims=True)
    return x * lax.rsqrt(ms + EPS) * g


def _mod_kernel(c_ref, w_ref, b_ref, o_ref):
    c = c_ref[...]
    act = (c * _sigmoid(c)).astype(BF16)
    o_ref[0] = _dot(act, w_ref[0]) + b_ref[0]


def _mod_all(cc, w_mod, b_mod):
    n_layers, _, n_out = w_mod.shape
    rows = cc.shape[0]
    tn = 1024
    return pl.pallas_call(
        _mod_kernel,
        grid=(n_layers, n_out // tn),
        in_specs=[
            pl.BlockSpec((rows, D_MODEL), lambda l, j: (0, 0)),
            pl.BlockSpec((1, D_MODEL, tn), lambda l, j: (l, 0, j)),
            pl.BlockSpec((1, 1, tn), lambda l, j: (l, 0, j)),
        ],
        out_specs=pl.BlockSpec((1, rows, tn), lambda l, j: (l, 0, j)),
        out_shape=jax.ShapeDtypeStruct((n_layers, rows, n_out), F32),
        compiler_params=_cparams(("arbitrary", "arbitrary")),
        name="mod_all",
    )(cc, w_mod, b_mod)


def _head_rmsnorm(y, groups, y_rot=None, g=None, g_rot=None, cos=None, sin_signed=None):
    lane = lax.broadcasted_iota(jnp.int32, (1, LANES), 1)
    outs = []
    for c in range(y.shape[1] // LANES):
        yc = y[:, c * LANES:(c + 1) * LANES]
        sq = yc * yc
        if len(groups) == 1:
            ms = jnp.sum(sq, axis=-1, keepdims=True) * (1.0 / groups[0][1])
        else:
            ms = 0.0
            for first, size in groups:
                shift = size.bit_length() - 1
                member = lax.shift_right_logical(lane, shift) == (first >> shift)
                ms = jnp.where(member, jnp.sum(jnp.where(member, sq, 0.0), axis=-1, keepdims=True) * (1.0 / size), ms)
        r = lax.rsqrt(ms + EPS)
        outs.append(yc * r if y_rot is None else
                    (yc * r * g[:, c * LANES:(c + 1) * LANES]) * cos
                    + (y_rot[:, c * LANES:(c + 1) * LANES] * r * g_rot[:, c * LANES:(c + 1) * LANES]) * sin_signed)
    return outs[0] if len(outs) == 1 else jnp.concatenate(outs, axis=1)


def _store_q(ref, y):
    ref[0] = y.astype(ref.dtype)


def _store_kt(ref, y):
    ref[0] = y.T.astype(ref.dtype)


def _store_v(ref, y):
    lane = lax.broadcasted_iota(jnp.int32, (1, 2 * LANES), 1)
    ones_col = jnp.where(jnp.bitwise_and(lane, LANES - 1) == HEAD_DIM, 1.0, 0.0)
    for j in range(y.shape[1] // (2 * LANES)):
        ref[0, j] = (y[:, 2 * j * LANES:2 * (j + 1) * LANES] + ones_col).astype(ref.dtype)


def _inproj_kernel(x_ref, mod_ref, g1_ref, w_ref, wqb_ref, wkvb_ref, gv_ref, tab_ref, *rest):
    (qna_ref, kna_ref, vna_ref, qdf_ref, kdf_ref, vdf_ref,
     qgq_ref, kgq_ref, vgq_ref, qml_ref, kml_ref, vml_ref) = rest[-3 * len(BRANCH_LAYOUT):]
    g1_ref, w_ref, wqb_ref, wkvb_ref, gv_ref = (r.at[0] for r in (g1_ref, w_ref, wqb_ref, wkvb_ref, gv_ref))
    x = x_ref[0]
    mod = mod_ref[0, 0]
    h = (_row_rmsnorm(x, g1_ref[...]) * (1.0 + mod[1:2]) + mod[0:1]).astype(BF16)

    heads64 = ((0, 64), (64, 64))
    maps32 = ((0, 32), (32, 32), (64, 32), (96, 32))
    nope_only = ((0, MLA_NOPE),)
    nope_rope = ((0, MLA_NOPE), (MLA_NOPE, MLA_ROPE))
    rope_only = ((MLA_NOPE, MLA_ROPE),)
    tab = tab_ref[...]
    cos_d, sin_d = tab[:, 0:128], tab[:, 128:256]
    cos_g, sin_g = tab[:, 256:384], tab[:, 384:512]
    cos_m, sin_m = tab[:, 512:640], tab[:, 640:768]

    def proj(lo, width):
        return _dot(h, w_ref[:, lo:lo + width])

    def gain(row, width):
        return gv_ref[row:row + 1, :width]

    def normed(y, groups, row):
        return _head_rmsnorm(y, groups) * gain(row, y.shape[1])

    def roped(y, y_rot, groups, row, row_rot, cos, sin):
        w = y.shape[1]
        return _head_rmsnorm(y, groups, y_rot=y_rot, g=gain(row, w), g_rot=gain(row_rot, w), cos=cos, sin_signed=sin)

    _store_q(qna_ref, normed(proj(0, 256), heads64, 0))
    _store_kt(kna_ref, normed(proj(256, 256), heads64, 1))
    _store_v(vna_ref, proj(512, 512))
    _store_q(qdf_ref, roped(proj(1024, 256), proj(3328, 256), maps32, 2, 11, cos_d, sin_d))
    _store_kt(kdf_ref, roped(proj(1280, 256), proj(3584, 256), maps32, 3, 12, cos_d, sin_d))
    _store_v(vdf_ref, proj(1536, 512))
    _store_q(qgq_ref, roped(proj(2048, 256), proj(3840, 256), heads64, 4, 13, cos_g, sin_g))
    k_gq = proj(2304, 256)
    _store_kt(kgq_ref, roped(k_gq[:, :LANES], k_gq[:, LANES:], heads64, 5, 14, cos_g, sin_g))
    _store_v(vgq_ref, proj(2560, 256))
    qa = proj(2816, 256)
    ms_qa = jnp.sum(qa * qa, axis=-1, keepdims=True) * (1.0 / MLA_Q_RANK)
    qa_n = (qa * lax.rsqrt(ms_qa + EPS) * gain(8, 256)).astype(BF16)
    q_ml = _dot(qa_n, wqb_ref[:, :512])
    q_ml_rot = _dot(qa_n, wqb_ref[:, 512:])
    _store_q(qml_ref, roped(q_ml, q_ml_rot, nope_rope, 6, 15, cos_m, sin_m))
    kva_kr = proj(3072, 256)
    kva_n = _row_rmsnorm(kva_kr[:, :LANES], gain(9, 128)).astype(BF16)
    kv = _dot(kva_n, wkvb_ref[...])
    k_nope = normed(kv[:, :512], nope_only, 7)
    kr_rot = proj(4096, 256)[:, :LANES]
    kr_r = roped(kva_kr[:, LANES:], kr_rot, rope_only, 10, 16, cos_m, sin_m)
    _store_kt(kml_ref, k_nope + jnp.concatenate([kr_r] * N_HEADS, axis=1))
    _store_v(vml_ref, kv[:, 512:])


def _inproj_rows(src, src_row0, mods, lw, tables, seq, layer, tm, row0, n_tiles, prev_outs=()):
    bsz = src.shape[0]
    t_all = tables.shape[0]
    ctx_row = mods.shape[1] - HALO
    tile0 = row0 // tm
    src_tile0 = src_row0 // tm
    weights = (lw["g1"], lw["w_all"], lw["w_qb"], lw["w_kvb"], lw["gvec"])
    mod_row = (lambda t, b: ctx_row) if row0 >= seq else (lambda t, b: b)

    def q_spec(width):
        return (pl.BlockSpec((1, tm, width), lambda t, b: (b, tile0 + t, 0)),
                jax.ShapeDtypeStruct((bsz, t_all, width), BF16))

    def kt_spec(width):
        return (pl.BlockSpec((1, width, tm), lambda t, b: (b, 0, tile0 + t)),
                jax.ShapeDtypeStruct((bsz, width, t_all), BF16))

    def v_spec(nh):
        return (pl.BlockSpec((1, nh // 2, tm, 2 * LANES), lambda t, b: (b, 0, tile0 + t, 0)),
                jax.ShapeDtypeStruct((bsz, nh // 2, t_all, 2 * LANES), BF16))

    outs = []
    for wq, wk, n_kv in BRANCH_LAYOUT:
        outs += [q_spec(wq), kt_spec(wk), v_spec(n_kv)]
    n_in = 3 + len(weights)
    return pl.pallas_call(
        _inproj_kernel,
        grid=(n_tiles, bsz),
        in_specs=[
            pl.BlockSpec((1, tm, D_MODEL), lambda t, b: (b, src_tile0 + t, 0)),
            _mod_spec(mods, layer, mod_row),
            *[_layer_spec(w, layer) for w in weights],
            pl.BlockSpec((tm, tables.shape[1]), lambda t, b: (tile0 + t, 0)),
            *[pl.BlockSpec(memory_space=pl.ANY) for _ in prev_outs],
        ],
        out_specs=[o[0] for o in outs],
        out_shape=[o[1] for o in outs],
        input_output_aliases={n_in + j: j for j in range(len(prev_outs))},
        compiler_params=_cparams(("arbitrary", "arbitrary")),
        name="inproj",
    )(src, mods, *weights, tables, *prev_outs)


def _inproj(latent, context, mods, lw, tables, seq, layer):
    t_all = tables.shape[0]
    outs = _inproj_rows(*latent, mods, lw, tables, seq, layer, ROW_TM, 0, seq // ROW_TM)
    return _inproj_rows(*context, mods, lw, tables, seq, layer, TM, seq, (t_all - seq) // TM, prev_outs=outs)


def _softmax_pv(q, kt, v, half, bias=None):
    s = _dot(q, kt)
    if bias is not None:
        s = s + bias
    p = jnp.exp2(s - jnp.max(s, axis=-1, keepdims=True)).astype(BF16)
    o = _dot(p, v)[:, half * LANES:(half + 1) * LANES]
    return o[:, :HEAD_DIM] * (1.0 / o[:, HEAD_DIM:HEAD_DIM + 1])


def _attn_kernel(seq, lambda_init, qna_ref, kna_ref, vna_ref, qdf_ref, kdf_ref, vdf_ref,
                 qgq_ref, kgq_ref, vgq_ref, qml_ref, kml_ref, vml_ref, bias_ref, lam_ref, sub_ref, o_ref):
    bias_ref, lam_ref, sub_ref = bias_ref.at[0], lam_ref.at[0], sub_ref.at[0]
    qi = pl.program_id(1)
    t_all = vna_ref.shape[2]
    n_lat_tiles = seq // TM
    lp = lam_ref[...]
    lam = (jnp.exp(jnp.sum(lp[0:1] * lp[1:2], axis=-1, keepdims=True))
           - jnp.exp(jnp.sum(lp[2:3] * lp[3:4], axis=-1, keepdims=True)) + lambda_init)
    lane = lax.broadcasted_iota(jnp.int32, (TM, LANES), 1)

    def q_lanes(q_ref, tile, first, size):
        shift = size.bit_length() - 1
        qf = q_ref[0, :, tile * LANES:(tile + 1) * LANES].astype(F32)
        return jnp.where(lax.shift_right_logical(lane, shift) == (first >> shift), qf, 0.0).astype(BF16)

    def tile_rows(tile):
        return slice(tile * LANES, (tile + 1) * LANES)

    def paired_heads(q_ref, kt_ref, v_ref, k_lo):
        return [_softmax_pv(q_lanes(q_ref, h // 2, HEAD_DIM * (h % 2), HEAD_DIM),
                            kt_ref[0, tile_rows(h // 2), k_lo:t_all], v_ref[0, h // 2, k_lo:t_all, :], h % 2)
                for h in range(N_HEADS)]

    def grouped_heads(k_lo):
        outs = []
        for h in range(N_HEADS):
            g = h // (N_HEADS // GQA_KV_HEADS)
            outs.append(_softmax_pv(q_lanes(qgq_ref, h % 2, HEAD_DIM * g, HEAD_DIM),
                                    kgq_ref[0, :, k_lo:t_all], vgq_ref[0, 0, k_lo:t_all, :], g))
        return outs

    def latent_heads(k_lo):
        return [_softmax_pv(qml_ref[0, :, tile_rows(h)], kml_ref[0, tile_rows(h), k_lo:t_all],
                            vml_ref[0, h // 2, k_lo:t_all, :], h % 2) for h in range(N_HEADS)]

    def diff_heads(k_lo):
        outs = []
        for h in range(N_HEADS):
            kt = kdf_ref[0, tile_rows(h // 2), k_lo:t_all]
            v = vdf_ref[0, h // 2, k_lo:t_all, :]
            first = HEAD_DIM * (h % 2)
            o = (_softmax_pv(q_lanes(qdf_ref, h // 2, first, DIFF_DH), kt, v, h % 2)
                 - lam * _softmax_pv(q_lanes(qdf_ref, h // 2, first + DIFF_DH, DIFF_DH), kt, v, h % 2))
            ms = jnp.mean(o * o, axis=-1, keepdims=True)
            outs.append(o * lax.rsqrt(ms + EPS) * sub_ref[...] * (1.0 - lambda_init))
        return outs

    def band_heads():
        ws = pl.multiple_of(jnp.clip(NA_ROWS_PER_TILE * qi - NA_WIN_R // 2, 0,
                                     seq // GRID_W - NA_WIN_ROWS) * GRID_W, TM)
        n_win = NA_WIN_ROWS * GRID_W
        outs = []
        for h in range(N_HEADS):
            rows = tile_rows(h // 2)
            kt = jnp.concatenate([kna_ref[0, rows, pl.ds(ws, n_win)], kna_ref[0, rows, seq:t_all]], axis=1)
            v = jnp.concatenate([vna_ref[0, h // 2, pl.ds(ws, n_win), :], vna_ref[0, h // 2, seq:t_all, :]], axis=0)
            outs.append(_softmax_pv(q_lanes(qna_ref, h // 2, HEAD_DIM * (h % 2), HEAD_DIM), kt, v, h % 2,
                                    bias_ref[0, h]))
        return outs

    def run(k_lo, latent):
        outs = band_heads() if latent else paired_heads(qna_ref, kna_ref, vna_ref, k_lo)
        outs += diff_heads(k_lo)
        outs += grouped_heads(k_lo)
        outs += latent_heads(k_lo)
        o_ref[0] = jnp.concatenate(outs, axis=1).astype(o_ref.dtype)

    @pl.when(qi < n_lat_tiles)
    def _():
        run(0, True)

    @pl.when(qi >= n_lat_tiles)
    def _():
        run(seq, False)


def _attention(qkv, bias, lam_p, sub_g, seq, n_q_tiles, lambda_init, layer):
    bsz, t_all, _ = qkv[0].shape
    n_lat_tiles = seq // TM
    n_cls = bias.shape[1]
    in_specs = []
    for wq, wk, n_kv in BRANCH_LAYOUT:
        in_specs += [
            pl.BlockSpec((1, TM, wq), lambda b, i: (b, i, 0)),
            pl.BlockSpec((1, wk, t_all), lambda b, i: (b, 0, 0)),
            pl.BlockSpec((1, n_kv // 2, t_all, 2 * LANES), lambda b, i: (b, 0, 0, 0)),
        ]
    in_specs += [
        pl.BlockSpec((1, 1) + bias.shape[2:],
                     lambda b, i: (layer, jnp.where(i == 0, 0, jnp.where(i >= n_lat_tiles - 1, n_cls - 1, 1)), 0, 0, 0)),
        _layer_spec(lam_p, layer),
        _layer_spec(sub_g, layer),
    ]
    width = 4 * N_HEADS * HEAD_DIM
    return pl.pallas_call(
        functools.partial(_attn_kernel, seq, lambda_init),
        grid=(bsz, n_q_tiles),
        in_specs=in_specs,
        out_specs=pl.BlockSpec((1, TM, width), lambda b, i: (b, i, 0)),
        out_shape=jax.ShapeDtypeStruct((bsz, t_all, width), BF16),
        compiler_params=_cparams(("arbitrary", "arbitrary")),
        name="attn",
    )(*qkv, bias, lam_p, sub_g)


def _merge_kernel(x_ref, mod_ref, g1_ref, y_ref, wg_ref, wb_ref, wo_ref, *rest):
    o_ref = rest[-1]
    g1_ref, wg_ref, wb_ref, wo_ref = (r.at[0] for r in (g1_ref, wg_ref, wb_ref, wo_ref))
    x = x_ref[0]
    mod = mod_ref[0, 0]
    h = (_row_rmsnorm(x, g1_ref[...]) * (1.0 + mod[1:2]) + mod[0:1]).astype(BF16)
    acc = None
    width = wb_ref.shape[1]
    for i in range(wb_ref.shape[0]):
        term = _sigmoid(_dot(h, wg_ref[i])) * _dot(y_ref[0, :, i * width:(i + 1) * width], wb_ref[i])
        acc = term if acc is None else acc + term
    o_ref[0] = x + mod[2:3] * _dot(acc.astype(BF16), wo_ref[...])


def _merge(x_all, mods, y, lw, n_tiles, layer):
    bsz, t_all, _ = x_all.shape
    ctx_tile = t_all // TM - 1
    ctx_row = mods.shape[1] - HALO
    tok = lambda w: pl.BlockSpec((1, TM, w), lambda b, t: (b, t, 0))
    return pl.pallas_call(
        _merge_kernel,
        grid=(bsz, n_tiles),
        in_specs=[
            tok(D_MODEL),
            _mod_spec(mods, layer, lambda b, t: jnp.where(t == ctx_tile, ctx_row, b)),
            _layer_spec(lw["g1"], layer), tok(y.shape[2]),
            _layer_spec(lw["w_gate"], layer), _layer_spec(lw["w_branch"], layer), _layer_spec(lw["w_out"], layer),
        ],
        out_specs=tok(D_MODEL),
        out_shape=jax.ShapeDtypeStruct(x_all.shape, F32),
        input_output_aliases={0: 0},
        compiler_params=_cparams(("arbitrary", "arbitrary")),
        name="merge",
    )(x_all, mods, lw["g1"], y, lw["w_gate"], lw["w_branch"], lw["w_out"])


def _merge_first(x, ctx, mods, y, lw, layer):
    bsz, seq, _ = x.shape
    t_all = y.shape[1]
    ctx_row = mods.shape[1] - HALO
    weights = (lw["w_gate"], lw["w_branch"], lw["w_out"])

    def call(src, tile0, n_tiles, mod_row, prev):
        tok = lambda w, t0: pl.BlockSpec((1, TM, w), lambda b, t: (b, t0 + t, 0))
        return pl.pallas_call(
            _merge_kernel,
            grid=(bsz, n_tiles),
            in_specs=[
                tok(D_MODEL, 0), _mod_spec(mods, layer, mod_row), _layer_spec(lw["g1"], layer), tok(y.shape[2], tile0),
                *[_layer_spec(w, layer) for w in weights],
                *[pl.BlockSpec(memory_space=pl.ANY) for _ in prev],
            ],
            out_specs=tok(D_MODEL, tile0),
            out_shape=jax.ShapeDtypeStruct((bsz, t_all, D_MODEL), F32),
            input_output_aliases={4 + len(weights): 0} if prev else {},
            compiler_params=_cparams(("arbitrary", "arbitrary")),
            name="merge",
        )(src, mods, lw["g1"], y, *weights, *prev)

    out = call(x, 0, seq // TM, lambda b, t: b, ())
    return call(ctx, seq // TM, (t_all - seq) // TM, lambda b, t: ctx_row, (out,))


def _ffn_kernel(seq, t_all, row0, x_ref, xp_ref, xn_ref, mod_ref, g2_ref, wu_ref, cw_ref, wd_ref, *rest):
    o_ref, act_scr = rest[-2:]
    g2_ref, wu_ref, cw_ref, wd_ref = (r.at[0] for r in (g2_ref, wu_ref, cw_ref, wd_ref))
    tm = x_ref.shape[1]
    t = pl.program_id(1)
    x = x_ref[0]
    mod = mod_ref[0, 0]
    x_ext = jnp.concatenate([x, xp_ref[0], xn_ref[0]], axis=0)
    hx = (_row_rmsnorm(x_ext, g2_ref[...]) * (1.0 + mod[4:5]) + mod[3:4]).astype(BF16)

    row = lax.broadcasted_iota(jnp.int32, (tm, FF_CHUNK), 0)
    g_first = row0 + t * tm
    has_prev = jnp.logical_and(g_first != 0, g_first != seq)
    has_next = jnp.logical_and(g_first + tm != seq, g_first + tm != t_all)
    first_row = row == 0
    last_row = row == tm - 1

    def conv(lo):
        u_ext = _dot(hx, wu_ref[:, lo:lo + FF_CHUNK])
        c = cw_ref[:, lo:lo + FF_CHUNK]
        u = u_ext[0:tm]
        prev_row = jnp.where(has_prev, u_ext[tm + HALO - 1:tm + HALO], 0.0)
        next_row = jnp.where(has_next, u_ext[tm + HALO:tm + HALO + 1], 0.0)
        up = jnp.where(first_row, prev_row, pltpu.roll(u, 1, 0))
        dn = jnp.where(last_row, next_row, pltpu.roll(u, tm - 1, 0))
        return c[3:4] + up * c[0:1] + u * c[1:2] + dn * c[2:3]

    for j in range(D_FF // FF_CHUNK):
        lo = j * FF_CHUNK
        gate = conv(lo)
        val = conv(D_FF + lo)
        act_scr[:, lo:lo + FF_CHUNK] = (gate * _sigmoid(gate) * val).astype(BF16)
    o_ref[0] = x + mod[5:6] * _dot(act_scr[...], wd_ref[...])


def _ffn_rows(x_all, mods, lw, seq, layer, tm, row0, n_tiles, out_rows, prev_out=None):
    bsz, t_all, _ = x_all.shape
    ctx_row = mods.shape[1] - HALO
    tile0 = row0 // tm
    blocks_per_tile = tm // HALO
    last_block = t_all // HALO - 1
    weights = (lw["g2"], lw["w_up"], lw["conv"], lw["w_down"])
    mod_row = (lambda b, t: ctx_row) if row0 >= seq else (lambda b, t: b)
    extra = () if prev_out is None else (prev_out,)
    return pl.pallas_call(
        functools.partial(_ffn_kernel, seq, t_all, row0),
        grid=(bsz, n_tiles),
        in_specs=[
            pl.BlockSpec((1, tm, D_MODEL), lambda b, t: (b, tile0 + t, 0)),
            pl.BlockSpec((1, HALO, D_MODEL), lambda b, t: (b, jnp.maximum((tile0 + t) * blocks_per_tile - 1, 0), 0)),
            pl.BlockSpec((1, HALO, D_MODEL),
                         lambda b, t: (b, jnp.minimum((tile0 + t + 1) * blocks_per_tile, last_block), 0)),
            _mod_spec(mods, layer, mod_row),
            *[_layer_spec(w, layer) for w in weights],
            *[pl.BlockSpec(memory_space=pl.ANY) for _ in extra],
        ],
        out_specs=pl.BlockSpec((1, tm, D_MODEL), lambda b, t: (b, tile0 + t, 0)),
        out_shape=jax.ShapeDtypeStruct((bsz, out_rows, D_MODEL), F32),
        input_output_aliases={} if prev_out is None else {4 + len(weights): 0},
        scratch_shapes=[pltpu.VMEM((tm, D_FF), BF16)],
        compiler_params=_cparams(("arbitrary", "arbitrary")),
        name="ffn",
    )(x_all, x_all, x_all, mods, *weights, *extra)


def _ffn(x_all, mods, lw, seq, with_ctx, layer):
    t_all = x_all.shape[1]
    out = _ffn_rows(x_all, mods, lw, seq, layer, FFN_TM, 0, seq // FFN_TM, t_all if with_ctx else seq)
    if with_ctx:
        out = _ffn_rows(x_all, mods, lw, seq, layer, TM, seq, (t_all - seq) // TM, t_all, prev_out=out)
    return out


def _pad_heads(w, n_heads, d):
    lead = w.shape[:-1]
    w = w.reshape(lead + (n_heads, d))
    w = jnp.pad(w, [(0, 0)] * len(lead) + [(0, 0), (0, LANES - d)])
    return w.reshape(lead + (n_heads * LANES,))


def _pad_last(a, before, after):
    return jnp.pad(a, [(0, 0)] * (a.ndim - 1) + [(before, after)])


def _pad_lanes(v, width):
    return _pad_last(v, 0, width - v.shape[-1])


def _swap_halves(w, lane0, used, half):
    lead = w.shape[:-1]
    tiles = w.reshape(lead + (w.shape[-1] // LANES, LANES))
    grp = tiles[..., lane0:lane0 + used].reshape(lead + (tiles.shape[-2], used // (2 * half), 2, half))
    grp = grp[..., ::-1, :].reshape(lead + (tiles.shape[-2], used))
    out = jnp.pad(grp, [(0, 0)] * (len(lead) + 1) + [(lane0, LANES - lane0 - used)])
    return out.reshape(w.shape)


def _rope_table(n_lat, n_ctx, rot_dim, lane0, lanes_used):
    t = jnp.arange(n_lat)
    row = (t // GRID_W).astype(F32)
    col = (t % GRID_W).astype(F32)
    n_axis = rot_dim // 4
    inv_freq = ROPE_THETA ** (-jnp.arange(n_axis, dtype=F32) / n_axis)
    ang = jnp.concatenate([row[:, None] * inv_freq, col[:, None] * inv_freq], axis=-1)
    cos_g = jnp.concatenate([jnp.cos(ang), jnp.cos(ang)], axis=-1)
    sin_g = jnp.concatenate([-jnp.sin(ang), jnp.sin(ang)], axis=-1)
    reps = lanes_used // rot_dim
    cos = jnp.ones((n_lat, LANES), F32).at[:, lane0:lane0 + lanes_used].set(jnp.tile(cos_g, (1, reps)))
    sin = jnp.zeros((n_lat, LANES), F32).at[:, lane0:lane0 + lanes_used].set(jnp.tile(sin_g, (1, reps)))
    cos = jnp.concatenate([cos, jnp.ones((n_ctx, LANES), F32)], axis=0)
    sin = jnp.concatenate([sin, jnp.zeros((n_ctx, LANES), F32)], axis=0)
    return cos, sin


def _rope_tables(n_lat, n_ctx):
    cd, sd = _rope_table(n_lat, n_ctx, DIFF_DH, 0, LANES)
    cg, sg = _rope_table(n_lat, n_ctx, HEAD_DIM, 0, LANES)
    cm, sm = _rope_table(n_lat, n_ctx, MLA_ROPE, MLA_NOPE, MLA_ROPE)
    return jnp.concatenate([cd, sd, cg, sg, cm, sm], axis=1)


def _na_bias_tables(rel_bias, rows, n_ctx):
    cq = jnp.arange(GRID_W)
    col_start = jnp.clip(cq - NA_WIN_C // 2, 0, GRID_W - NA_WIN_C)
    ck = jnp.arange(GRID_W)
    col_in = (ck[None, :] >= col_start[:, None]) & (ck[None, :] < col_start[:, None] + NA_WIN_C)
    col_idx = jnp.clip(ck[None, :] - cq[:, None], -(NA_WIN_C - 1), NA_WIN_C - 1) + NA_WIN_C - 1
    col_sel = jax.nn.one_hot(col_idx.reshape(-1), 2 * NA_WIN_C - 1, dtype=F32)
    tables = []
    for r0 in (0, NA_ROWS_PER_TILE, rows - NA_ROWS_PER_TILE):
        ws = min(max(r0 - NA_WIN_R // 2, 0), rows - NA_WIN_ROWS)
        r = r0 + jnp.arange(NA_ROWS_PER_TILE)
        kr = ws + jnp.arange(NA_WIN_ROWS)
        rs = jnp.clip(r - NA_WIN_R // 2, 0, rows - NA_WIN_R)
        in_band = (kr[None, :] >= rs[:, None]) & (kr[None, :] < rs[:, None] + NA_WIN_R)
        row_idx = jnp.clip(kr[None, :] - r[:, None] + NA_WIN_R - 1, 0, 2 * NA_WIN_R - 2)
        row_sel = jax.nn.one_hot(row_idx.reshape(-1), 2 * NA_WIN_R - 1, dtype=F32)
        b = jnp.einsum("pa,lhac,qc->lhpq", row_sel, rel_bias, col_sel, precision=lax.Precision.HIGHEST)
        b = b.reshape(rel_bias.shape[:2] + (NA_ROWS_PER_TILE, NA_WIN_ROWS, GRID_W, GRID_W)).transpose(0, 1, 2, 4, 3, 5)
        ok = in_band[:, None, :, None] & col_in[None, :, None, :]
        b = jnp.where(ok[None, None], b * LOG2E, NEG_INF)
        b = b.reshape(rel_bias.shape[:2] + (TM, NA_WIN_ROWS * GRID_W))
        tables.append(jnp.concatenate([b, jnp.zeros(rel_bias.shape[:2] + (TM, n_ctx), F32)], axis=-1))
    return jnp.stack(tables, axis=1)


def _prep_weights(p):
    w_in = p["w_in"]
    depth = w_in.shape[0]
    offs = [0]
    for n in IN_SIZES:
        offs.append(offs[-1] + n)
    seg = [w_in[..., offs[j]:offs[j + 1]] for j in range(len(IN_SIZES))]
    kr_cols = _pad_last(seg[11], MLA_NOPE, LANES - MLA_NOPE - MLA_ROPE)
    rot_d = functools.partial(_swap_halves, lane0=0, used=LANES, half=DIFF_DH // 2)
    rot_g = functools.partial(_swap_halves, lane0=0, used=LANES, half=HEAD_DIM // 2)
    rot_m = functools.partial(_swap_halves, lane0=MLA_NOPE, used=MLA_ROPE, half=MLA_ROPE // 2)
    df_q, df_k, gq_k = seg[3], seg[4], seg[7]
    gq_q = seg[6].reshape(depth, D_MODEL, 2, 2, HEAD_DIM).swapaxes(2, 3).reshape(depth, D_MODEL, N_HEADS * HEAD_DIM)
    w_all = jnp.concatenate(
        [seg[0], seg[1], _pad_heads(seg[2], 4, 64),
         df_q, df_k, _pad_heads(seg[5], 4, 64), gq_q, gq_k, rot_g(gq_k), _pad_heads(seg[8], 2, 64),
         _pad_last(seg[9], 0, 256 - MLA_Q_RANK), seg[10], kr_cols,
         rot_d(df_q), rot_d(df_k), rot_g(gq_q), rot_m(kr_cols), jnp.zeros_like(kr_cols)], axis=-1).astype(BF16)

    w_qb = jnp.pad(_pad_heads(p["w_mla_qb"], N_HEADS, MLA_QK), ((0, 0), (0, 256 - MLA_Q_RANK), (0, 0)))
    w_qb = jnp.concatenate([w_qb, rot_m(w_qb)], axis=-1).astype(BF16)
    kvb = p["w_mla_kvb"].reshape(depth, MLA_KV_RANK, N_HEADS, MLA_NOPE + MLA_V)
    w_kvb = jnp.concatenate(
        [_pad_heads(kvb[..., :MLA_NOPE].reshape(depth, MLA_KV_RANK, -1), N_HEADS, MLA_NOPE),
         _pad_heads(kvb[..., MLA_NOPE:].reshape(depth, MLA_KV_RANK, -1), N_HEADS, MLA_V)], axis=-1).astype(BF16)

    def head_gain(g, n_heads, scale=1.0):
        return _pad_lanes(jnp.tile(_pad_lanes(g * scale, LANES), (1, n_heads)), 512)

    def dense_gain(g, reps, scale=1.0):
        return _pad_lanes(jnp.tile(g * scale, (1, reps)), 512)

    dg = p["diff_qk_g"]
    mq = p["mla_qk_g"]
    rows = [
        dense_gain(p["na_qk_g"][:, 0], 4, HEAD_DIM ** -0.5 * LOG2E), dense_gain(p["na_qk_g"][:, 1], 4),
        dense_gain(dg[:, 0], 8, DIFF_DH ** -0.5 * LOG2E), dense_gain(dg[:, 1], 8),
        dense_gain(p["gqa_qk_g"][:, 0], 4, HEAD_DIM ** -0.5 * LOG2E), dense_gain(p["gqa_qk_g"][:, 1], 2),
        head_gain(mq[:, 0], 4, MLA_QK ** -0.5 * LOG2E), head_gain(_pad_lanes(mq[:, 1, :MLA_NOPE], MLA_QK), 4),
        _pad_lanes(p["mla_qa_g"], 512), _pad_lanes(p["mla_kva_g"], 512),
        _pad_lanes(_pad_last(mq[:, 1, MLA_NOPE:], MLA_NOPE, 0), 512),
    ]
    rows += [rot_d(rows[2]), rot_d(rows[3]), rot_g(rows[4]), rot_g(rows[5]), rot_m(rows[6]), rot_m(rows[10])]
    gvec = jnp.stack(rows + [jnp.zeros((depth, 512), F32)] * (24 - len(rows)), axis=1)
    conv = jnp.concatenate([p["conv_w"], p["conv_b"][:, None], jnp.zeros((depth, 4, 2 * D_FF), F32)], axis=1)
    return dict(
        w_all=w_all, w_qb=w_qb, w_kvb=w_kvb, gvec=gvec,
        w_gate=p["w_gate"].astype(BF16), w_branch=p["w_branch"].astype(BF16), w_out=p["w_out"].astype(BF16),
        w_up=p["w_up"].astype(BF16), conv=conv, w_down=p["w_down"].astype(BF16),
        sub_g=p["diff_subln_g"][:, None], g1=p["norm1_g"][:, None], g2=p["norm2_g"][:, None],
    )


def kernel(x, c, ctx, c_ctx, w_mod, b_mod, norm1_g, norm2_g, w_in, na_qk_g, na_rel_bias, diff_qk_g, diff_lambda, diff_subln_g, gqa_qk_g, mla_qa_g, mla_kva_g, w_mla_qb, w_mla_kvb, mla_qk_g, w_gate, w_branch, w_out, w_up, conv_w, conv_b, w_down):
    p = dict(w_in=w_in, na_qk_g=na_qk_g, diff_qk_g=diff_qk_g, gqa_qk_g=gqa_qk_g, mla_qa_g=mla_qa_g,
             mla_kva_g=mla_kva_g, w_mla_qb=w_mla_qb, w_mla_kvb=w_mla_kvb, mla_qk_g=mla_qk_g, w_gate=w_gate,
             w_branch=w_branch, w_out=w_out, w_up=w_up, conv_w=conv_w, conv_b=conv_b, w_down=w_down,
             diff_subln_g=diff_subln_g, norm1_g=norm1_g, norm2_g=norm2_g)
    bsz, seq, d = x.shape
    n_ctx = ctx.shape[1]
    depth = w_mod.shape[0]
    assert d == D_MODEL and seq % TM == 0 and n_ctx == TM and seq // GRID_W >= NA_WIN_ROWS
    t_all = seq + n_ctx
    n_lat_tiles = seq // TM

    cc = jnp.concatenate([c, c_ctx[None], jnp.zeros((HALO - 1, d), F32)], axis=0)
    mods = _mod_all(cc, w_mod.astype(BF16), b_mod[:, None, :]).reshape(depth, bsz + HALO, N_MOD, d)
    tables = _rope_tables(seq, n_ctx)
    na_bias = _na_bias_tables(na_rel_bias, seq // GRID_W, n_ctx)
    lw = _prep_weights(p)

    split_first = depth > 1
    x_all = None if split_first else jnp.concatenate([x, ctx], axis=1)
    for i in range(depth):
        last = i == depth - 1
        n_tiles = n_lat_tiles if last else n_lat_tiles + 1
        lambda_init = 0.8 - 0.6 * math.exp(-0.3 * i)
        first_split = split_first and i == 0
        sources = ((x, 0), (ctx, 0)) if first_split else ((x_all, 0), (x_all, seq))
        qkv = _inproj(*sources, mods, lw, tables, seq, i)
        y = _attention(qkv, na_bias, diff_lambda, lw["sub_g"], seq, n_tiles, lambda_init, i)
        x_all = _merge_first(x, ctx, mods, y, lw, i) if first_split else _merge(x_all, mods, y, lw, n_tiles, i)
        x_all = _ffn(x_all, mods, lw, seq, not last, i)
    return x_all
```

```python
import functools
import math

import jax
import jax.numpy as jnp
from jax import lax
from jax.experimental import pallas as pl
from jax.experimental.pallas import tpu as pltpu

D_MODEL = 1024
GRID_W = 64
HEAD_DIM = 64
EPS = 1e-6
ROPE_THETA = 10000.0
NEG_INF = -1e30
NA_WIN_R = 8
NA_WIN_C = 16
DIFF_DH = 32
MLA_Q_RANK = 192
MLA_KV_RANK = 128
MLA_NOPE = 64
MLA_ROPE = 32
MLA_V = 64
MLA_QK = MLA_NOPE + MLA_ROPE
N_HEADS = 4
GQA_KV_HEADS = 2
N_MOD = 6
D_FF = 2816
IN_SIZES = (256, 256, 256, 256, 256, 256, 256, 128, 128, MLA_Q_RANK, MLA_KV_RANK, MLA_ROPE)

LANES = 128
TM = 256
FFN_TM = 512
ROW_TM = 512
FF_CHUNK = 256
HALO = 8
LOG2E = math.log2(math.e)
BRANCH_LAYOUT = ((256, 256, 4), (256, 256, 4), (256, 128, 2), (512, 512, 4))
NA_ROWS_PER_TILE = TM // GRID_W
NA_WIN_ROWS = NA_WIN_R + NA_ROWS_PER_TILE
VMEM_LIMIT = 56 * 1024 * 1024

BF16 = jnp.bfloat16
F32 = jnp.float32


def _cparams(sem):
    return pltpu.CompilerParams(dimension_semantics=sem, vmem_limit_bytes=VMEM_LIMIT)


def _layer_spec(a, layer):
    nd = a.ndim
    return pl.BlockSpec((1,) + a.shape[1:], lambda *_: (layer,) + (0,) * (nd - 1))


def _mod_spec(mods, layer, row_of):
    return pl.BlockSpec((1, 1) + mods.shape[2:], lambda *g: (layer, row_of(*g), 0, 0))


def _dot(a, b):
    return jnp.dot(a, b, preferred_element_type=F32)


def _sigmoid(z):
    return 1.0 / (1.0 + jnp.exp(-z))


def _row_rmsnorm(x, g):
    ms = jnp.mean(x * x, axis=-1, keepdims=True)
    return x * lax.rsqrt(ms + EPS) * g


def _mod_kernel(c_ref, w_ref, b_ref, o_ref):
    c = c_ref[...]
    act = (c * _sigmoid(c)).astype(BF16)
    o_ref[0] = _dot(act, w_ref[0]) + b_ref[0]


def _mod_all(cc, w_mod, b_mod):
    n_layers, _, n_out = w_mod.shape
    rows = cc.shape[0]
    tn = 1024
    return pl.pallas_call(
        _mod_kernel,
        grid=(n_layers, n_out // tn),
        in_specs=[
            pl.BlockSpec((rows, D_MODEL), lambda l, j: (0, 0)),
            pl.BlockSpec((1, D_MODEL, tn), lambda l, j: (l, 0, j)),
            pl.BlockSpec((1, 1, tn), lambda l, j: (l, 0, j)),
        ],
        out_specs=pl.BlockSpec((1, rows, tn), lambda l, j: (l, 0, j)),
        out_shape=jax.ShapeDtypeStruct((n_layers, rows, n_out), F32),
        compiler_params=_cparams(("arbitrary", "arbitrary")),
        name="mod_all",
    )(cc, w_mod, b_mod)


def _head_rmsnorm(y, groups, y_rot=None, g=None, g_rot=None, cos=None, sin_signed=None):
    lane = lax.broadcasted_iota(jnp.int32, (1, LANES), 1)
    outs = []
    for c in range(y.shape[1] // LANES):
        yc = y[:, c * LANES:(c + 1) * LANES]
        sq = yc * yc
        if len(groups) == 1:
            ms = jnp.sum(sq, axis=-1, keepdims=True) * (1.0 / groups[0][1])
        else:
            ms = 0.0
            for first, size in groups:
                shift = size.bit_length() - 1
                member = lax.shift_right_logical(lane, shift) == (first >> shift)
                ms = jnp.where(member, jnp.sum(jnp.where(member, sq, 0.0), axis=-1, keepdims=True) * (1.0 / size), ms)
        r = lax.rsqrt(ms + EPS)
        outs.append(yc * r if y_rot is None else
                    (yc * r * g[:, c * LANES:(c + 1) * LANES]) * cos
                    + (y_rot[:, c * LANES:(c + 1) * LANES] * r * g_rot[:, c * LANES:(c + 1) * LANES]) * sin_signed)
    return outs[0] if len(outs) == 1 else jnp.concatenate(outs, axis=1)


def _store_q(ref, y):
    ref[0] = y.astype(ref.dtype)


def _store_kt(ref, y):
    ref[0] = y.T.astype(ref.dtype)


def _store_v(ref, y):
    lane = lax.broadcasted_iota(jnp.int32, (1, 2 * LANES), 1)
    ones_col = jnp.where(jnp.bitwise_and(lane, LANES - 1) == HEAD_DIM, 1.0, 0.0)
    for j in range(y.shape[1] // (2 * LANES)):
        ref[0, j] = (y[:, 2 * j * LANES:2 * (j + 1) * LANES] + ones_col).astype(ref.dtype)


def _inproj_kernel(x_ref, mod_ref, g1_ref, w_ref, wqb_ref, wkvb_ref, gv_ref, tab_ref, *rest):
    (qna_ref, kna_ref, vna_ref, qdf_ref, kdf_ref, vdf_ref,
     qgq_ref, kgq_ref, vgq_ref, qml_ref, kml_ref, vml_ref) = rest[-3 * len(BRANCH_LAYOUT):]
    g1_ref, w_ref, wqb_ref, wkvb_ref, gv_ref = (r.at[0] for r in (g1_ref, w_ref, wqb_ref, wkvb_ref, gv_ref))
    x = x_ref[0]
    mod = mod_ref[0, 0]
    h = (_row_rmsnorm(x, g1_ref[...]) * (1.0 + mod[1:2]) + mod[0:1]).astype(BF16)

    heads64 = ((0, 64), (64, 64))
    maps32 = ((0, 32), (32, 32), (64, 32), (96, 32))
    nope_only = ((0, MLA_NOPE),)
    nope_rope = ((0, MLA_NOPE), (MLA_NOPE, MLA_ROPE))
    rope_only = ((MLA_NOPE, MLA_ROPE),)
    tab = tab_ref[...]
    cos_d, sin_d = tab[:, 0:128], tab[:, 128:256]
    cos_g, sin_g = tab[:, 256:384], tab[:, 384:512]
    cos_m, sin_m = tab[:, 512:640], tab[:, 640:768]

    def proj(lo, width):
        return _dot(h, w_ref[:, lo:lo + width])

    def gain(row, width):
        return gv_ref[row:row + 1, :width]

    def normed(y, groups, row):
        return _head_rmsnorm(y, groups) * gain(row, y.shape[1])

    def roped(y, y_rot, groups, row, row_rot, cos, sin):
        w = y.shape[1]
        return _head_rmsnorm(y, groups, y_rot=y_rot, g=gain(row, w), g_rot=gain(row_rot, w), cos=cos, sin_signed=sin)

    _store_q(qna_ref, normed(proj(0, 256), heads64, 0))
    _store_kt(kna_ref, normed(proj(256, 256), heads64, 1))
    _store_v(vna_ref, proj(512, 512))
    _store_q(qdf_ref, roped(proj(1024, 256), proj(3328, 256), maps32, 2, 11, cos_d, sin_d))
    _store_kt(kdf_ref, roped(proj(1280, 256), proj(3584, 256), maps32, 3, 12, cos_d, sin_d))
    _store_v(vdf_ref, proj(1536, 512))
    _store_q(qgq_ref, roped(proj(2048, 256), proj(3840, 256), heads64, 4, 13, cos_g, sin_g))
    k_gq = proj(2304, 256)
    _store_kt(kgq_ref, roped(k_gq[:, :LANES], k_gq[:, LANES:], heads64, 5, 14, cos_g, sin_g))
    _store_v(vgq_ref, proj(2560, 256))
    qa = proj(2816, 256)
    ms_qa = jnp.sum(qa * qa, axis=-1, keepdims=True) * (1.0 / MLA_Q_RANK)
    qa_n = (qa * lax.rsqrt(ms_qa + EPS) * gain(8, 256)).astype(BF16)
    q_ml = _dot(qa_n, wqb_ref[:, :512])
    q_ml_rot = _dot(qa_n, wqb_ref[:, 512:])
    _store_q(qml_ref, roped(q_ml, q_ml_rot, nope_rope, 6, 15, cos_m, sin_m))
    kva_kr = proj(3072, 256)
    kva_n = _row_rmsnorm(kva_kr[:, :LANES], gain(9, 128)).astype(BF16)
    kv = _dot(kva_n, wkvb_ref[...])
    k_nope = normed(kv[:, :512], nope_only, 7)
    kr_rot = proj(4096, 256)[:, :LANES]
    kr_r = roped(kva_kr[:, LANES:], kr_rot, rope_only, 10, 16, cos_m, sin_m)
    _store_kt(kml_ref, k_nope + jnp.concatenate([kr_r] * N_HEADS, axis=1))
    _store_v(vml_ref, kv[:, 512:])


def _inproj_rows(src, src_row0, mods, lw, tables, seq, layer, tm, row0, n_tiles, prev_outs=()):
    bsz = src.shape[0]
    t_all = tables.shape[0]
    ctx_row = mods.shape[1] - HALO
    tile0 = row0 // tm
    src_tile0 = src_row0 // tm
    weights = (lw["g1"], lw["w_all"], lw["w_qb"], lw["w_kvb"], lw["gvec"])
    mod_row = (lambda t, b: ctx_row) if row0 >= seq else (lambda t, b: b)

    def q_spec(width):
        return (pl.BlockSpec((1, tm, width), lambda t, b: (b, tile0 + t, 0)),
                jax.ShapeDtypeStruct((bsz, t_all, width), BF16))

    def kt_spec(width):
        return (pl.BlockSpec((1, width, tm), lambda t, b: (b, 0, tile0 + t)),
                jax.ShapeDtypeStruct((bsz, width, t_all), BF16))

    def v_spec(nh):
        return (pl.BlockSpec((1, nh // 2, tm, 2 * LANES), lambda t, b: (b, 0, tile0 + t, 0)),
                jax.ShapeDtypeStruct((bsz, nh // 2, t_all, 2 * LANES), BF16))

    outs = []
    for wq, wk, n_kv in BRANCH_LAYOUT:
        outs += [q_spec(wq), kt_spec(wk), v_spec(n_kv)]
    n_in = 3 + len(weights)
    return pl.pallas_call(
        _inproj_kernel,
        grid=(n_tiles, bsz),
        in_specs=[
            pl.BlockSpec((1, tm, D_MODEL), lambda t, b: (b, src_tile0 + t, 0)),
            _mod_spec(mods, layer, mod_row),
            *[_layer_spec(w, layer) for w in weights],
            pl.BlockSpec((tm, tables.shape[1]), lambda t, b: (tile0 + t, 0)),
            *[pl.BlockSpec(memory_space=pl.ANY) for _ in prev_outs],
        ],
        out_specs=[o[0] for o in outs],
        out_shape=[o[1] for o in outs],
        input_output_aliases={n_in + j: j for j in range(len(prev_outs))},
        compiler_params=_cparams(("arbitrary", "arbitrary")),
        name="inproj",
    )(src, mods, *weights, tables, *prev_outs)


def _inproj(latent, context, mods, lw, tables, seq, layer):
    t_all = tables.shape[0]
    outs = _inproj_rows(*latent, mods, lw, tables, seq, layer, ROW_TM, 0, seq // ROW_TM)
    return _inproj_rows(*context, mods, lw, tables, seq, layer, TM, seq, (t_all - seq) // TM, prev_outs=outs)


def _softmax_pv(q, kt, v, half, bias=None):
    s = _dot(q, kt)
    if bias is not None:
        s = s + bias
    p = jnp.exp2(s - jnp.max(s, axis=-1, keepdims=True)).astype(BF16)
    o = _dot(p, v)[:, half * LANES:(half + 1) * LANES]
    return o[:, :HEAD_DIM] * (1.0 / o[:, HEAD_DIM:HEAD_DIM + 1])


def _attn_kernel(seq, lambda_init, qna_ref, kna_ref, vna_ref, qdf_ref, kdf_ref, vdf_ref,
                 qgq_ref, kgq_ref, vgq_ref, qml_ref, kml_ref, vml_ref, bias_ref, lam_ref, sub_ref, o_ref):
    bias_ref, lam_ref, sub_ref = bias_ref.at[0], lam_ref.at[0], sub_ref.at[0]
    qi = pl.program_id(1)
    t_all = vna_ref.shape[2]
    n_lat_tiles = seq // TM
    lp = lam_ref[...]
    lam = (jnp.exp(jnp.sum(lp[0:1] * lp[1:2], axis=-1, keepdims=True))
           - jnp.exp(jnp.sum(lp[2:3] * lp[3:4], axis=-1, keepdims=True)) + lambda_init)
    lane = lax.broadcasted_iota(jnp.int32, (TM, LANES), 1)

    def q_lanes(q_ref, tile, first, size):
        shift = size.bit_length() - 1
        qf = q_ref[0, :, tile * LANES:(tile + 1) * LANES].astype(F32)
        return jnp.where(lax.shift_right_logical(lane, shift) == (first >> shift), qf, 0.0).astype(BF16)

    def tile_rows(tile):
        return slice(tile * LANES, (tile + 1) * LANES)

    def paired_heads(q_ref, kt_ref, v_ref, k_lo):
        return [_softmax_pv(q_lanes(q_ref, h // 2, HEAD_DIM * (h % 2), HEAD_DIM),
                            kt_ref[0, tile_rows(h // 2), k_lo:t_all], v_ref[0, h // 2, k_lo:t_all, :], h % 2)
                for h in range(N_HEADS)]

    def grouped_heads(k_lo):
        outs = []
        for h in range(N_HEADS):
            g = h // (N_HEADS // GQA_KV_HEADS)
            outs.append(_softmax_pv(q_lanes(qgq_ref, h % 2, HEAD_DIM * g, HEAD_DIM),
                                    kgq_ref[0, :, k_lo:t_all], vgq_ref[0, 0, k_lo:t_all, :], g))
        return outs

    def latent_heads(k_lo):
        return [_softmax_pv(qml_ref[0, :, tile_rows(h)], kml_ref[0, tile_rows(h), k_lo:t_all],
                            vml_ref[0, h // 2, k_lo:t_all, :], h % 2) for h in range(N_HEADS)]

    def diff_heads(k_lo):
        outs = []
        for h in range(N_HEADS):
            kt = kdf_ref[0, tile_rows(h // 2), k_lo:t_all]
            v = vdf_ref[0, h // 2, k_lo:t_all, :]
            first = HEAD_DIM * (h % 2)
            o = (_softmax_pv(q_lanes(qdf_ref, h // 2, first, DIFF_DH), kt, v, h % 2)
                 - lam * _softmax_pv(q_lanes(qdf_ref, h // 2, first + DIFF_DH, DIFF_DH), kt, v, h % 2))
            ms = jnp.mean(o * o, axis=-1, keepdims=True)
            outs.append(o * lax.rsqrt(ms + EPS) * sub_ref[...] * (1.0 - lambda_init))
        return outs

    def band_heads():
        ws = pl.multiple_of(jnp.clip(NA_ROWS_PER_TILE * qi - NA_WIN_R // 2, 0,
                                     seq // GRID_W - NA_WIN_ROWS) * GRID_W, TM)
        n_win = NA_WIN_ROWS * GRID_W
        outs = []
        for h in range(N_HEADS):
            rows = tile_rows(h // 2)
            kt = jnp.concatenate([kna_ref[0, rows, pl.ds(ws, n_win)], kna_ref[0, rows, seq:t_all]], axis=1)
            v = jnp.concatenate([vna_ref[0, h // 2, pl.ds(ws, n_win), :], vna_ref[0, h // 2, seq:t_all, :]], axis=0)
            outs.append(_softmax_pv(q_lanes(qna_ref, h // 2, HEAD_DIM * (h % 2), HEAD_DIM), kt, v, h % 2,
                                    bias_ref[0, h]))
        return outs

    def run(k_lo, latent):
        outs = band_heads() if latent else paired_heads(qna_ref, kna_ref, vna_ref, k_lo)
        outs += diff_heads(k_lo)
        outs += grouped_heads(k_lo)
        outs += latent_heads(k_lo)
        o_ref[0] = jnp.concatenate(outs, axis=1).astype(o_ref.dtype)

    @pl.when(qi < n_lat_tiles)
    def _():
        run(0, True)

    @pl.when(qi >= n_lat_tiles)
    def _():
        run(seq, False)


def _attention(qkv, bias, lam_p, sub_g, seq, n_q_tiles, lambda_init, layer):
    bsz, t_all, _ = qkv[0].shape
    n_lat_tiles = seq // TM
    n_cls = bias.shape[1]
    in_specs = []
    for wq, wk, n_kv in BRANCH_LAYOUT:
        in_specs += [
            pl.BlockSpec((1, TM, wq), lambda b, i: (b, i, 0)),
            pl.BlockSpec((1, wk, t_all), lambda b, i: (b, 0, 0)),
            pl.BlockSpec((1, n_kv // 2, t_all, 2 * LANES), lambda b, i: (b, 0, 0, 0)),
        ]
    in_specs += [
        pl.BlockSpec((1, 1) + bias.shape[2:],
                     lambda b, i: (layer, jnp.where(i == 0, 0, jnp.where(i >= n_lat_tiles - 1, n_cls - 1, 1)), 0, 0, 0)),
        _layer_spec(lam_p, layer),
        _layer_spec(sub_g, layer),
    ]
    width = 4 * N_HEADS * HEAD_DIM
    return pl.pallas_call(
        functools.partial(_attn_kernel, seq, lambda_init),
        grid=(bsz, n_q_tiles),
        in_specs=in_specs,
        out_specs=pl.BlockSpec((1, TM, width), lambda b, i: (b, i, 0)),
        out_shape=jax.ShapeDtypeStruct((bsz, t_all, width), BF16),
        compiler_params=_cparams(("arbitrary", "arbitrary")),
        name="attn",
    )(*qkv, bias, lam_p, sub_g)


def _merge_kernel(x_ref, mod_ref, g1_ref, y_ref, wg_ref, wb_ref, wo_ref, *rest):
    o_ref = rest[-1]
    g1_ref, wg_ref, wb_ref, wo_ref = (r.at[0] for r in (g1_ref, wg_ref, wb_ref, wo_ref))
    x = x_ref[0]
    mod = mod_ref[0, 0]
    h = (_row_rmsnorm(x, g1_ref[...]) * (1.0 + mod[1:2]) + mod[0:1]).astype(BF16)
    acc = None
    width = wb_ref.shape[1]
    for i in range(wb_ref.shape[0]):
        term = _sigmoid(_dot(h, wg_ref[i])) * _dot(y_ref[0, :, i * width:(i + 1) * width], wb_ref[i])
        acc = term if acc is None else acc + term
    o_ref[0] = x + mod[2:3] * _dot(acc.astype(BF16), wo_ref[...])


def _merge(x_all, mods, y, lw, n_tiles, layer):
    bsz, t_all, _ = x_all.shape
    ctx_tile = t_all // TM - 1
    ctx_row = mods.shape[1] - HALO
    tok = lambda w: pl.BlockSpec((1, TM, w), lambda b, t: (b, t, 0))
    return pl.pallas_call(
        _merge_kernel,
        grid=(bsz, n_tiles),
        in_specs=[
            tok(D_MODEL),
            _mod_spec(mods, layer, lambda b, t: jnp.where(t == ctx_tile, ctx_row, b)),
            _layer_spec(lw["g1"], layer), tok(y.shape[2]),
            _layer_spec(lw["w_gate"], layer), _layer_spec(lw["w_branch"], layer), _layer_spec(lw["w_out"], layer),
        ],
        out_specs=tok(D_MODEL),
        out_shape=jax.ShapeDtypeStruct(x_all.shape, F32),
        input_output_aliases={0: 0},
        compiler_params=_cparams(("arbitrary", "arbitrary")),
        name="merge",
    )(x_all, mods, lw["g1"], y, lw["w_gate"], lw["w_branch"], lw["w_out"])


def _merge_first(x, ctx, mods, y, lw, layer):
    bsz, seq, _ = x.shape
    t_all = y.shape[1]
    ctx_row = mods.shape[1] - HALO
    weights = (lw["w_gate"], lw["w_branch"], lw["w_out"])

    def call(src, tile0, n_tiles, mod_row, prev):
        tok = lambda w, t0: pl.BlockSpec((1, TM, w), lambda b, t: (b, t0 + t, 0))
        return pl.pallas_call(
            _merge_kernel,
            grid=(bsz, n_tiles),
            in_specs=[
                tok(D_MODEL, 0), _mod_spec(mods, layer, mod_row), _layer_spec(lw["g1"], layer), tok(y.shape[2], tile0),
                *[_layer_spec(w, layer) for w in weights],
                *[pl.BlockSpec(memory_space=pl.ANY) for _ in prev],
            ],
            out_specs=tok(D_MODEL, tile0),
            out_shape=jax.ShapeDtypeStruct((bsz, t_all, D_MODEL), F32),
            input_output_aliases={4 + len(weights): 0} if prev else {},
            compiler_params=_cparams(("arbitrary", "arbitrary")),
            name="merge",
        )(src, mods, lw["g1"], y, *weights, *prev)

    out = call(x, 0, seq // TM, lambda b, t: b, ())
    return call(ctx, seq // TM, (t_all - seq) // TM, lambda b, t: ctx_row, (out,))


def _ffn_kernel(seq, t_all, row0, x_ref, xp_ref, xn_ref, mod_ref, g2_ref, wu_ref, cw_ref, wd_ref, *rest):
    o_ref, act_scr = rest[-2:]
    g2_ref, wu_ref, cw_ref, wd_ref = (r.at[0] for r in (g2_ref, wu_ref, cw_ref, wd_ref))
    tm = x_ref.shape[1]
    t = pl.program_id(1)
    x = x_ref[0]
    mod = mod_ref[0, 0]
    x_ext = jnp.concatenate([x, xp_ref[0], xn_ref[0]], axis=0)
    hx = (_row_rmsnorm(x_ext, g2_ref[...]) * (1.0 + mod[4:5]) + mod[3:4]).astype(BF16)

    row = lax.broadcasted_iota(jnp.int32, (tm, FF_CHUNK), 0)
    g_first = row0 + t * tm
    has_prev = jnp.logical_and(g_first != 0, g_first != seq)
    has_next = jnp.logical_and(g_first + tm != seq, g_first + tm != t_all)
    first_row = row == 0
    last_row = row == tm - 1

    def conv(lo):
        u_ext = _dot(hx, wu_ref[:, lo:lo + FF_CHUNK])
        c = cw_ref[:, lo:lo + FF_CHUNK]
        u = u_ext[0:tm]
        prev_row = jnp.where(has_prev, u_ext[tm + HALO - 1:tm + HALO], 0.0)
        next_row = jnp.where(has_next, u_ext[tm + HALO:tm + HALO + 1], 0.0)
        up = jnp.where(first_row, prev_row, pltpu.roll(u, 1, 0))
        dn = jnp.where(last_row, next_row, pltpu.roll(u, tm - 1, 0))
        return c[3:4] + up * c[0:1] + u * c[1:2] + dn * c[2:3]

    for j in range(D_FF // FF_CHUNK):
        lo = j * FF_CHUNK
        gate = conv(lo)
        val = conv(D_FF + lo)
        act_scr[:, lo:lo + FF_CHUNK] = (gate * _sigmoid(gate) * val).astype(BF16)
    o_ref[0] = x + mod[5:6] * _dot(act_scr[...], wd_ref[...])


def _ffn_rows(x_all, mods, lw, seq, layer, tm, row0, n_tiles, out_rows, prev_out=None):
    bsz, t_all, _ = x_all.shape
    ctx_row = mods.shape[1] - HALO
    tile0 = row0 // tm
    blocks_per_tile = tm // HALO
    last_block = t_all // HALO - 1
    weights = (lw["g2"], lw["w_up"], lw["conv"], lw["w_down"])
    mod_row = (lambda b, t: ctx_row) if row0 >= seq else (lambda b, t: b)
    extra = () if prev_out is None else (prev_out,)
    return pl.pallas_call(
        functools.partial(_ffn_kernel, seq, t_all, row0),
        grid=(bsz, n_tiles),
        in_specs=[
            pl.BlockSpec((1, tm, D_MODEL), lambda b, t: (b, tile0 + t, 0)),
            pl.BlockSpec((1, HALO, D_MODEL), lambda b, t: (b, jnp.maximum((tile0 + t) * blocks_per_tile - 1, 0), 0)),
            pl.BlockSpec((1, HALO, D_MODEL),
                         lambda b, t: (b, jnp.minimum((tile0 + t + 1) * blocks_per_tile, last_block), 0)),
            _mod_spec(mods, layer, mod_row),
            *[_layer_spec(w, layer) for w in weights],
            *[pl.BlockSpec(memory_space=pl.ANY) for _ in extra],
        ],
        out_specs=pl.BlockSpec((1, tm, D_MODEL), lambda b, t: (b, tile0 + t, 0)),
        out_shape=jax.ShapeDtypeStruct((bsz, out_rows, D_MODEL), F32),
        input_output_aliases={} if prev_out is None else {4 + len(weights): 0},
        scratch_shapes=[pltpu.VMEM((tm, D_FF), BF16)],
        compiler_params=_cparams(("arbitrary", "arbitrary")),
        name="ffn",
    )(x_all, x_all, x_all, mods, *weights, *extra)


def _ffn(x_all, mods, lw, seq, with_ctx, layer):
    t_all = x_all.shape[1]
    out = _ffn_rows(x_all, mods, lw, seq, layer, FFN_TM, 0, seq // FFN_TM, t_all if with_ctx else seq)
    if with_ctx:
        out = _ffn_rows(x_all, mods, lw, seq, layer, TM, seq, (t_all - seq) // TM, t_all, prev_out=out)
    return out


def _pad_heads(w, n_heads, d):
    lead = w.shape[:-1]
    w = w.reshape(lead + (n_heads, d))
    w = jnp.pad(w, [(0, 0)] * len(lead) + [(0, 0), (0, LANES - d)])
    return w.reshape(lead + (n_heads * LANES,))


def _pad_last(a, before, after):
    return jnp.pad(a, [(0, 0)] * (a.ndim - 1) + [(before, after)])


def _pad_lanes(v, width):
    return _pad_last(v, 0, width - v.shape[-1])


def _swap_halves(w, lane0, used, half):
    lead = w.shape[:-1]
    tiles = w.reshape(lead + (w.shape[-1] // LANES, LANES))
    grp = tiles[..., lane0:lane0 + used].reshape(lead + (tiles.shape[-2], used // (2 * half), 2, half))
    grp = grp[..., ::-1, :].reshape(lead + (tiles.shape[-2], used))
    out = jnp.pad(grp, [(0, 0)] * (len(lead) + 1) + [(lane0, LANES - lane0 - used)])
    return out.reshape(w.shape)


def _rope_table(n_lat, n_ctx, rot_dim, lane0, lanes_used):
    t = jnp.arange(n_lat)
    row = (t // GRID_W).astype(F32)
    col = (t % GRID_W).astype(F32)
    n_axis = rot_dim // 4
    inv_freq = ROPE_THETA ** (-jnp.arange(n_axis, dtype=F32) / n_axis)
    ang = jnp.concatenate([row[:, None] * inv_freq, col[:, None] * inv_freq], axis=-1)
    cos_g = jnp.concatenate([jnp.cos(ang), jnp.cos(ang)], axis=-1)
    sin_g = jnp.concatenate([-jnp.sin(ang), jnp.sin(ang)], axis=-1)
    reps = lanes_used // rot_dim
    cos = jnp.ones((n_lat, LANES), F32).at[:, lane0:lane0 + lanes_used].set(jnp.tile(cos_g, (1, reps)))
    sin = jnp.zeros((n_lat, LANES), F32).at[:, lane0:lane0 + lanes_used].set(jnp.tile(sin_g, (1, reps)))
    cos = jnp.concatenate([cos, jnp.ones((n_ctx, LANES), F32)], axis=0)
    sin = jnp.concatenate([sin, jnp.zeros((n_ctx, LANES), F32)], axis=0)
    return cos, sin


def _rope_tables(n_lat, n_ctx):
    cd, sd = _rope_table(n_lat, n_ctx, DIFF_DH, 0, LANES)
    cg, sg = _rope_table(n_lat, n_ctx, HEAD_DIM, 0, LANES)
    cm, sm = _rope_table(n_lat, n_ctx, MLA_ROPE, MLA_NOPE, MLA_ROPE)
    return jnp.concatenate([cd, sd, cg, sg, cm, sm], axis=1)


def _na_bias_tables(rel_bias, rows, n_ctx):
    cq = jnp.arange(GRID_W)
    col_start = jnp.clip(cq - NA_WIN_C // 2, 0, GRID_W - NA_WIN_C)
    ck = jnp.arange(GRID_W)
    col_in = (ck[None, :] >= col_start[:, None]) & (ck[None, :] < col_start[:, None] + NA_WIN_C)
    col_idx = jnp.clip(ck[None, :] - cq[:, None], -(NA_WIN_C - 1), NA_WIN_C - 1) + NA_WIN_C - 1
    col_sel = jax.nn.one_hot(col_idx.reshape(-1), 2 * NA_WIN_C - 1, dtype=F32)
    tables = []
    for r0 in (0, NA_ROWS_PER_TILE, rows - NA_ROWS_PER_TILE):
        ws = min(max(r0 - NA_WIN_R // 2, 0), rows - NA_WIN_ROWS)
        r = r0 + jnp.arange(NA_ROWS_PER_TILE)
        kr = ws + jnp.arange(NA_WIN_ROWS)
        rs = jnp.clip(r - NA_WIN_R // 2, 0, rows - NA_WIN_R)
        in_band = (kr[None, :] >= rs[:, None]) & (kr[None, :] < rs[:, None] + NA_WIN_R)
        row_idx = jnp.clip(kr[None, :] - r[:, None] + NA_WIN_R - 1, 0, 2 * NA_WIN_R - 2)
        row_sel = jax.nn.one_hot(row_idx.reshape(-1), 2 * NA_WIN_R - 1, dtype=F32)
        b = jnp.einsum("pa,lhac,qc->lhpq", row_sel, rel_bias, col_sel, precision=lax.Precision.HIGHEST)
        b = b.reshape(rel_bias.shape[:2] + (NA_ROWS_PER_TILE, NA_WIN_ROWS, GRID_W, GRID_W)).transpose(0, 1, 2, 4, 3, 5)
        ok = in_band[:, None, :, None] & col_in[None, :, None, :]
        b = jnp.where(ok[None, None], b * LOG2E, NEG_INF)
        b = b.reshape(rel_bias.shape[:2] + (TM, NA_WIN_ROWS * GRID_W))
        tables.append(jnp.concatenate([b, jnp.zeros(rel_bias.shape[:2] + (TM, n_ctx), F32)], axis=-1))
    return jnp.stack(tables, axis=1)


def _prep_weights(p):
    w_in = p["w_in"]
    depth = w_in.shape[0]
    offs = [0]
    for n in IN_SIZES:
        offs.append(offs[-1] + n)
    seg = [w_in[..., offs[j]:offs[j + 1]] for j in range(len(IN_SIZES))]
    kr_cols = _pad_last(seg[11], MLA_NOPE, LANES - MLA_NOPE - MLA_ROPE)
    rot_d = functools.partial(_swap_halves, lane0=0, used=LANES, half=DIFF_DH // 2)
    rot_g = functools.partial(_swap_halves, lane0=0, used=LANES, half=HEAD_DIM // 2)
    rot_m = functools.partial(_swap_halves, lane0=MLA_NOPE, used=MLA_ROPE, half=MLA_ROPE // 2)
    df_q, df_k, gq_k = seg[3], seg[4], seg[7]
    gq_q = seg[6].reshape(depth, D_MODEL, 2, 2, HEAD_DIM).swapaxes(2, 3).reshape(depth, D_MODEL, N_HEADS * HEAD_DIM)
    w_all = jnp.concatenate(
        [seg[0], seg[1], _pad_heads(seg[2], 4, 64),
         df_q, df_k, _pad_heads(seg[5], 4, 64), gq_q, gq_k, rot_g(gq_k), _pad_heads(seg[8], 2, 64),
         _pad_last(seg[9], 0, 256 - MLA_Q_RANK), seg[10], kr_cols,
         rot_d(df_q), rot_d(df_k), rot_g(gq_q), rot_m(kr_cols), jnp.zeros_like(kr_cols)], axis=-1).astype(BF16)

    w_qb = jnp.pad(_pad_heads(p["w_mla_qb"], N_HEADS, MLA_QK), ((0, 0), (0, 256 - MLA_Q_RANK), (0, 0)))
    w_qb = jnp.concatenate([w_qb, rot_m(w_qb)], axis=-1).astype(BF16)
    kvb = p["w_mla_kvb"].reshape(depth, MLA_KV_RANK, N_HEADS, MLA_NOPE + MLA_V)
    w_kvb = jnp.concatenate(
        [_pad_heads(kvb[..., :MLA_NOPE].reshape(depth, MLA_KV_RANK, -1), N_HEADS, MLA_NOPE),
         _pad_heads(kvb[..., MLA_NOPE:].reshape(depth, MLA_KV_RANK, -1), N_HEADS, MLA_V)], axis=-1).astype(BF16)

    def head_gain(g, n_heads, scale=1.0):
        return _pad_lanes(jnp.tile(_pad_lanes(g * scale, LANES), (1, n_heads)), 512)

    def dense_gain(g, reps, scale=1.0):
        return _pad_lanes(jnp.tile(g * scale, (1, reps)), 512)

    dg = p["diff_qk_g"]
    mq = p["mla_qk_g"]
    rows = [
        dense_gain(p["na_qk_g"][:, 0], 4, HEAD_DIM ** -0.5 * LOG2E), dense_gain(p["na_qk_g"][:, 1], 4),
        dense_gain(dg[:, 0], 8, DIFF_DH ** -0.5 * LOG2E), dense_gain(dg[:, 1], 8),
        dense_gain(p["gqa_qk_g"][:, 0], 4, HEAD_DIM ** -0.5 * LOG2E), dense_gain(p["gqa_qk_g"][:, 1], 2),
        head_gain(mq[:, 0], 4, MLA_QK ** -0.5 * LOG2E), head_gain(_pad_lanes(mq[:, 1, :MLA_NOPE], MLA_QK), 4),
        _pad_lanes(p["mla_qa_g"], 512), _pad_lanes(p["mla_kva_g"], 512),
        _pad_lanes(_pad_last(mq[:, 1, MLA_NOPE:], MLA_NOPE, 0), 512),
    ]
    rows += [rot_d(rows[2]), rot_d(rows[3]), rot_g(rows[4]), rot_g(rows[5]), rot_m(rows[6]), rot_m(rows[10])]
    gvec = jnp.stack(rows + [jnp.zeros((depth, 512), F32)] * (24 - len(rows)), axis=1)
    conv = jnp.concatenate([p["conv_w"], p["conv_b"][:, None], jnp.zeros((depth, 4, 2 * D_FF), F32)], axis=1)
    return dict(
        w_all=w_all, w_qb=w_qb, w_kvb=w_kvb, gvec=gvec,
        w_gate=p["w_gate"].astype(BF16), w_branch=p["w_branch"].astype(BF16), w_out=p["w_out"].astype(BF16),
        w_up=p["w_up"].astype(BF16), conv=conv, w_down=p["w_down"].astype(BF16),
        sub_g=p["diff_subln_g"][:, None], g1=p["norm1_g"][:, None], g2=p["norm2_g"][:, None],
    )


def kernel(x, c, ctx, c_ctx, w_mod, b_mod, norm1_g, norm2_g, w_in, na_qk_g, na_rel_bias, diff_qk_g, diff_lambda, diff_subln_g, gqa_qk_g, mla_qa_g, mla_kva_g, w_mla_qb, w_mla_kvb, mla_qk_g, w_gate, w_branch, w_out, w_up, conv_w, conv_b, w_down):
    p = dict(w_in=w_in, na_qk_g=na_qk_g, diff_qk_g=diff_qk_g, gqa_qk_g=gqa_qk_g, mla_qa_g=mla_qa_g,
             mla_kva_g=mla_kva_g, w_mla_qb=w_mla_qb, w_mla_kvb=w_mla_kvb, mla_qk_g=mla_qk_g, w_gate=w_gate,
             w_branch=w_branch, w_out=w_out, w_up=w_up, conv_w=conv_w, conv_b=conv_b, w_down=w_down,
             diff_subln_g=diff_subln_g, norm1_g=norm1_g, norm2_g=norm2_g)
    bsz, seq, d = x.shape
    n_ctx = ctx.shape[1]
    depth = w_mod.shape[0]
    assert d == D_MODEL and seq % TM == 0 and n_ctx == TM and seq // GRID_W >= NA_WIN_ROWS
    t_all = seq + n_ctx
    n_lat_tiles = seq // TM

    cc = jnp.concatenate([c, c_ctx[None], jnp.zeros((HALO - 1, d), F32)], axis=0)
    mods = _mod_all(cc, w_mod.astype(BF16), b_mod[:, None, :]).reshape(depth, bsz + HALO, N_MOD, d)
    tables = _rope_tables(seq, n_ctx)
    na_bias = _na_bias_tables(na_rel_bias, seq // GRID_W, n_ctx)
    lw = _prep_weights(p)

    split_first = depth > 1
    x_all = None if split_first else jnp.concatenate([x, ctx], axis=1)
    for i in range(depth):
        last = i == depth - 1
        n_tiles = n_lat_tiles if last else n_lat_tiles + 1
        lambda_init = 0.8 - 0.6 * math.exp(-0.3 * i)
        first_split = split_first and i == 0
        sources = ((x, 0), (ctx, 0)) if first_split else ((x_all, 0), (x_all, seq))
        qkv = _inproj(*sources, mods, lw, tables, seq, i)
        y = _attention(qkv, na_bias, diff_lambda, lw["sub_g"], seq, n_tiles, lambda_init, i)
        x_all = _merge_first(x, ctx, mods, y, lw, i) if first_split else _merge(x_all, mods, y, lw, n_tiles, i)
        x_all = _ffn(x_all, mods, lw, seq, not last, i)
    return x_all
```

```python
import functools
import math

import jax
import jax.numpy as jnp
from jax import lax
from jax.experimental import pallas as pl
from jax.experimental.pallas import tpu as pltpu

D_MODEL = 1024
GRID_W = 64
HEAD_DIM = 64
EPS = 1e-6
ROPE_THETA = 10000.0
NEG_INF = -1e30
NA_WIN_R = 8
NA_WIN_C = 16
DIFF_DH = 32
MLA_Q_RANK = 192
MLA_KV_RANK = 128
MLA_NOPE = 64
MLA_ROPE = 32
MLA_V = 64
MLA_QK = MLA_NOPE + MLA_ROPE
N_HEADS = 4
GQA_KV_HEADS = 2
N_MOD = 6
D_FF = 2816
IN_SIZES = (256, 256, 256, 256, 256, 256, 256, 128, 128, MLA_Q_RANK, MLA_KV_RANK, MLA_ROPE)

LANES = 128
TM = 256
FFN_TM = 512
ROW_TM = 512
FF_CHUNK = 256
HALO = 8
LOG2E = math.log2(math.e)
BRANCH_LAYOUT = ((256, 256, 4), (256, 256, 4), (256, 128, 2), (512, 512, 4))
NA_ROWS_PER_TILE = TM // GRID_W
NA_WIN_ROWS = NA_WIN_R + NA_ROWS_PER_TILE
VMEM_LIMIT = 56 * 1024 * 1024

BF16 = jnp.bfloat16
F32 = jnp.float32


def _cparams(sem):
    return pltpu.CompilerParams(dimension_semantics=sem, vmem_limit_bytes=VMEM_LIMIT)


def _layer_spec(a, layer):
    nd = a.ndim
    return pl.BlockSpec((1,) + a.shape[1:], lambda *_: (layer,) + (0,) * (nd - 1))


def _mod_spec(mods, layer, row_of):
    return pl.BlockSpec((1, 1) + mods.shape[2:], lambda *g: (layer, row_of(*g), 0, 0))


def _dot(a, b):
    return jnp.dot(a, b, preferred_element_type=F32)


def _sigmoid(z):
    return 1.0 / (1.0 + jnp.exp(-z))


def _row_rmsnorm(x, g):
    ms = jnp.mean(x * x, axis=-1, keepdims=True)
    return x * lax.rsqrt(ms + EPS) * g


def _mod_kernel(c_ref, w_ref, b_ref, o_ref):
    c = c_ref[...]
    act = (c * _sigmoid(c)).astype(BF16)
    o_ref[0] = _dot(act, w_ref[0]) + b_ref[0]


def _mod_all(cc, w_mod, b_mod):
    n_layers, _, n_out = w_mod.shape
    rows = cc.shape[0]
    tn = 1024
    return pl.pallas_call(
        _mod_kernel,
        grid=(n_layers, n_out // tn),
        in_specs=[
            pl.BlockSpec((rows, D_MODEL), lambda l, j: (0, 0)),
            pl.BlockSpec((1, D_MODEL, tn), lambda l, j: (l, 0, j)),
            pl.BlockSpec((1, 1, tn), lambda l, j: (l, 0, j)),
        ],
        out_specs=pl.BlockSpec((1, rows, tn), lambda l, j: (l, 0, j)),
        out_shape=jax.ShapeDtypeStruct((n_layers, rows, n_out), F32),
        compiler_params=_cparams(("arbitrary", "arbitrary")),
        name="mod_all",
    )(cc, w_mod, b_mod)


def _head_rmsnorm(y, groups, y_rot=None, g=None, g_rot=None, cos=None, sin_signed=None):
    lane = lax.broadcasted_iota(jnp.int32, (1, LANES), 1)
    outs = []
    for c in range(y.shape[1] // LANES):
        yc = y[:, c * LANES:(c + 1) * LANES]
        sq = yc * yc
        if len(groups) == 1:
            ms = jnp.sum(sq, axis=-1, keepdims=True) * (1.0 / groups[0][1])
        else:
            ms = 0.0
            for first, size in groups:
                shift = size.bit_length() - 1
                member = lax.shift_right_logical(lane, shift) == (first >> shift)
                ms = jnp.where(member, jnp.sum(jnp.where(member, sq, 0.0), axis=-1, keepdims=True) * (1.0 / size), ms)
        r = lax.rsqrt(ms + EPS)
        outs.append(yc * r if y_rot is None else
                    (yc * r * g[:, c * LANES:(c + 1) * LANES]) * cos
                    + (y_rot[:, c * LANES:(c + 1) * LANES] * r * g_rot[:, c * LANES:(c + 1) * LANES]) * sin_signed)
    return outs[0] if len(outs) == 1 else jnp.concatenate(outs, axis=1)


def _store_q(ref, y):
    ref[0] = y.astype(ref.dtype)


def _store_kt(ref, y):
    ref[0] = y.T.astype(ref.dtype)


def _store_v(ref, y):
    lane = lax.broadcasted_iota(jnp.int32, (1, 2 * LANES), 1)
    ones_col = jnp.where(jnp.bitwise_and(lane, LANES - 1) == HEAD_DIM, 1.0, 0.0)
    for j in range(y.shape[1] // (2 * LANES)):
        ref[0, j] = (y[:, 2 * j * LANES:2 * (j + 1) * LANES] + ones_col).astype(ref.dtype)


def _inproj_kernel(x_ref, mod_ref, g1_ref, w_ref, wqb_ref, wkvb_ref, gv_ref, tab_ref, *rest):
    (qna_ref, kna_ref, vna_ref, qdf_ref, kdf_ref, vdf_ref,
     qgq_ref, kgq_ref, vgq_ref, qml_ref, kml_ref, vml_ref) = rest[-3 * len(BRANCH_LAYOUT):]
    g1_ref, w_ref, wqb_ref, wkvb_ref, gv_ref = (r.at[0] for r in (g1_ref, w_ref, wqb_ref, wkvb_ref, gv_ref))
    x = x_ref[0]
    mod = mod_ref[0, 0]
    h = (_row_rmsnorm(x, g1_ref[...]) * (1.0 + mod[1:2]) + mod[0:1]).astype(BF16)

    heads64 = ((0, 64), (64, 64))
    maps32 = ((0, 32), (32, 32), (64, 32), (96, 32))
    nope_only = ((0, MLA_NOPE),)
    nope_rope = ((0, MLA_NOPE), (MLA_NOPE, MLA_ROPE))
    rope_only = ((MLA_NOPE, MLA_ROPE),)
    tab = tab_ref[...]
    cos_d, sin_d = tab[:, 0:128], tab[:, 128:256]
    cos_g, sin_g = tab[:, 256:384], tab[:, 384:512]
    cos_m, sin_m = tab[:, 512:640], tab[:, 640:768]

    def proj(lo, width):
        return _dot(h, w_ref[:, lo:lo + width])

    def gain(row, width):
        return gv_ref[row:row + 1, :width]

    def normed(y, groups, row):
        return _head_rmsnorm(y, groups) * gain(row, y.shape[1])

    def roped(y, y_rot, groups, row, row_rot, cos, sin):
        w = y.shape[1]
        return _head_rmsnorm(y, groups, y_rot=y_rot, g=gain(row, w), g_rot=gain(row_rot, w), cos=cos, sin_signed=sin)

    _store_q(qna_ref, normed(proj(0, 256), heads64, 0))
    _store_kt(kna_ref, normed(proj(256, 256), heads64, 1))
    _store_v(vna_ref, proj(512, 512))
    _store_q(qdf_ref, roped(proj(1024, 256), proj(3328, 256), maps32, 2, 11, cos_d, sin_d))
    _store_kt(kdf_ref, roped(proj(1280, 256), proj(3584, 256), maps32, 3, 12, cos_d, sin_d))
    _store_v(vdf_ref, proj(1536, 512))
    _store_q(qgq_ref, roped(proj(2048, 256), proj(3840, 256), heads64, 4, 13, cos_g, sin_g))
    k_gq = proj(2304, 256)
    _store_kt(kgq_ref, roped(k_gq[:, :LANES], k_gq[:, LANES:], heads64, 5, 14, cos_g, sin_g))
    _store_v(vgq_ref, proj(2560, 256))
    qa = proj(2816, 256)
    ms_qa = jnp.sum(qa * qa, axis=-1, keepdims=True) * (1.0 / MLA_Q_RANK)
    qa_n = (qa * lax.rsqrt(ms_qa + EPS) * gain(8, 256)).astype(BF16)
    q_ml = _dot(qa_n, wqb_ref[:, :512])
    q_ml_rot = _dot(qa_n, wqb_ref[:, 512:])
    _store_q(qml_ref, roped(q_ml, q_ml_rot, nope_rope, 6, 15, cos_m, sin_m))
    kva_kr = proj(3072, 256)
    kva_n = _row_rmsnorm(kva_kr[:, :LANES], gain(9, 128)).astype(BF16)
    kv = _dot(kva_n, wkvb_ref[...])
    k_nope = normed(kv[:, :512], nope_only, 7)
    kr_rot = proj(4096, 256)[:, :LANES]
    kr_r = roped(kva_kr[:, LANES:], kr_rot, rope_only, 10, 16, cos_m, sin_m)
    _store_kt(kml_ref, k_nope + jnp.concatenate([kr_r] * N_HEADS, axis=1))
    _store_v(vml_ref, kv[:, 512:])


def _inproj_rows(src, src_row0, mods, lw, tables, seq, layer, tm, row0, n_tiles, prev_outs=()):
    bsz = src.shape[0]
    t_all = tables.shape[0]
    ctx_row = mods.shape[1] - HALO
    tile0 = row0 // tm
    src_tile0 = src_row0 // tm
    weights = (lw["g1"], lw["w_all"], lw["w_qb"], lw["w_kvb"], lw["gvec"])
    mod_row = (lambda t, b: ctx_row) if row0 >= seq else (lambda t, b: b)

    def q_spec(width):
        return (pl.BlockSpec((1, tm, width), lambda t, b: (b, tile0 + t, 0)),
                jax.ShapeDtypeStruct((bsz, t_all, width), BF16))

    def kt_spec(width):
        return (pl.BlockSpec((1, width, tm), lambda t, b: (b, 0, tile0 + t)),
                jax.ShapeDtypeStruct((bsz, width, t_all), BF16))

    def v_spec(nh):
        return (pl.BlockSpec((1, nh // 2, tm, 2 * LANES), lambda t, b: (b, 0, tile0 + t, 0)),
                jax.ShapeDtypeStruct((bsz, nh // 2, t_all, 2 * LANES), BF16))

    outs = []
    for wq, wk, n_kv in BRANCH_LAYOUT:
        outs += [q_spec(wq), kt_spec(wk), v_spec(n_kv)]
    n_in = 3 + len(weights)
    return pl.pallas_call(
        _inproj_kernel,
        grid=(n_tiles, bsz),
        in_specs=[
            pl.BlockSpec((1, tm, D_MODEL), lambda t, b: (b, src_tile0 + t, 0)),
            _mod_spec(mods, layer, mod_row),
            *[_layer_spec(w, layer) for w in weights],
            pl.BlockSpec((tm, tables.shape[1]), lambda t, b: (tile0 + t, 0)),
            *[pl.BlockSpec(memory_space=pl.ANY) for _ in prev_outs],
        ],
        out_specs=[o[0] for o in outs],
        out_shape=[o[1] for o in outs],
        input_output_aliases={n_in + j: j for j in range(len(prev_outs))},
        compiler_params=_cparams(("arbitrary", "arbitrary")),
        name="inproj",
    )(src, mods, *weights, tables, *prev_outs)


def _inproj(latent, context, mods, lw, tables, seq, layer):
    t_all = tables.shape[0]
    outs = _inproj_rows(*latent, mods, lw, tables, seq, layer, ROW_TM, 0, seq // ROW_TM)
    return _inproj_rows(*context, mods, lw, tables, seq, layer, TM, seq, (t_all - seq) // TM, prev_outs=outs)


def _softmax_pv(q, kt, v, half, bias=None):
    s = _dot(q, kt)
    if bias is not None:
        s = s + bias
    p = jnp.exp2(s - jnp.max(s, axis=-1, keepdims=True)).astype(BF16)
    o = _dot(p, v)[:, half * LANES:(half + 1) * LANES]
    return o[:, :HEAD_DIM] * (1.0 / o[:, HEAD_DIM:HEAD_DIM + 1])


def _attn_kernel(seq, lambda_init, qna_ref, kna_ref, vna_ref, qdf_ref, kdf_ref, vdf_ref,
                 qgq_ref, kgq_ref, vgq_ref, qml_ref, kml_ref, vml_ref, bias_ref, lam_ref, sub_ref, o_ref):
    bias_ref, lam_ref, sub_ref = bias_ref.at[0], lam_ref.at[0], sub_ref.at[0]
    qi = pl.program_id(1)
    t_all = vna_ref.shape[2]
    n_lat_tiles = seq // TM
    lp = lam_ref[...]
    lam = (jnp.exp(jnp.sum(lp[0:1] * lp[1:2], axis=-1, keepdims=True))
           - jnp.exp(jnp.sum(lp[2:3] * lp[3:4], axis=-1, keepdims=True)) + lambda_init)
    lane = lax.broadcasted_iota(jnp.int32, (TM, LANES), 1)

    def q_lanes(q_ref, tile, first, size):
        shift = size.bit_length() - 1
        qf = q_ref[0, :, tile * LANES:(tile + 1) * LANES].astype(F32)
        return jnp.where(lax.shift_right_logical(lane, shift) == (first >> shift), qf, 0.0).astype(BF16)

    def tile_rows(tile):
        return slice(tile * LANES, (tile + 1) * LANES)

    def paired_heads(q_ref, kt_ref, v_ref, k_lo):
        return [_softmax_pv(q_lanes(q_ref, h // 2, HEAD_DIM * (h % 2), HEAD_DIM),
                            kt_ref[0, tile_rows(h // 2), k_lo:t_all], v_ref[0, h // 2, k_lo:t_all, :], h % 2)
                for h in range(N_HEADS)]

    def grouped_heads(k_lo):
        outs = []
        for g in range(GQA_KV_HEADS):
            q2 = jnp.concatenate([q_lanes(qgq_ref, t, HEAD_DIM * g, HEAD_DIM) for t in range(2)], axis=0)
            o2 = _softmax_pv(q2, kgq_ref[0, :, k_lo:t_all], vgq_ref[0, 0, k_lo:t_all, :], g)
            outs += [o2[:TM], o2[TM:]]
        return outs

    def latent_heads(k_lo):
        return [_softmax_pv(qml_ref[0, :, tile_rows(h)], kml_ref[0, tile_rows(h), k_lo:t_all],
                            vml_ref[0, h // 2, k_lo:t_all, :], h % 2) for h in range(N_HEADS)]

    def diff_heads(k_lo):
        outs = []
        for h in range(N_HEADS):
            kt = kdf_ref[0, tile_rows(h // 2), k_lo:t_all]
            v = vdf_ref[0, h // 2, k_lo:t_all, :]
            first = HEAD_DIM * (h % 2)
            q2 = jnp.concatenate([q_lanes(qdf_ref, h // 2, first, DIFF_DH),
                                  q_lanes(qdf_ref, h // 2, first + DIFF_DH, DIFF_DH)], axis=0)
            o2 = _softmax_pv(q2, kt, v, h % 2)
            o = o2[:TM] - lam * o2[TM:]
            ms = jnp.mean(o * o, axis=-1, keepdims=True)
            outs.append(o * lax.rsqrt(ms + EPS) * sub_ref[...] * (1.0 - lambda_init))
        return outs

    def band_heads():
        ws = pl.multiple_of(jnp.clip(NA_ROWS_PER_TILE * qi - NA_WIN_R // 2, 0,
                                     seq // GRID_W - NA_WIN_ROWS) * GRID_W, TM)
        n_win = NA_WIN_ROWS * GRID_W
        outs = []
        for h in range(N_HEADS):
            rows = tile_rows(h // 2)
            kt = jnp.concatenate([kna_ref[0, rows, pl.ds(ws, n_win)], kna_ref[0, rows, seq:t_all]], axis=1)
            v = jnp.concatenate([vna_ref[0, h // 2, pl.ds(ws, n_win), :], vna_ref[0, h // 2, seq:t_all, :]], axis=0)
            outs.append(_softmax_pv(q_lanes(qna_ref, h // 2, HEAD_DIM * (h % 2), HEAD_DIM), kt, v, h % 2,
                                    bias_ref[0, h]))
        return outs

    def run(k_lo, latent):
        outs = band_heads() if latent else paired_heads(qna_ref, kna_ref, vna_ref, k_lo)
        outs += diff_heads(k_lo)
        outs += grouped_heads(k_lo)
        outs += latent_heads(k_lo)
        o_ref[0] = jnp.concatenate(outs, axis=1).astype(o_ref.dtype)

    @pl.when(qi < n_lat_tiles)
    def _():
        run(0, True)

    @pl.when(qi >= n_lat_tiles)
    def _():
        run(seq, False)


def _attention(qkv, bias, lam_p, sub_g, seq, n_q_tiles, lambda_init, layer):
    bsz, t_all, _ = qkv[0].shape
    n_lat_tiles = seq // TM
    n_cls = bias.shape[1]
    in_specs = []
    for wq, wk, n_kv in BRANCH_LAYOUT:
        in_specs += [
            pl.BlockSpec((1, TM, wq), lambda b, i: (b, i, 0)),
            pl.BlockSpec((1, wk, t_all), lambda b, i: (b, 0, 0)),
            pl.BlockSpec((1, n_kv // 2, t_all, 2 * LANES), lambda b, i: (b, 0, 0, 0)),
        ]
    in_specs += [
        pl.BlockSpec((1, 1) + bias.shape[2:],
                     lambda b, i: (layer, jnp.where(i == 0, 0, jnp.where(i >= n_lat_tiles - 1, n_cls - 1, 1)), 0, 0, 0)),
        _layer_spec(lam_p, layer),
        _layer_spec(sub_g, layer),
    ]
    width = 4 * N_HEADS * HEAD_DIM
    return pl.pallas_call(
        functools.partial(_attn_kernel, seq, lambda_init),
        grid=(bsz, n_q_tiles),
        in_specs=in_specs,
        out_specs=pl.BlockSpec((1, TM, width), lambda b, i: (b, i, 0)),
        out_shape=jax.ShapeDtypeStruct((bsz, t_all, width), BF16),
        compiler_params=_cparams(("arbitrary", "arbitrary")),
        name="attn",
    )(*qkv, bias, lam_p, sub_g)


def _merge_kernel(x_ref, mod_ref, g1_ref, y_ref, wg_ref, wb_ref, wo_ref, *rest):
    o_ref = rest[-1]
    g1_ref, wg_ref, wb_ref, wo_ref = (r.at[0] for r in (g1_ref, wg_ref, wb_ref, wo_ref))
    x = x_ref[0]
    mod = mod_ref[0, 0]
    h = (_row_rmsnorm(x, g1_ref[...]) * (1.0 + mod[1:2]) + mod[0:1]).astype(BF16)
    acc = None
    width = wb_ref.shape[1]
    for i in range(wb_ref.shape[0]):
        term = _sigmoid(_dot(h, wg_ref[i])) * _dot(y_ref[0, :, i * width:(i + 1) * width], wb_ref[i])
        acc = term if acc is None else acc + term
    o_ref[0] = x + mod[2:3] * _dot(acc.astype(BF16), wo_ref[...])


def _merge(x_all, mods, y, lw, n_tiles, layer):
    bsz, t_all, _ = x_all.shape
    ctx_tile = t_all // TM - 1
    ctx_row = mods.shape[1] - HALO
    tok = lambda w: pl.BlockSpec((1, TM, w), lambda b, t: (b, t, 0))
    return pl.pallas_call(
        _merge_kernel,
        grid=(bsz, n_tiles),
        in_specs=[
            tok(D_MODEL),
            _mod_spec(mods, layer, lambda b, t: jnp.where(t == ctx_tile, ctx_row, b)),
            _layer_spec(lw["g1"], layer), tok(y.shape[2]),
            _layer_spec(lw["w_gate"], layer), _layer_spec(lw["w_branch"], layer), _layer_spec(lw["w_out"], layer),
        ],
        out_specs=tok(D_MODEL),
        out_shape=jax.ShapeDtypeStruct(x_all.shape, F32),
        input_output_aliases={0: 0},
        compiler_params=_cparams(("arbitrary", "arbitrary")),
        name="merge",
    )(x_all, mods, lw["g1"], y, lw["w_gate"], lw["w_branch"], lw["w_out"])


def _merge_first(x, ctx, mods, y, lw, layer):
    bsz, seq, _ = x.shape
    t_all = y.shape[1]
    ctx_row = mods.shape[1] - HALO
    weights = (lw["w_gate"], lw["w_branch"], lw["w_out"])

    def call(src, tile0, n_tiles, mod_row, prev):
        tok = lambda w, t0: pl.BlockSpec((1, TM, w), lambda b, t: (b, t0 + t, 0))
        return pl.pallas_call(
            _merge_kernel,
            grid=(bsz, n_tiles),
            in_specs=[
                tok(D_MODEL, 0), _mod_spec(mods, layer, mod_row), _layer_spec(lw["g1"], layer), tok(y.shape[2], tile0),
                *[_layer_spec(w, layer) for w in weights],
                *[pl.BlockSpec(memory_space=pl.ANY) for _ in prev],
            ],
            out_specs=tok(D_MODEL, tile0),
            out_shape=jax.ShapeDtypeStruct((bsz, t_all, D_MODEL), F32),
            input_output_aliases={4 + len(weights): 0} if prev else {},
            compiler_params=_cparams(("arbitrary", "arbitrary")),
            name="merge",
        )(src, mods, lw["g1"], y, *weights, *prev)

    out = call(x, 0, seq // TM, lambda b, t: b, ())
    return call(ctx, seq // TM, (t_all - seq) // TM, lambda b, t: ctx_row, (out,))


def _ffn_kernel(seq, t_all, row0, x_ref, xp_ref, xn_ref, mod_ref, g2_ref, wu_ref, cw_ref, wd_ref, *rest):
    o_ref, act_scr = rest[-2:]
    g2_ref, wu_ref, cw_ref, wd_ref = (r.at[0] for r in (g2_ref, wu_ref, cw_ref, wd_ref))
    tm = x_ref.shape[1]
    t = pl.program_id(1)
    x = x_ref[0]
    mod = mod_ref[0, 0]
    x_ext = jnp.concatenate([x, xp_ref[0], xn_ref[0]], axis=0)
    hx = (_row_rmsnorm(x_ext, g2_ref[...]) * (1.0 + mod[4:5]) + mod[3:4]).astype(BF16)

    row = lax.broadcasted_iota(jnp.int32, (tm, FF_CHUNK), 0)
    g_first = row0 + t * tm
    has_prev = jnp.logical_and(g_first != 0, g_first != seq)
    has_next = jnp.logical_and(g_first + tm != seq, g_first + tm != t_all)
    first_row = row == 0
    last_row = row == tm - 1

    def conv(lo):
        u_ext = _dot(hx, wu_ref[:, lo:lo + FF_CHUNK])
        c = cw_ref[:, lo:lo + FF_CHUNK]
        u = u_ext[0:tm]
        prev_row = jnp.where(has_prev, u_ext[tm + HALO - 1:tm + HALO], 0.0)
        next_row = jnp.where(has_next, u_ext[tm + HALO:tm + HALO + 1], 0.0)
        up = jnp.where(first_row, prev_row, pltpu.roll(u, 1, 0))
        dn = jnp.where(last_row, next_row, pltpu.roll(u, tm - 1, 0))
        return c[3:4] + up * c[0:1] + u * c[1:2] + dn * c[2:3]

    for j in range(D_FF // FF_CHUNK):
        lo = j * FF_CHUNK
        gate = conv(lo)
        val = conv(D_FF + lo)
        act_scr[:, lo:lo + FF_CHUNK] = (gate * _sigmoid(gate) * val).astype(BF16)
    o_ref[0] = x + mod[5:6] * _dot(act_scr[...], wd_ref[...])


def _ffn_rows(x_all, mods, lw, seq, layer, tm, row0, n_tiles, out_rows, prev_out=None):
    bsz, t_all, _ = x_all.shape
    ctx_row = mods.shape[1] - HALO
    tile0 = row0 // tm
    blocks_per_tile = tm // HALO
    last_block = t_all // HALO - 1
    weights = (lw["g2"], lw["w_up"], lw["conv"], lw["w_down"])
    mod_row = (lambda b, t: ctx_row) if row0 >= seq else (lambda b, t: b)
    extra = () if prev_out is None else (prev_out,)
    return pl.pallas_call(
        functools.partial(_ffn_kernel, seq, t_all, row0),
        grid=(bsz, n_tiles),
        in_specs=[
            pl.BlockSpec((1, tm, D_MODEL), lambda b, t: (b, tile0 + t, 0)),
            pl.BlockSpec((1, HALO, D_MODEL), lambda b, t: (b, jnp.maximum((tile0 + t) * blocks_per_tile - 1, 0), 0)),
            pl.BlockSpec((1, HALO, D_MODEL),
                         lambda b, t: (b, jnp.minimum((tile0 + t + 1) * blocks_per_tile, last_block), 0)),
            _mod_spec(mods, layer, mod_row),
            *[_layer_spec(w, layer) for w in weights],
            *[pl.BlockSpec(memory_space=pl.ANY) for _ in extra],
        ],
        out_specs=pl.BlockSpec((1, tm, D_MODEL), lambda b, t: (b, tile0 + t, 0)),
        out_shape=jax.ShapeDtypeStruct((bsz, out_rows, D_MODEL), F32),
        input_output_aliases={} if prev_out is None else {4 + len(weights): 0},
        scratch_shapes=[pltpu.VMEM((tm, D_FF), BF16)],
        compiler_params=_cparams(("arbitrary", "arbitrary")),
        name="ffn",
    )(x_all, x_all, x_all, mods, *weights, *extra)


def _ffn(x_all, mods, lw, seq, with_ctx, layer):
    t_all = x_all.shape[1]
    out = _ffn_rows(x_all, mods, lw, seq, layer, FFN_TM, 0, seq // FFN_TM, t_all if with_ctx else seq)
    if with_ctx:
        out = _ffn_rows(x_all, mods, lw, seq, layer, TM, seq, (t_all - seq) // TM, t_all, prev_out=out)
    return out


def _pad_heads(w, n_heads, d):
    lead = w.shape[:-1]
    w = w.reshape(lead + (n_heads, d))
    w = jnp.pad(w, [(0, 0)] * len(lead) + [(0, 0), (0, LANES - d)])
    return w.reshape(lead + (n_heads * LANES,))


def _pad_last(a, before, after):
    return jnp.pad(a, [(0, 0)] * (a.ndim - 1) + [(before, after)])


def _pad_lanes(v, width):
    return _pad_last(v, 0, width - v.shape[-1])


def _swap_halves(w, lane0, used, half):
    lead = w.shape[:-1]
    tiles = w.reshape(lead + (w.shape[-1] // LANES, LANES))
    grp = tiles[..., lane0:lane0 + used].reshape(lead + (tiles.shape[-2], used // (2 * half), 2, half))
    grp = grp[..., ::-1, :].reshape(lead + (tiles.shape[-2], used))
    out = jnp.pad(grp, [(0, 0)] * (len(lead) + 1) + [(lane0, LANES - lane0 - used)])
    return out.reshape(w.shape)


def _rope_table(n_lat, n_ctx, rot_dim, lane0, lanes_used):
    t = jnp.arange(n_lat)
    row = (t // GRID_W).astype(F32)
    col = (t % GRID_W).astype(F32)
    n_axis = rot_dim // 4
    inv_freq = ROPE_THETA ** (-jnp.arange(n_axis, dtype=F32) / n_axis)
    ang = jnp.concatenate([row[:, None] * inv_freq, col[:, None] * inv_freq], axis=-1)
    cos_g = jnp.concatenate([jnp.cos(ang), jnp.cos(ang)], axis=-1)
    sin_g = jnp.concatenate([-jnp.sin(ang), jnp.sin(ang)], axis=-1)
    reps = lanes_used // rot_dim
    cos = jnp.ones((n_lat, LANES), F32).at[:, lane0:lane0 + lanes_used].set(jnp.tile(cos_g, (1, reps)))
    sin = jnp.zeros((n_lat, LANES), F32).at[:, lane0:lane0 + lanes_used].set(jnp.tile(sin_g, (1, reps)))
    cos = jnp.concatenate([cos, jnp.ones((n_ctx, LANES), F32)], axis=0)
    sin = jnp.concatenate([sin, jnp.zeros((n_ctx, LANES), F32)], axis=0)
    return cos, sin


def _rope_tables(n_lat, n_ctx):
    cd, sd = _rope_table(n_lat, n_ctx, DIFF_DH, 0, LANES)
    cg, sg = _rope_table(n_lat, n_ctx, HEAD_DIM, 0, LANES)
    cm, sm = _rope_table(n_lat, n_ctx, MLA_ROPE, MLA_NOPE, MLA_ROPE)
    return jnp.concatenate([cd, sd, cg, sg, cm, sm], axis=1)


def _na_bias_tables(rel_bias, rows, n_ctx):
    cq = jnp.arange(GRID_W)
    col_start = jnp.clip(cq - NA_WIN_C // 2, 0, GRID_W - NA_WIN_C)
    ck = jnp.arange(GRID_W)
    col_in = (ck[None, :] >= col_start[:, None]) & (ck[None, :] < col_start[:, None] + NA_WIN_C)
    col_idx = jnp.clip(ck[None, :] - cq[:, None], -(NA_WIN_C - 1), NA_WIN_C - 1) + NA_WIN_C - 1
    col_sel = jax.nn.one_hot(col_idx.reshape(-1), 2 * NA_WIN_C - 1, dtype=F32)
    tables = []
    for r0 in (0, NA_ROWS_PER_TILE, rows - NA_ROWS_PER_TILE):
        ws = min(max(r0 - NA_WIN_R // 2, 0), rows - NA_WIN_ROWS)
        r = r0 + jnp.arange(NA_ROWS_PER_TILE)
        kr = ws + jnp.arange(NA_WIN_ROWS)
        rs = jnp.clip(r - NA_WIN_R // 2, 0, rows - NA_WIN_R)
        in_band = (kr[None, :] >= rs[:, None]) & (kr[None, :] < rs[:, None] + NA_WIN_R)
        row_idx = jnp.clip(kr[None, :] - r[:, None] + NA_WIN_R - 1, 0, 2 * NA_WIN_R - 2)
        row_sel = jax.nn.one_hot(row_idx.reshape(-1), 2 * NA_WIN_R - 1, dtype=F32)
        b = jnp.einsum("pa,lhac,qc->lhpq", row_sel, rel_bias, col_sel, precision=lax.Precision.HIGHEST)
        b = b.reshape(rel_bias.shape[:2] + (NA_ROWS_PER_TILE, NA_WIN_ROWS, GRID_W, GRID_W)).transpose(0, 1, 2, 4, 3, 5)
        ok = in_band[:, None, :, None] & col_in[None, :, None, :]
        b = jnp.where(ok[None, None], b * LOG2E, NEG_INF)
        b = b.reshape(rel_bias.shape[:2] + (TM, NA_WIN_ROWS * GRID_W))
        tables.append(jnp.concatenate([b, jnp.zeros(rel_bias.shape[:2] + (TM, n_ctx), F32)], axis=-1))
    return jnp.stack(tables, axis=1)


def _prep_weights(p):
    w_in = p["w_in"]
    depth = w_in.shape[0]
    offs = [0]
    for n in IN_SIZES:
        offs.append(offs[-1] + n)
    seg = [w_in[..., offs[j]:offs[j + 1]] for j in range(len(IN_SIZES))]
    kr_cols = _pad_last(seg[11], MLA_NOPE, LANES - MLA_NOPE - MLA_ROPE)
    rot_d = functools.partial(_swap_halves, lane0=0, used=LANES, half=DIFF_DH // 2)
    rot_g = functools.partial(_swap_halves, lane0=0, used=LANES, half=HEAD_DIM // 2)
    rot_m = functools.partial(_swap_halves, lane0=MLA_NOPE, used=MLA_ROPE, half=MLA_ROPE // 2)
    df_q, df_k, gq_k = seg[3], seg[4], seg[7]
    gq_q = seg[6].reshape(depth, D_MODEL, 2, 2, HEAD_DIM).swapaxes(2, 3).reshape(depth, D_MODEL, N_HEADS * HEAD_DIM)
    w_all = jnp.concatenate(
        [seg[0], seg[1], _pad_heads(seg[2], 4, 64),
         df_q, df_k, _pad_heads(seg[5], 4, 64), gq_q, gq_k, rot_g(gq_k), _pad_heads(seg[8], 2, 64),
         _pad_last(seg[9], 0, 256 - MLA_Q_RANK), seg[10], kr_cols,
         rot_d(df_q), rot_d(df_k), rot_g(gq_q), rot_m(kr_cols), jnp.zeros_like(kr_cols)], axis=-1).astype(BF16)

    w_qb = jnp.pad(_pad_heads(p["w_mla_qb"], N_HEADS, MLA_QK), ((0, 0), (0, 256 - MLA_Q_RANK), (0, 0)))
    w_qb = jnp.concatenate([w_qb, rot_m(w_qb)], axis=-1).astype(BF16)
    kvb = p["w_mla_kvb"].reshape(depth, MLA_KV_RANK, N_HEADS, MLA_NOPE + MLA_V)
    w_kvb = jnp.concatenate(
        [_pad_heads(kvb[..., :MLA_NOPE].reshape(depth, MLA_KV_RANK, -1), N_HEADS, MLA_NOPE),
         _pad_heads(kvb[..., MLA_NOPE:].reshape(depth, MLA_KV_RANK, -1), N_HEADS, MLA_V)], axis=-1).astype(BF16)

    def head_gain(g, n_heads, scale=1.0):
        return _pad_lanes(jnp.tile(_pad_lanes(g * scale, LANES), (1, n_heads)), 512)

    def dense_gain(g, reps, scale=1.0):
        return _pad_lanes(jnp.tile(g * scale, (1, reps)), 512)

    dg = p["diff_qk_g"]
    mq = p["mla_qk_g"]
    rows = [
        dense_gain(p["na_qk_g"][:, 0], 4, HEAD_DIM ** -0.5 * LOG2E), dense_gain(p["na_qk_g"][:, 1], 4),
        dense_gain(dg[:, 0], 8, DIFF_DH ** -0.5 * LOG2E), dense_gain(dg[:, 1], 8),
        dense_gain(p["gqa_qk_g"][:, 0], 4, HEAD_DIM ** -0.5 * LOG2E), dense_gain(p["gqa_qk_g"][:, 1], 2),
        head_gain(mq[:, 0], 4, MLA_QK ** -0.5 * LOG2E), head_gain(_pad_lanes(mq[:, 1, :MLA_NOPE], MLA_QK), 4),
        _pad_lanes(p["mla_qa_g"], 512), _pad_lanes(p["mla_kva_g"], 512),
        _pad_lanes(_pad_last(mq[:, 1, MLA_NOPE:], MLA_NOPE, 0), 512),
    ]
    rows += [rot_d(rows[2]), rot_d(rows[3]), rot_g(rows[4]), rot_g(rows[5]), rot_m(rows[6]), rot_m(rows[10])]
    gvec = jnp.stack(rows + [jnp.zeros((depth, 512), F32)] * (24 - len(rows)), axis=1)
    conv = jnp.concatenate([p["conv_w"], p["conv_b"][:, None], jnp.zeros((depth, 4, 2 * D_FF), F32)], axis=1)
    return dict(
        w_all=w_all, w_qb=w_qb, w_kvb=w_kvb, gvec=gvec,
        w_gate=p["w_gate"].astype(BF16), w_branch=p["w_branch"].astype(BF16), w_out=p["w_out"].astype(BF16),
        w_up=p["w_up"].astype(BF16), conv=conv, w_down=p["w_down"].astype(BF16),
        sub_g=p["diff_subln_g"][:, None], g1=p["norm1_g"][:, None], g2=p["norm2_g"][:, None],
    )


def kernel(x, c, ctx, c_ctx, w_mod, b_mod, norm1_g, norm2_g, w_in, na_qk_g, na_rel_bias, diff_qk_g, diff_lambda, diff_subln_g, gqa_qk_g, mla_qa_g, mla_kva_g, w_mla_qb, w_mla_kvb, mla_qk_g, w_gate, w_branch, w_out, w_up, conv_w, conv_b, w_down):
    p = dict(w_in=w_in, na_qk_g=na_qk_g, diff_qk_g=diff_qk_g, gqa_qk_g=gqa_qk_g, mla_qa_g=mla_qa_g,
             mla_kva_g=mla_kva_g, w_mla_qb=w_mla_qb, w_mla_kvb=w_mla_kvb, mla_qk_g=mla_qk_g, w_gate=w_gate,
             w_branch=w_branch, w_out=w_out, w_up=w_up, conv_w=conv_w, conv_b=conv_b, w_down=w_down,
             diff_subln_g=diff_subln_g, norm1_g=norm1_g, norm2_g=norm2_g)
    bsz, seq, d = x.shape
    n_ctx = ctx.shape[1]
    depth = w_mod.shape[0]
    assert d == D_MODEL and seq % TM == 0 and n_ctx == TM and seq // GRID_W >= NA_WIN_ROWS
    t_all = seq + n_ctx
    n_lat_tiles = seq // TM

    cc = jnp.concatenate([c, c_ctx[None], jnp.zeros((HALO - 1, d), F32)], axis=0)
    mods = _mod_all(cc, w_mod.astype(BF16), b_mod[:, None, :]).reshape(depth, bsz + HALO, N_MOD, d)
    tables = _rope_tables(seq, n_ctx)
    na_bias = _na_bias_tables(na_rel_bias, seq // GRID_W, n_ctx)
    lw = _prep_weights(p)

    split_first = depth > 1
    x_all = None if split_first else jnp.concatenate([x, ctx], axis=1)
    for i in range(depth):
        last = i == depth - 1
        n_tiles = n_lat_tiles if last else n_lat_tiles + 1
        lambda_init = 0.8 - 0.6 * math.exp(-0.3 * i)
        first_split = split_first and i == 0
        sources = ((x, 0), (ctx, 0)) if first_split else ((x_all, 0), (x_all, seq))
        qkv = _inproj(*sources, mods, lw, tables, seq, i)
        y = _attention(qkv, na_bias, diff_lambda, lw["sub_g"], seq, n_tiles, lambda_init, i)
        x_all = _merge_first(x, ctx, mods, y, lw, i) if first_split else _merge(x_all, mods, y, lw, n_tiles, i)
        x_all = _ffn(x_all, mods, lw, seq, not last, i)
    return x_all
```
